```python
import jax, jax.numpy as jnp
from jax import lax
import numpy as np

D_MODEL = 2048
BATCH = 8
SEQ = 4096
DEPTH = 2

MEM_LEN = 256
N_MIXERS = 2
CHUNK = 64
EPS = 1e-6

XA_HEADS = 4
XA_WIDTH = D_MODEL // 4
XA_HEAD_DIM = XA_WIDTH // XA_HEADS
MIX_WIDTH = D_MODEL - XA_WIDTH

A_HEADS = 4
A_V_DIM = MIX_WIDTH // A_HEADS
A_QK_DIM = A_V_DIM // 2
A_CONV = 4

B_EXPAND = 128
B_HEADS = MIX_WIDTH // B_EXPAND
B_K_DIM = B_EXPAND
B_V_DIM = MIX_WIDTH // B_HEADS

D_FF = 5632
FFN_CONV = 3

N_A = (DEPTH + 1) // 2
N_B = DEPTH // 2
COLS_A = 2 * A_HEADS * A_QK_DIM + 2 * MIX_WIDTH + 2 * A_HEADS + XA_WIDTH
COLS_B = 2 * B_HEADS * B_K_DIM + 2 * MIX_WIDTH + XA_WIDTH

kernel_name = 'hybrid_mlstm_hgrn2_memxattn_convffn'


def rmsnorm(x, g):
    xf = x.astype(jnp.float32)
    y = xf * lax.rsqrt(jnp.mean(xf * xf, axis=-1, keepdims=True) + EPS)
    return (y * g.astype(jnp.float32)).astype(x.dtype)


def head_rmsnorm(o, g):
    o = o * lax.rsqrt(jnp.mean(o * o, axis=-1, keepdims=True) + EPS)
    b, s, h, d = o.shape
    return o.reshape(b, s, h * d) * g.astype(jnp.float32)


def causal_dwconv(x, w, b):
    width = w.shape[0]
    s = x.shape[1]
    xp = jnp.pad(x, ((0, 0), (width - 1, 0), (0, 0)))
    y = b
    for k in range(width):
        y = y + w[k] * xp[:, k:k + s]
    return y


def _to_chunks(t):
    b, s = t.shape[:2]
    t = t.reshape((b, s // CHUNK, CHUNK) + t.shape[2:])
    return jnp.swapaxes(jnp.moveaxis(t, 1, 0), 2, 3)


def _from_chunks(t):
    t = jnp.moveaxis(jnp.swapaxes(t, 2, 3), 0, 1)
    b, nc, l = t.shape[:3]
    return t.reshape((b, nc * l) + t.shape[3:])


def mlstm_chunkwise(q, k, v, ig, lf):
    b, s, h, dk = q.shape
    dv = v.shape[-1]
    causal = jnp.tril(jnp.ones((CHUNK, CHUNK), dtype=bool))

    def step(carry, xs):
        c_st, n_st, m_st = carry
        qc, kc, vc, ic, fc = xs
        g = jnp.cumsum(fc, axis=-1)
        a = g + m_st[..., None]
        dmat = g[..., :, None] - g[..., None, :] + ic[..., None, :]
        dmat = jnp.where(causal, dmat, -jnp.inf)
        m_row = jnp.maximum(a, jnp.max(dmat, axis=-1))
        w_inter = jnp.exp(a - m_row)
        sc = jnp.einsum('bhld,bhsd->bhls', qc, kc) * jnp.exp(dmat - m_row[..., None])
        num = (w_inter[..., None] * jnp.einsum('bhld,bhdv->bhlv', qc, c_st)
               + jnp.einsum('bhls,bhsv->bhlv', sc, vc))
        den = w_inter * jnp.einsum('bhld,bhd->bhl', qc, n_st) + jnp.sum(sc, axis=-1)
        out = num / jnp.maximum(jnp.abs(den), jnp.exp(-m_row))[..., None]
        a_end = g[..., -1] + m_st
        w_end = g[..., -1:] - g + ic
        m_new = jnp.maximum(a_end, jnp.max(w_end, axis=-1))
        decay = jnp.exp(a_end - m_new)
        ws = jnp.exp(w_end - m_new[..., None])
        c_new = decay[..., None, None] * c_st + jnp.einsum('bhl,bhld,bhlv->bhdv', ws, kc, vc)
        n_new = decay[..., None] * n_st + jnp.einsum('bhl,bhld->bhd', ws, kc)
        return (c_new, n_new, m_new), out

    init = (jnp.zeros((b, h, dk, dv), jnp.float32), jnp.zeros((b, h, dk), jnp.float32),
            jnp.zeros((b, h), jnp.float32))
    xs = (_to_chunks(q), _to_chunks(k), _to_chunks(v), _to_chunks(ig), _to_chunks(lf))
    _, out = lax.scan(step, init, xs)
    return _from_chunks(out)


def hgrn2_chunkwise(q, k, v, lf):
    b, s, h, dk = q.shape
    dv = v.shape[-1]
    causal = jnp.tril(jnp.ones((CHUNK, CHUNK), dtype=bool))

    def step(s_st, xs):
        qc, kc, vc, fc = xs
        gcum = jnp.cumsum(fc, axis=2)
        diff = gcum[:, :, :, None, :] - gcum[:, :, None, :, :]
        decay = jnp.exp(jnp.where(causal[..., None], diff, -jnp.inf))
        att = jnp.einsum('bhld,bhlsd,bhsd->bhls', qc, decay, kc)
        out = (jnp.einsum('bhld,bhdv->bhlv', qc * jnp.exp(gcum), s_st)
               + jnp.einsum('bhls,bhsv->bhlv', att, vc))
        g_end = gcum[:, :, -1]
        kd = kc * jnp.exp(g_end[:, :, None, :] - gcum)
        s_new = jnp.exp(g_end)[..., None] * s_st + jnp.einsum('bhld,bhlv->bhdv', kd, vc)
        return s_new, out

    init = jnp.zeros((b, h, dk, dv), jnp.float32)
    xs = (_to_chunks(q), _to_chunks(k), _to_chunks(v), _to_chunks(lf))
    _, out = lax.scan(step, init, xs)
    return _from_chunks(out)


def mem_cross_attention(qx, mem_n, w_kv):
    b, s, _ = qx.shape
    m = mem_n.shape[1]
    kv = mem_n @ w_kv
    km, vm = jnp.split(kv, 2, axis=-1)
    q = qx.reshape(b, s, XA_HEADS, XA_HEAD_DIM)
    km = km.reshape(b, m, XA_HEADS, XA_HEAD_DIM)
    vm = vm.reshape(b, m, XA_HEADS, XA_HEAD_DIM)
    sc = jnp.einsum('bshd,bmhd->bhsm', q, km).astype(jnp.float32) * (XA_HEAD_DIM ** -0.5)
    p = jax.nn.softmax(sc, axis=-1).astype(vm.dtype)
    return jnp.einsum('bhsm,bmhd->bshd', p, vm).reshape(b, s, XA_WIDTH)


def mlstm_layer_mixer(hn, mem_n, w_in, gate_b, conv_w, conv_b, head_g, w_kv, w_out):
    b, s, _ = hn.shape
    qk_w = A_HEADS * A_QK_DIM
    z = hn @ w_in
    qk, v, o_pre, gates, xq = jnp.split(
        z, [2 * qk_w, 2 * qk_w + MIX_WIDTH, 2 * qk_w + 2 * MIX_WIDTH,
            2 * qk_w + 2 * MIX_WIDTH + 2 * A_HEADS], axis=-1)
    qk = jax.nn.silu(causal_dwconv(qk, conv_w, conv_b))
    q, k = jnp.split(qk, 2, axis=-1)
    gates = gates.astype(jnp.float32) + gate_b.astype(jnp.float32)
    ig = gates[..., :A_HEADS]
    lf = jax.nn.log_sigmoid(gates[..., A_HEADS:])
    q = q.reshape(b, s, A_HEADS, A_QK_DIM).astype(jnp.float32)
    k = k.reshape(b, s, A_HEADS, A_QK_DIM).astype(jnp.float32) * (A_QK_DIM ** -0.5)
    v = v.reshape(b, s, A_HEADS, A_V_DIM).astype(jnp.float32)
    hh = head_rmsnorm(mlstm_chunkwise(q, k, v, ig, lf), head_g)
    y_mix = (jax.nn.sigmoid(o_pre.astype(jnp.float32)) * hh).astype(hn.dtype)
    y_mem = mem_cross_attention(xq, mem_n, w_kv)
    return jnp.concatenate([y_mix, y_mem], axis=-1) @ w_out


def hgrn2_layer_mixer(hn, mem_n, lb, w_in, head_g, w_kv, w_out):
    b, s, _ = hn.shape
    kw = B_HEADS * B_K_DIM
    z = hn @ w_in
    q, f, i, g, xq = jnp.split(z, [kw, 2 * kw, 2 * kw + MIX_WIDTH, 2 * kw + 2 * MIX_WIDTH], axis=-1)
    f = f.astype(jnp.float32)
    lf = jnp.logaddexp(jnp.log(lb), jnp.log1p(-lb) + jax.nn.log_sigmoid(f))
    kk = (1.0 - lb) * jax.nn.sigmoid(-f)
    q = jax.nn.silu(q.astype(jnp.float32)).reshape(b, s, B_HEADS, B_K_DIM)
    kk = kk.reshape(b, s, B_HEADS, B_K_DIM)
    lf = lf.reshape(b, s, B_HEADS, B_K_DIM)
    i = i.astype(jnp.float32).reshape(b, s, B_HEADS, B_V_DIM)
    o = head_rmsnorm(hgrn2_chunkwise(q, kk, i, lf), head_g)
    y_mix = (o * jax.nn.silu(g.astype(jnp.float32))).astype(hn.dtype)
    y_mem = mem_cross_attention(xq, mem_n, w_kv)
    return jnp.concatenate([y_mix, y_mem], axis=-1) @ w_out


def conv_ffn(hn, w_up, conv_w, conv_b, w_down):
    u, g = jnp.split(hn @ w_up, 2, axis=-1)
    g = causal_dwconv(g, conv_w, conv_b)
    return (jax.nn.silu(g) * u) @ w_down


def setup_inputs(seed: int = 0) -> dict:
    key = jax.random.key(seed)
    ks = jax.random.split(key, 24)
    f32 = jnp.float32

    def nrm(k, shape, scale):
        return jax.random.normal(k, shape, f32) * scale

    def gain(k, shape):
        return 1.0 + 0.02 * jax.random.normal(k, shape, f32)

    qk_w = A_HEADS * A_QK_DIM
    return {
        'x': nrm(ks[0], (BATCH, SEQ, D_MODEL), 1.0),
        'mem': nrm(ks[1], (BATCH, MEM_LEN, D_MODEL), 1.0),
        'norm_mix_g': gain(ks[2], (DEPTH, D_MODEL)),
        'norm_mem_g': gain(ks[3], (DEPTH, D_MODEL)),
        'norm_ffn_g': gain(ks[4], (DEPTH, D_MODEL)),
        'norm_out_g': gain(ks[5], (D_MODEL,)),
        'w_mem_kv': nrm(ks[6], (DEPTH, D_MODEL, 2 * XA_WIDTH), D_MODEL ** -0.5),
        'a_w_in': nrm(ks[7], (N_A, D_MODEL, COLS_A), D_MODEL ** -0.5),
        'a_gate_b': jnp.concatenate([nrm(ks[8], (N_A, A_HEADS), 0.1),
                                     3.0 + nrm(ks[9], (N_A, A_HEADS), 0.1)], axis=-1),
        'a_conv_w': nrm(ks[10], (N_A, A_CONV, 2 * qk_w), A_CONV ** -0.5),
        'a_conv_b': nrm(ks[11], (N_A, 2 * qk_w), 0.02),
        'a_head_g': gain(ks[12], (N_A, MIX_WIDTH)),
        'a_w_out': nrm(ks[13], (N_A, D_MODEL, D_MODEL), D_MODEL ** -0.5),
        'b_w_in': nrm(ks[14], (N_B, D_MODEL, COLS_B), D_MODEL ** -0.5),
        'b_lb_logits': nrm(ks[15], (DEPTH, B_HEADS * B_K_DIM), 0.1),
        'b_head_g': gain(ks[16], (N_B, MIX_WIDTH)),
        'b_w_out': nrm(ks[17], (N_B, D_MODEL, D_MODEL), D_MODEL ** -0.5),
        'ffn_w_up': nrm(ks[18], (DEPTH, D_MODEL, 2 * D_FF), D_MODEL ** -0.5),
        'ffn_conv_w': nrm(ks[19], (DEPTH, FFN_CONV, D_FF), FFN_CONV ** -0.5),
        'ffn_conv_b': nrm(ks[20], (DEPTH, D_FF), 0.02),
        'ffn_w_down': nrm(ks[21], (DEPTH, D_FF, D_MODEL), D_FF ** -0.5),
    }


def reference(x, mem, norm_mix_g, norm_mem_g, norm_ffn_g, norm_out_g, w_mem_kv,
              a_w_in, a_gate_b, a_conv_w, a_conv_b, a_head_g, a_w_out,
              b_w_in, b_lb_logits, b_head_g, b_w_out,
              ffn_w_up, ffn_conv_w, ffn_conv_b, ffn_w_down):
    lb_all = jnp.cumsum(jax.nn.softmax(b_lb_logits.astype(jnp.float32), axis=0), axis=0)
    lb_all = lb_all - lb_all[0]
    h = x
    for layer in range(DEPTH):
        hn = rmsnorm(h, norm_mix_g[layer])
        mem_n = rmsnorm(mem, norm_mem_g[layer])
        j = layer // N_MIXERS
        if layer % N_MIXERS == 0:
            y = mlstm_layer_mixer(hn, mem_n, a_w_in[j], a_gate_b[j], a_conv_w[j], a_conv_b[j],
                                  a_head_g[j], w_mem_kv[layer], a_w_out[j])
        else:
            y = hgrn2_layer_mixer(hn, mem_n, lb_all[layer], b_w_in[j], b_head_g[j],
                                  w_mem_kv[layer], b_w_out[j])
        h = h + y
        h = h + conv_ffn(rmsnorm(h, norm_ffn_g[layer]), ffn_w_up[layer], ffn_conv_w[layer],
                         ffn_conv_b[layer], ffn_w_down[layer])
    return rmsnorm(h, norm_out_g)
```

```python
import functools

import numpy as np
import jax
import jax.numpy as jnp
from jax import lax
from jax.experimental import pallas as pl
from jax.experimental.pallas import tpu as pltpu

F32 = jnp.float32
BF16 = jnp.bfloat16

D_MODEL = 2048
N_MIXERS = 2
CHUNK = 64
EPS = 1e-6

XA_HEADS = 4
XA_WIDTH = D_MODEL // 4
XA_HEAD_DIM = XA_WIDTH // XA_HEADS
MIX_WIDTH = D_MODEL - XA_WIDTH

A_HEADS = 4
A_V_DIM = MIX_WIDTH // A_HEADS
A_QK_DIM = A_V_DIM // 2
A_QK_PAD = 256
A_CONV = 4
A_AUG = 128

B_HEADS = 12
B_K_DIM = 128
B_V_DIM = 128
B_LEVELS = 6

D_FF = 5632
FFN_CONV = 3

LANES = 128
SUBLANES = 8
VMEM_LIMIT_BYTES = 56 * 1024 * 1024

_NN = (((1,), (0,)), ((), ()))
_NT = (((1,), (1,)), ((), ()))
_TN = (((0,), (0,)), ((), ()))


def _dot(a, b, dims=_NN):
    return lax.dot_general(a, b, dims, preferred_element_type=F32)


def _split3(x):
    hi = x.astype(BF16)
    r1 = x - hi.astype(F32)
    mid = r1.astype(BF16)
    r2 = r1 - mid.astype(F32)
    return hi, mid, r2.astype(BF16)


def _dot_exact_lhs(a_bf, x, dims=_NN):
    return sum(_dot(a_bf, p, dims) for p in _split3(x))


def _dot_exact_rhs(x, b_bf, dims=_NN):
    return sum(_dot(p, b_bf, dims) for p in _split3(x))


def _sigmoid(x):
    e = jnp.exp(-jnp.abs(x))
    r = 1.0 / (1.0 + e)
    return jnp.where(x >= 0, r, e * r)


def _silu(x):
    return x * _sigmoid(x)


def _log_sigmoid(x):
    return jnp.minimum(x, 0.0) - jnp.log(1.0 + jnp.exp(-jnp.abs(x)))


def _params(n_grid):
    return pltpu.CompilerParams(dimension_semantics=("arbitrary",) * n_grid,
                                vmem_limit_bytes=VMEM_LIMIT_BYTES)


def _resident(shape):
    nd = len(shape)
    return pl.BlockSpec(shape, lambda *_: (0,) * nd, pipeline_mode=pl.Buffered(1))


def _norm_matmul_kernel(x_ref, g_ref, w_ref, *out_refs, splits):
    x = x_ref[...]
    ms = jnp.mean(x * x, axis=-1, keepdims=True)
    hn = (x * lax.rsqrt(ms + EPS) * g_ref[...]).astype(BF16)
    off = 0
    for o_ref, n in zip(out_refs, splits):
        o_ref[...] = _dot(hn, w_ref[:, off:off + n]).astype(o_ref.dtype)
        off += n


def _norm_matmul(x, g, w_bf, splits, tm):
    t, d = x.shape
    n = w_bf.shape[1]
    assert sum(splits) == n and t % tm == 0
    return pl.pallas_call(
        functools.partial(_norm_matmul_kernel, splits=tuple(splits)),
        grid=(t // tm,),
        in_specs=[pl.BlockSpec((tm, d), lambda i: (i, 0)),
                  _resident((1, d)),
                  _resident((d, n))],
        out_specs=[pl.BlockSpec((tm, s), lambda i: (i, 0)) for s in splits],
        out_shape=[jax.ShapeDtypeStruct((t, s), F32) for s in splits],
        compiler_params=_params(1),
        name="norm_matmul",
    )(x, g.reshape(1, d), w_bf)


def _mlstm_kernel(q_ref, k_ref, v_ref, o_ref, gt_ref, cwq_ref, cwk_ref, cbq_ref, cbk_ref, gb_ref, hg_ref,
                  tril_ref, out_ref, qbuf, kbuf, qs, ks, c_st, m_st, *, tb):
    h = pl.program_id(1)
    t = pl.program_id(2)
    aug_w = A_V_DIM + A_AUG
    rep = aug_w // LANES

    @pl.when(t == 0)
    def _():
        qbuf[0:SUBLANES] = jnp.zeros((SUBLANES, A_QK_PAD), F32)
        kbuf[0:SUBLANES] = jnp.zeros((SUBLANES, A_QK_PAD), F32)
        c_st[...] = jnp.zeros_like(c_st)
        m_st[...] = jnp.zeros_like(m_st)

    qbuf[SUBLANES:SUBLANES + tb] = q_ref[...]
    kbuf[SUBLANES:SUBLANES + tb] = k_ref[...]

    def conv_silu(buf, w_ref, b_ref):
        acc = b_ref[...]
        for j in range(A_CONV):
            lo = SUBLANES - (A_CONV - 1) + j
            acc = acc + w_ref[j:j + 1, :] * buf[lo:lo + tb, :]
        return _silu(acc)

    qs[...] = conv_silu(qbuf, cwq_ref, cbq_ref).astype(BF16)
    ks[...] = (conv_silu(kbuf, cwk_ref, cbk_ref) * (A_QK_DIM ** -0.5)).astype(BF16)
    qbuf[0:SUBLANES] = qbuf[tb:tb + SUBLANES]
    kbuf[0:SUBLANES] = kbuf[tb:tb + SUBLANES]

    tril = tril_ref[...]
    row = lax.broadcasted_iota(jnp.int32, (CHUNK, CHUNK), 0)
    col = lax.broadcasted_iota(jnp.int32, (CHUNK, CHUNK), 1)
    causal = row >= col
    srow = lax.broadcasted_iota(jnp.int32, (LANES, 2 * LANES), 0)
    scol = lax.broadcasted_iota(jnp.int32, (LANES, 2 * LANES), 1)
    sel = jnp.where(srow == jnp.where(scol < LANES, h, A_HEADS + h), 1.0, 0.0).astype(BF16)
    frow = lax.broadcasted_iota(jnp.int32, (CHUNK, LANES), 1)
    first = jnp.where(frow == 0, 1.0, 0.0).astype(BF16)
    ones_aug = jnp.ones((CHUNK, A_AUG), BF16)

    def chunk(c, carry):
        r0 = pl.multiple_of(c * CHUNK, CHUNK)
        rows = pl.ds(r0, CHUNK)
        gates = gt_ref[rows, :] + gb_ref[...]
        g2 = _dot_exact_rhs(gates, sel)
        ic = g2[:, :LANES]
        lf = _log_sigmoid(g2[:, LANES:])
        gc = _dot_exact_lhs(tril, lf)
        m_prev = m_st[0:1, :]
        a = gc + m_prev
        r_mat = _dot_exact_lhs(first, ic - gc, _NT)
        dmat = jnp.where(causal, gc[:, :CHUNK] + r_mat, -jnp.inf)
        m_row = jnp.maximum(a, jnp.max(dmat, axis=1, keepdims=True))
        w_inter = jnp.exp(a - m_row)
        p = jnp.exp(dmat - m_row[:, :CHUNK])

        q = qs[rows, :]
        k = ks[rows, :]
        v_aug = jnp.concatenate([v_ref[rows, :].astype(BF16), ones_aug], axis=1)
        sc = _dot(q, k, _NT) * p
        inter = _dot(q, c_st[...].astype(BF16))
        intra = _dot(sc.astype(BF16), v_aug)
        tot = jnp.concatenate([w_inter] * rep, axis=1) * inter + intra
        den = tot[:, A_V_DIM:]
        inv = 1.0 / jnp.maximum(jnp.abs(den), jnp.exp(-m_row))
        hh = tot[:, :A_V_DIM] * jnp.concatenate([inv] * (A_V_DIM // LANES), axis=1)

        ms = jnp.mean(hh * hh, axis=-1, keepdims=True)
        hn = hh * lax.rsqrt(ms + EPS) * hg_ref[...]
        out_ref[rows, :] = (_sigmoid(o_ref[rows, :]) * hn).astype(out_ref.dtype)

        g_end = gc[CHUNK - 1:CHUNK, :]
        a_end = g_end + m_prev
        w_end = g_end - gc + ic
        m_new = jnp.maximum(a_end, jnp.max(w_end, axis=0, keepdims=True))
        decay = jnp.exp(a_end - m_new)
        ws = jnp.exp(w_end - m_new)
        vw = (v_aug.astype(F32) * jnp.concatenate([ws] * rep, axis=1)).astype(BF16)
        c_st[...] = jnp.concatenate([decay] * rep, axis=1) * c_st[...] + _dot(k, vw, _TN)
        m_st[...] = jnp.broadcast_to(m_new, m_st.shape)
        return carry

    lax.fori_loop(0, tb // CHUNK, chunk, 0)


def _mlstm(zq, zk, zv, zo, zg, cwq, cwk, cbq, cbk, gate_b, head_g, batch, seq, tb):
    t_total = batch * seq
    nt = seq // tb
    tril = jnp.asarray(np.tril(np.ones((CHUNK, CHUNK), np.float32)), BF16)
    row_map = lambda b, h, t: (b * nt + t, h)
    head_map = lambda b, h, t: (0, h)
    return pl.pallas_call(
        functools.partial(_mlstm_kernel, tb=tb),
        grid=(batch, A_HEADS, nt),
        in_specs=[pl.BlockSpec((tb, A_QK_PAD), row_map),
                  pl.BlockSpec((tb, A_QK_PAD), row_map),
                  pl.BlockSpec((tb, A_V_DIM), row_map),
                  pl.BlockSpec((tb, A_V_DIM), row_map),
                  pl.BlockSpec((tb, LANES), lambda b, h, t: (b * nt + t, 0)),
                  pl.BlockSpec((A_CONV, A_QK_PAD), head_map),
                  pl.BlockSpec((A_CONV, A_QK_PAD), head_map),
                  pl.BlockSpec((1, A_QK_PAD), head_map),
                  pl.BlockSpec((1, A_QK_PAD), head_map),
                  pl.BlockSpec((1, LANES), lambda b, h, t: (0, 0)),
                  pl.BlockSpec((1, A_V_DIM), head_map),
                  pl.BlockSpec((CHUNK, CHUNK), lambda b, h, t: (0, 0))],
        out_specs=pl.BlockSpec((tb, A_V_DIM), row_map),
        out_shape=jax.ShapeDtypeStruct((t_total, MIX_WIDTH), BF16),
        scratch_shapes=[pltpu.VMEM((tb + SUBLANES, A_QK_PAD), F32),
                        pltpu.VMEM((tb + SUBLANES, A_QK_PAD), F32),
                        pltpu.VMEM((tb, A_QK_PAD), BF16),
                        pltpu.VMEM((tb, A_QK_PAD), BF16),
                        pltpu.VMEM((A_QK_PAD, A_V_DIM + A_AUG), F32),
                        pltpu.VMEM((SUBLANES, LANES), F32)],
        compiler_params=_params(3),
        name="mlstm",
    )(zq, zk, zv, zo, zg, cwq, cwk, cbq, cbk, gate_b, head_g, tril)


def _hgrn2_tables():
    n = CHUNK
    mats = [np.tril(np.ones((n, n), np.float32))]
    masks = [np.eye(n, dtype=np.float32)]
    for j in range(1, B_LEVELS + 1):
        c = n >> j
        m = np.zeros((n, n), np.float32)
        pm = np.zeros((n, n), np.float32)
        for r in range(n):
            mid = (r // (2 * c)) * 2 * c + c
            if r % (2 * c) >= c:
                m[r, mid:r + 1] = 1.0
                pm[r, mid - c:mid] = 1.0
            else:
                m[r, r + 1:mid] = 1.0
        mats.append(m)
        masks.append(pm)
    return np.concatenate(mats, axis=0), np.stack(masks, axis=0)


def _hgrn2_kernel(q_ref, f_ref, i_ref, g_ref, lbl_ref, hg_ref, ex_ref, mask_ref, out_ref, s_st, *, tb, layer):
    t = pl.program_id(2)

    @pl.when(t == 0)
    def _():
        s_st[...] = jnp.zeros_like(s_st)

    lg = lbl_ref[...]
    lg = lg - jnp.max(lg, axis=0, keepdims=True)
    pe = jnp.exp(lg)
    pr = pe / jnp.sum(pe, axis=0, keepdims=True)
    c0 = pr[0:1, :]
    cl = c0
    for r in range(1, layer + 1):
        cl = cl + pr[r:r + 1, :]
    lb = cl - c0
    om = 1.0 - lb

    exm = ex_ref[...]
    rowi = lax.broadcasted_iota(jnp.int32, (CHUNK, B_K_DIM), 0)

    def chunk(c, carry):
        r0 = pl.multiple_of(c * CHUNK, CHUNK)
        rows = pl.ds(r0, CHUNK)
        fz = f_ref[rows, :]
        e = jnp.exp(-jnp.abs(fz))
        r = 1.0 / (1.0 + e)
        sig_pos = jnp.where(fz >= 0, r, e * r)
        sig_neg = jnp.where(fz >= 0, e * r, r)
        lf = jnp.log(lb + om * sig_pos)
        kk = om * sig_neg
        q = _silu(q_ref[rows, :])
        v = i_ref[rows, :].astype(BF16)

        ex = _dot_exact_lhs(exm, lf)
        gc = ex[0:CHUNK]
        g_end = gc[CHUNK - 1:CHUNK, :]

        att = mask_ref[0] * _dot(q.astype(BF16), kk.astype(BF16), _NT)
        for j in range(1, B_LEVELS + 1):
            half = CHUNK >> j
            upper = (rowi & half) != 0
            x = (jnp.where(upper, q, kk) * jnp.exp(ex[j * CHUNK:(j + 1) * CHUNK])).astype(BF16)
            att = att + mask_ref[j] * _dot(x, x, _NT)

        s_old = s_st[...]
        o = _dot(att.astype(BF16), v) + _dot((q * jnp.exp(gc)).astype(BF16), s_old.astype(BF16), _NT)
        kd = (kk * jnp.exp(g_end - gc)).astype(BF16)
        s_st[...] = s_old * jnp.exp(g_end) + _dot(v, kd, _TN)

        ms = jnp.mean(o * o, axis=-1, keepdims=True)
        on = o * lax.rsqrt(ms + EPS) * hg_ref[...]
        out_ref[rows, :] = (on * _silu(g_ref[rows, :])).astype(out_ref.dtype)
        return carry

    lax.fori_loop(0, tb // CHUNK, chunk, 0)


def _hgrn2(zq, zf, zi, zg, lb_logits, head_g, batch, seq, tb, layer):
    t_total = batch * seq
    nt = seq // tb
    depth = lb_logits.shape[0]
    ex_np, mask_np = _hgrn2_tables()
    row_map = lambda b, h, t: (b * nt + t, h)
    head_map = lambda b, h, t: (0, h)
    return pl.pallas_call(
        functools.partial(_hgrn2_kernel, tb=tb, layer=layer),
        grid=(batch, B_HEADS, nt),
        in_specs=[pl.BlockSpec((tb, B_K_DIM), row_map),
                  pl.BlockSpec((tb, B_K_DIM), row_map),
                  pl.BlockSpec((tb, B_V_DIM), row_map),
                  pl.BlockSpec((tb, B_V_DIM), row_map),
                  pl.BlockSpec((depth, B_K_DIM), head_map),
                  pl.BlockSpec((1, B_V_DIM), head_map),
                  pl.BlockSpec(ex_np.shape, lambda b, h, t: (0, 0)),
                  pl.BlockSpec(mask_np.shape, lambda b, h, t: (0, 0, 0))],
        out_specs=pl.BlockSpec((tb, B_V_DIM), row_map),
        out_shape=jax.ShapeDtypeStruct((t_total, MIX_WIDTH), BF16),
        scratch_shapes=[pltpu.VMEM((B_V_DIM, B_K_DIM), F32)],
        compiler_params=_params(3),
        name="hgrn2",
    )(zq, zf, zi, zg, lb_logits, head_g, jnp.asarray(ex_np, BF16), jnp.asarray(mask_np, F32))


def _outproj_kernel(y_ref, xq_ref, kv_ref, wo_ref, h_ref, out_ref):
    kv = kv_ref[...]
    parts = []
    for hh in range(XA_HEADS):
        lo = hh * XA_HEAD_DIM
        qh = xq_ref[:, lo:lo + XA_HEAD_DIM].astype(BF16)
        kh = kv[:, lo:lo + XA_HEAD_DIM].astype(BF16)
        vh = kv[:, XA_WIDTH + lo:XA_WIDTH + lo + XA_HEAD_DIM].astype(BF16)
        s = _dot(qh, kh, _NT) * (XA_HEAD_DIM ** -0.5)
        e = jnp.exp(s - jnp.max(s, axis=-1, keepdims=True))
        den = jnp.sum(e, axis=-1, keepdims=True)
        parts.append((_dot(e.astype(BF16), vh) / den).astype(BF16))
    y_mem = jnp.concatenate(parts, axis=1)
    out_ref[...] = (h_ref[...] + _dot(y_ref[...], wo_ref[0:MIX_WIDTH, :])
                    + _dot(y_mem, wo_ref[MIX_WIDTH:D_MODEL, :]))


def _outproj(y_mix, zxq, kv, w_out_bf, h, seq, mem_len, tm):
    t = h.shape[0]
    per_seq = seq // tm
    return pl.pallas_call(
        _outproj_kernel,
        grid=(t // tm,),
        in_specs=[pl.BlockSpec((tm, MIX_WIDTH), lambda i: (i, 0)),
                  pl.BlockSpec((tm, XA_WIDTH), lambda i: (i, 0)),
                  pl.BlockSpec((mem_len, 2 * XA_WIDTH), lambda i: (i // per_seq, 0)),
                  _resident((D_MODEL, D_MODEL)),
                  pl.BlockSpec((tm, D_MODEL), lambda i: (i, 0))],
        out_specs=pl.BlockSpec((tm, D_MODEL), lambda i: (i, 0)),
        out_shape=jax.ShapeDtypeStruct((t, D_MODEL), F32),
        compiler_params=_params(1),
        name="outproj",
    )(y_mix, zxq, kv, w_out_bf, h)


def _ffn_kernel(h_ref, g_ref, wu_ref, wg_ref, cw_ref, cb_ref, wd_ref, og_ref, out_ref,
                hn_s, gbuf, gcarry, acc, *, tm, nf, per_seq, final_norm):
    i = pl.program_id(0)
    j = pl.program_id(1)

    @pl.when(j == 0)
    def _():
        x = h_ref[...]
        ms = jnp.mean(x * x, axis=-1, keepdims=True)
        hn_s[...] = (x * lax.rsqrt(ms + EPS) * g_ref[...]).astype(BF16)

    hn = hn_s[...]
    u = _dot(hn, wu_ref[...])
    g = _dot(hn, wg_ref[...])
    prev = jnp.where(i % per_seq == 0, 0.0, gcarry[j])
    gbuf[0:SUBLANES] = prev
    gbuf[SUBLANES:SUBLANES + tm] = g
    gcarry[j] = g[tm - SUBLANES:tm, :]
    gc = cb_ref[...]
    for k in range(FFN_CONV):
        lo = SUBLANES - (FFN_CONV - 1) + k
        gc = gc + cw_ref[k:k + 1, :] * gbuf[lo:lo + tm, :]
    act = (_silu(gc) * u).astype(BF16)
    d = _dot(act, wd_ref[...])

    @pl.when(j == 0)
    def _():
        acc[...] = d

    @pl.when(j > 0)
    def _():
        acc[...] = acc[...] + d

    @pl.when(j == nf - 1)
    def _():
        y = h_ref[...] + acc[...]
        if final_norm:
            ms = jnp.mean(y * y, axis=-1, keepdims=True)
            y = y * lax.rsqrt(ms + EPS) * og_ref[...]
        out_ref[...] = y


def _ffn(h, g, w_up_bf, conv_w, conv_b, w_down_bf, out_g, seq, tm, tf, final_norm):
    t, d = h.shape
    nf = D_FF // tf
    per_seq = seq // tm
    return pl.pallas_call(
        functools.partial(_ffn_kernel, tm=tm, nf=nf, per_seq=per_seq, final_norm=final_norm),
        grid=(t // tm, nf),
        in_specs=[pl.BlockSpec((tm, d), lambda i, j: (i, 0)),
                  pl.BlockSpec((1, d), lambda i, j: (0, 0)),
                  pl.BlockSpec((d, tf), lambda i, j: (0, j)),
                  pl.BlockSpec((d, tf), lambda i, j: (0, nf + j)),
                  pl.BlockSpec((FFN_CONV, tf), lambda i, j: (0, j)),
                  pl.BlockSpec((1, tf), lambda i, j: (0, j)),
                  pl.BlockSpec((tf, d), lambda i, j: (j, 0)),
                  pl.BlockSpec((1, d), lambda i, j: (0, 0))],
        out_specs=pl.BlockSpec((tm, d), lambda i, j: (i, 0)),
        out_shape=jax.ShapeDtypeStruct((t, d), F32),
        scratch_shapes=[pltpu.VMEM((tm, d), BF16),
                        pltpu.VMEM((tm + SUBLANES, tf), F32),
                        pltpu.VMEM((nf, SUBLANES, tf), F32),
                        pltpu.VMEM((tm, d), F32)],
        compiler_params=_params(2),
        name="ffn",
    )(h, g.reshape(1, d), w_up_bf, w_up_bf, conv_w, conv_b.reshape(1, D_FF), w_down_bf, out_g.reshape(1, d))


def _pad_heads(w, heads, dim, pad):
    lead = w.shape[:-1]
    w = w.reshape(lead + (heads, dim))
    w = jnp.pad(w, [(0, 0)] * len(lead) + [(0, 0), (0, pad - dim)])
    return w.reshape(lead + (heads * pad,))


def _mlstm_weights(w_in, gate_b, conv_w, conv_b):
    qk_w = A_HEADS * A_QK_DIM
    o0 = 2 * qk_w
    wq = _pad_heads(w_in[:, :qk_w], A_HEADS, A_QK_DIM, A_QK_PAD)
    wk = _pad_heads(w_in[:, qk_w:o0], A_HEADS, A_QK_DIM, A_QK_PAD)
    wv = w_in[:, o0:o0 + MIX_WIDTH]
    wo = w_in[:, o0 + MIX_WIDTH:o0 + 2 * MIX_WIDTH]
    wg = jnp.pad(w_in[:, o0 + 2 * MIX_WIDTH:o0 + 2 * MIX_WIDTH + 2 * A_HEADS], ((0, 0), (0, LANES - 2 * A_HEADS)))
    wx = w_in[:, o0 + 2 * MIX_WIDTH + 2 * A_HEADS:]
    w_all = jnp.concatenate([wq, wk, wv, wo, wx, wg], axis=1).astype(BF16)
    splits = (A_HEADS * A_QK_PAD, A_HEADS * A_QK_PAD, MIX_WIDTH, MIX_WIDTH, XA_WIDTH, LANES)
    cwq = _pad_heads(conv_w[:, :qk_w], A_HEADS, A_QK_DIM, A_QK_PAD)
    cwk = _pad_heads(conv_w[:, qk_w:], A_HEADS, A_QK_DIM, A_QK_PAD)
    cbq = _pad_heads(conv_b[None, :qk_w], A_HEADS, A_QK_DIM, A_QK_PAD)
    cbk = _pad_heads(conv_b[None, qk_w:], A_HEADS, A_QK_DIM, A_QK_PAD)
    gb = jnp.pad(gate_b[None, :], ((0, 0), (0, LANES - 2 * A_HEADS)))
    return w_all, splits, cwq, cwk, cbq, cbk, gb


def _pick(n, cap):
    b = min(n, cap)
    while n % b:
        b //= 2
    return b


def kernel(x, mem, norm_mix_g, norm_mem_g, norm_ffn_g, norm_out_g, w_mem_kv, a_w_in, a_gate_b, a_conv_w, a_conv_b, a_head_g, a_w_out, b_w_in, b_lb_logits, b_head_g, b_w_out, ffn_w_up, ffn_conv_w, ffn_conv_b, ffn_w_down):
    batch, seq, d = x.shape
    mem_len = mem.shape[1]
    depth = norm_mix_g.shape[0]
    t = batch * seq
    tm_proj = _pick(seq, 256)
    tm_out = _pick(seq, 512)
    tm_ffn = _pick(seq, 512)
    tb = _pick(seq, 256)

    h = x.reshape(t, d)
    mem2 = mem.reshape(batch * mem_len, d)
    for layer in range(depth):
        j = layer // N_MIXERS
        (kv,) = _norm_matmul(mem2, norm_mem_g[layer], w_mem_kv[layer].astype(BF16), (2 * XA_WIDTH,),
                             _pick(mem_len, 256))
        if layer % N_MIXERS == 0:
            w_all, splits, cwq, cwk, cbq, cbk, gb = _mlstm_weights(a_w_in[j], a_gate_b[j], a_conv_w[j], a_conv_b[j])
            zq, zk, zv, zo, zxq, zg = _norm_matmul(h, norm_mix_g[layer], w_all, splits, tm_proj)
            y_mix = _mlstm(zq, zk, zv, zo, zg, cwq, cwk, cbq, cbk, gb, a_head_g[j].reshape(1, MIX_WIDTH),
                           batch, seq, tb)
            w_out = a_w_out[j]
        else:
            kw = B_HEADS * B_K_DIM
            splits = (kw, kw, MIX_WIDTH, MIX_WIDTH, XA_WIDTH)
            zq, zf, zi, zg, zxq = _norm_matmul(h, norm_mix_g[layer], b_w_in[j].astype(BF16), splits, tm_proj)
            y_mix = _hgrn2(zq, zf, zi, zg, b_lb_logits, b_head_g[j].reshape(1, MIX_WIDTH), batch, seq, tb, layer)
            w_out = b_w_out[j]
        h = _outproj(y_mix, zxq, kv, w_out.astype(BF16), h, seq, mem_len, tm_out)
        h = _ffn(h, norm_ffn_g[layer], ffn_w_up[layer].astype(BF16), ffn_conv_w[layer], ffn_conv_b[layer],
                 ffn_w_down[layer].astype(BF16), norm_out_g, seq, tm_ffn, 512, layer == depth - 1)
    return h.reshape(batch, seq, d)
```

```python
import functools

import numpy as np
import jax
import jax.numpy as jnp
from jax import lax
from jax.experimental import pallas as pl
from jax.experimental.pallas import tpu as pltpu

F32 = jnp.float32
BF16 = jnp.bfloat16

D_MODEL = 2048
N_MIXERS = 2
CHUNK = 64
EPS = 1e-6

XA_HEADS = 4
XA_WIDTH = D_MODEL // 4
XA_HEAD_DIM = XA_WIDTH // XA_HEADS
MIX_WIDTH = D_MODEL - XA_WIDTH

A_HEADS = 4
A_V_DIM = MIX_WIDTH // A_HEADS
A_QK_DIM = A_V_DIM // 2
A_QK_PAD = 256
A_CONV = 4
A_AUG = 128
A_HEADS_PER_STEP = 4

B_HEADS = 12
B_K_DIM = 128
B_V_DIM = 128
B_LEVELS = 6
B_HEADS_PER_STEP = 4

D_FF = 5632
FFN_CONV = 3

LANES = 128
SUBLANES = 8
VMEM_LIMIT_BYTES = 56 * 1024 * 1024

_NN = (((1,), (0,)), ((), ()))
_NT = (((1,), (1,)), ((), ()))
_TN = (((0,), (0,)), ((), ()))


def _dot(a, b, dims=_NN):
    return lax.dot_general(a, b, dims, preferred_element_type=F32)


def _split2(x):
    hi = x.astype(BF16)
    return hi, (x - hi.astype(F32)).astype(BF16)


def _dot_exact_lhs(a_bf, x, dims=_NN):
    return sum(_dot(a_bf, p, dims) for p in _split2(x))


def _dot_exact_rhs(x, b_bf, dims=_NN):
    return sum(_dot(p, b_bf, dims) for p in _split2(x))


def _sigmoid(x):
    e = jnp.exp(-jnp.abs(x))
    r = 1.0 / (1.0 + e)
    return jnp.where(x >= 0, r, e * r)


def _silu(x):
    return x * _sigmoid(x)


def _log_sigmoid(x):
    return jnp.minimum(x, 0.0) - jnp.log(1.0 + jnp.exp(-jnp.abs(x)))


def _params(n_grid):
    return pltpu.CompilerParams(dimension_semantics=("arbitrary",) * n_grid,
                                vmem_limit_bytes=VMEM_LIMIT_BYTES)


def _resident(shape):
    nd = len(shape)
    return pl.BlockSpec(shape, lambda *_: (0,) * nd, pipeline_mode=pl.Buffered(1))


def _norm_matmul_kernel(x_ref, g_ref, w_ref, *out_refs, splits):
    x = x_ref[...]
    ms = jnp.mean(x * x, axis=-1, keepdims=True)
    hn = (x * lax.rsqrt(ms + EPS) * g_ref[...]).astype(BF16)
    off = 0
    for o_ref, n in zip(out_refs, splits):
        o_ref[...] = _dot(hn, w_ref[:, off:off + n]).astype(o_ref.dtype)
        off += n


def _norm_matmul(x, g, w_bf, splits, tm):
    t, d = x.shape
    n = w_bf.shape[1]
    assert sum(splits) == n and t % tm == 0
    return pl.pallas_call(
        functools.partial(_norm_matmul_kernel, splits=tuple(splits)),
        grid=(t // tm,),
        in_specs=[pl.BlockSpec((tm, d), lambda i: (i, 0)),
                  _resident((1, d)),
                  _resident((d, n))],
        out_specs=[pl.BlockSpec((tm, s), lambda i: (i, 0)) for s in splits],
        out_shape=[jax.ShapeDtypeStruct((t, s), F32) for s in splits],
        compiler_params=_params(1),
        name="norm_matmul",
    )(x, g.reshape(1, d), w_bf)


def _mlstm_kernel(q_ref, k_ref, v_ref, o_ref, gt_ref, cwq_ref, cwk_ref, cbq_ref, cbk_ref, gb_ref, hg_ref,
                  tril_ref, out_ref, qbuf, kbuf, qs, ks, c_st, m_st, *, tb, hb):
    hgrp = pl.program_id(1)
    t = pl.program_id(2)
    aug_w = A_V_DIM + A_AUG
    rep = aug_w // LANES

    @pl.when(t == 0)
    def _():
        qbuf[0:SUBLANES] = jnp.zeros((SUBLANES, hb * A_QK_PAD), F32)
        kbuf[0:SUBLANES] = jnp.zeros((SUBLANES, hb * A_QK_PAD), F32)
        c_st[...] = jnp.zeros_like(c_st)
        m_st[...] = jnp.zeros_like(m_st)

    qbuf[SUBLANES:SUBLANES + tb] = q_ref[...]
    kbuf[SUBLANES:SUBLANES + tb] = k_ref[...]

    def conv_silu(buf, w_ref, b_ref):
        acc = b_ref[...]
        for j in range(A_CONV):
            lo = SUBLANES - (A_CONV - 1) + j
            acc = acc + w_ref[j:j + 1, :] * buf[lo:lo + tb, :]
        return _silu(acc)

    qs[...] = conv_silu(qbuf, cwq_ref, cbq_ref).astype(BF16)
    ks[...] = (conv_silu(kbuf, cwk_ref, cbk_ref) * (A_QK_DIM ** -0.5)).astype(BF16)
    qbuf[0:SUBLANES] = qbuf[tb:tb + SUBLANES]
    kbuf[0:SUBLANES] = kbuf[tb:tb + SUBLANES]

    tril = tril_ref[...]
    row = lax.broadcasted_iota(jnp.int32, (CHUNK, CHUNK), 0)
    col = lax.broadcasted_iota(jnp.int32, (CHUNK, CHUNK), 1)
    causal = row >= col
    wide = hb * LANES
    srow = lax.broadcasted_iota(jnp.int32, (LANES, 2 * wide), 0)
    scol = lax.broadcasted_iota(jnp.int32, (LANES, 2 * wide), 1)
    blk = scol // LANES
    want = jnp.where(blk < hb, hgrp * hb + blk, A_HEADS + hgrp * hb + (blk - hb))
    sel = jnp.where(srow == want, 1.0, 0.0).astype(BF16)
    frow = lax.broadcasted_iota(jnp.int32, (CHUNK, LANES), 1)
    first = jnp.where(frow == 0, 1.0, 0.0).astype(BF16)
    ones_aug = jnp.ones((CHUNK, A_AUG), BF16)
    heads = range(hb)

    def lanes_of(hh):
        return slice(hh * LANES, (hh + 1) * LANES)

    def chunk(c, carry):
        rows = pl.ds(pl.multiple_of(c * CHUNK, CHUNK), CHUNK)
        gates = gt_ref[rows, :] + gb_ref[...]
        g2 = _dot_exact_rhs(gates, sel)
        ic = g2[:, :wide]
        lf = _log_sigmoid(g2[:, wide:])
        gc = _dot_exact_lhs(tril, lf)
        m_prev = m_st[0:1, :]
        a = gc + m_prev
        icg = ic - gc
        r_mat = [_dot_exact_lhs(first, icg[:, lanes_of(hh)], _NT) for hh in heads]
        dmat = [jnp.where(causal, gc[:, hh * LANES:hh * LANES + CHUNK] + r_mat[hh], -jnp.inf) for hh in heads]
        mx = jnp.concatenate([jnp.broadcast_to(jnp.max(dmat[hh], axis=1, keepdims=True), (CHUNK, LANES))
                              for hh in heads], axis=1)
        m_row = jnp.maximum(a, mx)
        w_inter = jnp.exp(a - m_row)
        e_neg = jnp.exp(-m_row)
        p = [jnp.exp(dmat[hh] - m_row[:, hh * LANES:hh * LANES + CHUNK]) for hh in heads]

        g_end = gc[CHUNK - 1:CHUNK, :]
        a_end = g_end + m_prev
        w_end = g_end - gc + ic
        m_new = jnp.maximum(a_end, jnp.max(w_end, axis=0, keepdims=True))
        decay = jnp.exp(a_end - m_new)
        ws = jnp.exp(w_end - m_new)
        m_st[...] = jnp.broadcast_to(m_new, (SUBLANES, wide))

        qk_lanes = [slice(hh * A_QK_PAD, (hh + 1) * A_QK_PAD) for hh in heads]
        v_lanes = [slice(hh * A_V_DIM, (hh + 1) * A_V_DIM) for hh in heads]
        q = [qs[rows, qk_lanes[hh]] for hh in heads]
        k = [ks[rows, qk_lanes[hh]] for hh in heads]
        v_aug = [jnp.concatenate([v_ref[rows, v_lanes[hh]].astype(BF16), ones_aug], axis=1) for hh in heads]
        qk = [_dot(q[hh], k[hh], _NT) for hh in heads]
        c_old = [c_st[hh] for hh in heads]
        inter = [_dot(q[hh], c_old[hh].astype(BF16)) for hh in heads]
        intra = [_dot((qk[hh] * p[hh]).astype(BF16), v_aug[hh]) for hh in heads]
        for hh in heads:
            vw = (v_aug[hh].astype(F32) * jnp.concatenate([ws[:, lanes_of(hh)]] * rep, axis=1)).astype(BF16)
            c_st[hh] = (jnp.concatenate([decay[:, lanes_of(hh)]] * rep, axis=1) * c_old[hh]
                        + _dot(k[hh], vw, _TN))
        for hh in heads:
            tot = jnp.concatenate([w_inter[:, lanes_of(hh)]] * rep, axis=1) * inter[hh] + intra[hh]
            den = tot[:, A_V_DIM:]
            inv = 1.0 / jnp.maximum(jnp.abs(den), e_neg[:, lanes_of(hh)])
            h_out = tot[:, :A_V_DIM] * jnp.concatenate([inv] * (A_V_DIM // LANES), axis=1)
            ms = jnp.mean(h_out * h_out, axis=-1, keepdims=True)
            hn = h_out * lax.rsqrt(ms + EPS) * hg_ref[:, v_lanes[hh]]
            out_ref[rows, v_lanes[hh]] = (_sigmoid(o_ref[rows, v_lanes[hh]]) * hn).astype(out_ref.dtype)
        return carry

    lax.fori_loop(0, tb // CHUNK, chunk, 0)


def _mlstm(zq, zk, zv, zo, zg, cwq, cwk, cbq, cbk, gate_b, head_g, batch, seq, tb, hb):
    t_total = batch * seq
    nt = seq // tb
    tril = jnp.asarray(np.tril(np.ones((CHUNK, CHUNK), np.float32)), BF16)
    row_map = lambda b, h, t: (b * nt + t, h)
    head_map = lambda b, h, t: (0, h)
    fixed = lambda b, h, t: (0, 0)
    return pl.pallas_call(
        functools.partial(_mlstm_kernel, tb=tb, hb=hb),
        grid=(batch, A_HEADS // hb, nt),
        in_specs=[pl.BlockSpec((tb, hb * A_QK_PAD), row_map),
                  pl.BlockSpec((tb, hb * A_QK_PAD), row_map),
                  pl.BlockSpec((tb, hb * A_V_DIM), row_map),
                  pl.BlockSpec((tb, hb * A_V_DIM), row_map),
                  pl.BlockSpec((tb, LANES), lambda b, h, t: (b * nt + t, 0)),
                  pl.BlockSpec((A_CONV, hb * A_QK_PAD), head_map),
                  pl.BlockSpec((A_CONV, hb * A_QK_PAD), head_map),
                  pl.BlockSpec((1, hb * A_QK_PAD), head_map),
                  pl.BlockSpec((1, hb * A_QK_PAD), head_map),
                  pl.BlockSpec((1, LANES), fixed),
                  pl.BlockSpec((1, hb * A_V_DIM), head_map),
                  pl.BlockSpec((CHUNK, CHUNK), fixed)],
        out_specs=pl.BlockSpec((tb, hb * A_V_DIM), row_map),
        out_shape=jax.ShapeDtypeStruct((t_total, MIX_WIDTH), BF16),
        scratch_shapes=[pltpu.VMEM((tb + SUBLANES, hb * A_QK_PAD), F32),
                        pltpu.VMEM((tb + SUBLANES, hb * A_QK_PAD), F32),
                        pltpu.VMEM((tb, hb * A_QK_PAD), BF16),
                        pltpu.VMEM((tb, hb * A_QK_PAD), BF16),
                        pltpu.VMEM((hb, A_QK_PAD, A_V_DIM + A_AUG), F32),
                        pltpu.VMEM((SUBLANES, hb * LANES), F32)],
        compiler_params=_params(3),
        name="mlstm",
    )(zq, zk, zv, zo, zg, cwq, cwk, cbq, cbk, gate_b, head_g, tril)


def _hgrn2_pair_masks():
    n = CHUNK
    masks = [np.eye(n, dtype=np.float32)]
    for j in range(1, B_LEVELS + 1):
        c = n >> j
        pm = np.zeros((n, n), np.float32)
        for r in range(n):
            if r % (2 * c) >= c:
                mid = (r // (2 * c)) * 2 * c + c
                pm[r, mid - c:mid] = 1.0
        masks.append(pm)
    return np.stack(masks, axis=0)


def _hgrn2_kernel(q_ref, f_ref, i_ref, g_ref, lbl_ref, hg_ref, tril_ref, mask_ref, out_ref, s_st, *, tb, hb, layer):
    t = pl.program_id(2)

    @pl.when(t == 0)
    def _():
        s_st[...] = jnp.zeros_like(s_st)

    lg = lbl_ref[...]
    lg = lg - jnp.max(lg, axis=0, keepdims=True)
    pe = jnp.exp(lg)
    pr = pe / jnp.sum(pe, axis=0, keepdims=True)
    c0 = pr[0:1, :]
    cl = c0
    for r in range(1, layer + 1):
        cl = cl + pr[r:r + 1, :]
    lb_all = cl - c0

    om_all = 1.0 - lb_all
    tril = tril_ref[...]
    wide = hb * B_K_DIM
    heads = range(hb)
    rowi = lax.broadcasted_iota(jnp.int32, (CHUNK, wide), 0)
    sub = lax.broadcasted_iota(jnp.int32, (CHUNK // SUBLANES, SUBLANES, wide), 1)

    def level_exponent(gc, lf, half):
        if half >= SUBLANES:
            pieces = []
            for b0 in range(0, CHUNK, 2 * half):
                mid = b0 + half
                ref = gc[mid - 1:mid, :]
                pieces.append(ref - gc[b0:mid, :])
                pieces.append(gc[mid:mid + half, :] - ref)
            return jnp.concatenate(pieces, axis=0)
        if half == 1:
            return jnp.where((rowi & 1) != 0, lf, 0.0)
        g3 = gc.reshape(CHUNK // SUBLANES, SUBLANES, wide)
        ref = g3[:, half - 1:half, :]
        for b0 in range(2 * half, SUBLANES, 2 * half):
            ref = jnp.where(sub >= b0, g3[:, b0 + half - 1:b0 + half, :], ref)
        d = g3 - ref
        return jnp.where((sub & half) != 0, d, -d).reshape(CHUNK, wide)

    def chunk(c, carry):
        rows = pl.ds(pl.multiple_of(c * CHUNK, CHUNK), CHUNK)
        fz = f_ref[rows, :]
        e = jnp.exp(-jnp.abs(fz))
        r = 1.0 / (1.0 + e)
        sig_pos = jnp.where(fz >= 0, r, e * r)
        sig_neg = jnp.where(fz >= 0, e * r, r)
        lf = jnp.log(lb_all + om_all * sig_pos)
        kk = om_all * sig_neg
        q = _silu(q_ref[rows, :])
        v = i_ref[rows, :].astype(BF16)
        gc = _dot_exact_lhs(tril, lf)
        g_end = gc[CHUNK - 1:CHUNK, :]

        xs = []
        for j in range(1, B_LEVELS + 1):
            half = CHUNK >> j
            upper = (rowi & half) != 0
            xs.append((jnp.where(upper, q, kk) * jnp.exp(level_exponent(gc, lf, half))).astype(BF16))
        q_bf = q.astype(BF16)
        k_bf = kk.astype(BF16)
        qg = (q * jnp.exp(gc)).astype(BF16)
        kd = (kk * jnp.exp(g_end - gc)).astype(BF16)
        dec = jnp.exp(g_end)

        def hl(x, hh):
            return x[:, hh * B_K_DIM:(hh + 1) * B_K_DIM]

        prod = [[_dot(hl(q_bf, hh), hl(k_bf, hh), _NT)] + [_dot(hl(x, hh), hl(x, hh), _NT) for x in xs]
                for hh in heads]
        s_old = [s_st[hh] for hh in heads]
        o_inter = [_dot(hl(qg, hh), s_old[hh].astype(BF16), _NT) for hh in heads]
        for hh in heads:
            s_st[hh] = s_old[hh] * hl(dec, hh) + _dot(hl(v, hh), hl(kd, hh), _TN)
        att = []
        for hh in heads:
            acc = mask_ref[0] * prod[hh][0]
            for j in range(1, B_LEVELS + 1):
                acc = acc + mask_ref[j] * prod[hh][j]
            att.append(acc.astype(BF16))
        outs = []
        for hh in heads:
            o = _dot(att[hh], hl(v, hh)) + o_inter[hh]
            ms = jnp.mean(o * o, axis=-1, keepdims=True)
            outs.append(o * lax.rsqrt(ms + EPS))
        on = jnp.concatenate(outs, axis=1) * hg_ref[...]
        out_ref[rows, :] = (on * _silu(g_ref[rows, :])).astype(out_ref.dtype)
        return carry

    lax.fori_loop(0, tb // CHUNK, chunk, 0)


def _hgrn2(zq, zf, zi, zg, lb_logits, head_g, batch, seq, tb, hb, layer):
    t_total = batch * seq
    nt = seq // tb
    depth = lb_logits.shape[0]
    mask_np = _hgrn2_pair_masks()
    tril = jnp.asarray(np.tril(np.ones((CHUNK, CHUNK), np.float32)), BF16)
    row_map = lambda b, h, t: (b * nt + t, h)
    head_map = lambda b, h, t: (0, h)
    return pl.pallas_call(
        functools.partial(_hgrn2_kernel, tb=tb, hb=hb, layer=layer),
        grid=(batch, B_HEADS // hb, nt),
        in_specs=[pl.BlockSpec((tb, hb * B_K_DIM), row_map),
                  pl.BlockSpec((tb, hb * B_K_DIM), row_map),
                  pl.BlockSpec((tb, hb * B_V_DIM), row_map),
                  pl.BlockSpec((tb, hb * B_V_DIM), row_map),
                  pl.BlockSpec((depth, hb * B_K_DIM), head_map),
                  pl.BlockSpec((1, hb * B_V_DIM), head_map),
                  pl.BlockSpec((CHUNK, CHUNK), lambda b, h, t: (0, 0)),
                  pl.BlockSpec(mask_np.shape, lambda b, h, t: (0, 0, 0))],
        out_specs=pl.BlockSpec((tb, hb * B_V_DIM), row_map),
        out_shape=jax.ShapeDtypeStruct((t_total, MIX_WIDTH), BF16),
        scratch_shapes=[pltpu.VMEM((hb, B_V_DIM, B_K_DIM), F32)],
        compiler_params=_params(3),
        name="hgrn2",
    )(zq, zf, zi, zg, lb_logits, head_g, tril, jnp.asarray(mask_np, F32))


def _outproj_kernel(y_ref, xq_ref, kv_ref, wo_ref, h_ref, out_ref):
    kv = kv_ref[...]
    parts = []
    for hh in range(XA_HEADS):
        lo = hh * XA_HEAD_DIM
        qh = xq_ref[:, lo:lo + XA_HEAD_DIM].astype(BF16)
        kh = kv[:, lo:lo + XA_HEAD_DIM].astype(BF16)
        vh = kv[:, XA_WIDTH + lo:XA_WIDTH + lo + XA_HEAD_DIM].astype(BF16)
        s = _dot(qh, kh, _NT) * (XA_HEAD_DIM ** -0.5)
        e = jnp.exp(s - jnp.max(s, axis=-1, keepdims=True))
        den = jnp.sum(e, axis=-1, keepdims=True)
        parts.append((_dot(e.astype(BF16), vh) / den).astype(BF16))
    y_mem = jnp.concatenate(parts, axis=1)
    out_ref[...] = (h_ref[...] + _dot(y_ref[...], wo_ref[0:MIX_WIDTH, :])
                    + _dot(y_mem, wo_ref[MIX_WIDTH:D_MODEL, :]))


def _outproj(y_mix, zxq, kv, w_out_bf, h, seq, mem_len, tm):
    t = h.shape[0]
    per_seq = seq // tm
    return pl.pallas_call(
        _outproj_kernel,
        grid=(t // tm,),
        in_specs=[pl.BlockSpec((tm, MIX_WIDTH), lambda i: (i, 0)),
                  pl.BlockSpec((tm, XA_WIDTH), lambda i: (i, 0)),
                  pl.BlockSpec((mem_len, 2 * XA_WIDTH), lambda i: (i // per_seq, 0)),
                  _resident((D_MODEL, D_MODEL)),
                  pl.BlockSpec((tm, D_MODEL), lambda i: (i, 0))],
        out_specs=pl.BlockSpec((tm, D_MODEL), lambda i: (i, 0)),
        out_shape=jax.ShapeDtypeStruct((t, D_MODEL), F32),
        compiler_params=_params(1),
        name="outproj",
    )(y_mix, zxq, kv, w_out_bf, h)


def _ffn_kernel(h_ref, g_ref, wu_ref, wg_ref, cw_ref, cb_ref, wd_ref, og_ref, out_ref,
                hn_s, gbuf, gcarry, acc, *, tm, nf, per_seq, final_norm):
    i = pl.program_id(0)
    j = pl.program_id(1)

    @pl.when(j == 0)
    def _():
        x = h_ref[...]
        ms = jnp.mean(x * x, axis=-1, keepdims=True)
        hn_s[...] = (x * lax.rsqrt(ms + EPS) * g_ref[...]).astype(BF16)

    hn = hn_s[...]
    u = _dot(hn, wu_ref[...])
    g = _dot(hn, wg_ref[...])
    prev = jnp.where(i % per_seq == 0, 0.0, gcarry[j])
    gbuf[0:SUBLANES] = prev
    gbuf[SUBLANES:SUBLANES + tm] = g
    gcarry[j] = g[tm - SUBLANES:tm, :]
    gc = cb_ref[...]
    for k in range(FFN_CONV):
        lo = SUBLANES - (FFN_CONV - 1) + k
        gc = gc + cw_ref[k:k + 1, :] * gbuf[lo:lo + tm, :]
    act = (_silu(gc) * u).astype(BF16)
    d = _dot(act, wd_ref[...])

    @pl.when(j == 0)
    def _():
        acc[...] = d

    @pl.when(j > 0)
    def _():
        acc[...] = acc[...] + d

    @pl.when(j == nf - 1)
    def _():
        y = h_ref[...] + acc[...]
        if final_norm:
            ms = jnp.mean(y * y, axis=-1, keepdims=True)
            y = y * lax.rsqrt(ms + EPS) * og_ref[...]
        out_ref[...] = y


def _ffn(h, g, w_up_bf, conv_w, conv_b, w_down_bf, out_g, seq, tm, tf, final_norm):
    t, d = h.shape
    nf = D_FF // tf
    per_seq = seq // tm
    return pl.pallas_call(
        functools.partial(_ffn_kernel, tm=tm, nf=nf, per_seq=per_seq, final_norm=final_norm),
        grid=(t // tm, nf),
        in_specs=[pl.BlockSpec((tm, d), lambda i, j: (i, 0)),
                  pl.BlockSpec((1, d), lambda i, j: (0, 0)),
                  pl.BlockSpec((d, tf), lambda i, j: (0, j)),
                  pl.BlockSpec((d, tf), lambda i, j: (0, nf + j)),
                  pl.BlockSpec((FFN_CONV, tf), lambda i, j: (0, j)),
                  pl.BlockSpec((1, tf), lambda i, j: (0, j)),
                  pl.BlockSpec((tf, d), lambda i, j: (j, 0)),
                  pl.BlockSpec((1, d), lambda i, j: (0, 0))],
        out_specs=pl.BlockSpec((tm, d), lambda i, j: (i, 0)),
        out_shape=jax.ShapeDtypeStruct((t, d), F32),
        scratch_shapes=[pltpu.VMEM((tm, d), BF16),
                        pltpu.VMEM((tm + SUBLANES, tf), F32),
                        pltpu.VMEM((nf, SUBLANES, tf), F32),
                        pltpu.VMEM((tm, d), F32)],
        compiler_params=_params(2),
        name="ffn",
    )(h, g.reshape(1, d), w_up_bf, w_up_bf, conv_w, conv_b.reshape(1, D_FF), w_down_bf, out_g.reshape(1, d))


def _pad_heads(w, heads, dim, pad):
    lead = w.shape[:-1]
    w = w.reshape(lead + (heads, dim))
    w = jnp.pad(w, [(0, 0)] * len(lead) + [(0, 0), (0, pad - dim)])
    return w.reshape(lead + (heads * pad,))


def _mlstm_weights(w_in, gate_b, conv_w, conv_b):
    qk_w = A_HEADS * A_QK_DIM
    o0 = 2 * qk_w
    wq = _pad_heads(w_in[:, :qk_w], A_HEADS, A_QK_DIM, A_QK_PAD)
    wk = _pad_heads(w_in[:, qk_w:o0], A_HEADS, A_QK_DIM, A_QK_PAD)
    wv = w_in[:, o0:o0 + MIX_WIDTH]
    wo = w_in[:, o0 + MIX_WIDTH:o0 + 2 * MIX_WIDTH]
    wg = jnp.pad(w_in[:, o0 + 2 * MIX_WIDTH:o0 + 2 * MIX_WIDTH + 2 * A_HEADS], ((0, 0), (0, LANES - 2 * A_HEADS)))
    wx = w_in[:, o0 + 2 * MIX_WIDTH + 2 * A_HEADS:]
    w_all = jnp.concatenate([wq, wk, wv, wo, wx, wg], axis=1).astype(BF16)
    splits = (A_HEADS * A_QK_PAD, A_HEADS * A_QK_PAD, MIX_WIDTH, MIX_WIDTH, XA_WIDTH, LANES)
    cwq = _pad_heads(conv_w[:, :qk_w], A_HEADS, A_QK_DIM, A_QK_PAD)
    cwk = _pad_heads(conv_w[:, qk_w:], A_HEADS, A_QK_DIM, A_QK_PAD)
    cbq = _pad_heads(conv_b[None, :qk_w], A_HEADS, A_QK_DIM, A_QK_PAD)
    cbk = _pad_heads(conv_b[None, qk_w:], A_HEADS, A_QK_DIM, A_QK_PAD)
    gb = jnp.pad(gate_b[None, :], ((0, 0), (0, LANES - 2 * A_HEADS)))
    return w_all, splits, cwq, cwk, cbq, cbk, gb


def _pick(n, cap):
    b = min(n, cap)
    while n % b:
        b //= 2
    return b


def kernel(x, mem, norm_mix_g, norm_mem_g, norm_ffn_g, norm_out_g, w_mem_kv, a_w_in, a_gate_b, a_conv_w, a_conv_b, a_head_g, a_w_out, b_w_in, b_lb_logits, b_head_g, b_w_out, ffn_w_up, ffn_conv_w, ffn_conv_b, ffn_w_down):
    batch, seq, d = x.shape
    mem_len = mem.shape[1]
    depth = norm_mix_g.shape[0]
    t = batch * seq
    tm_proj = _pick(seq, 256)
    tm_out = _pick(seq, 512)
    tm_ffn = _pick(seq, 512)
    tb = _pick(seq, 256)

    h = x.reshape(t, d)
    mem2 = mem.reshape(batch * mem_len, d)
    for layer in range(depth):
        j = layer // N_MIXERS
        (kv,) = _norm_matmul(mem2, norm_mem_g[layer], w_mem_kv[layer].astype(BF16), (2 * XA_WIDTH,),
                             _pick(mem_len, 256))
        if layer % N_MIXERS == 0:
            w_all, splits, cwq, cwk, cbq, cbk, gb = _mlstm_weights(a_w_in[j], a_gate_b[j], a_conv_w[j], a_conv_b[j])
            zq, zk, zv, zo, zxq, zg = _norm_matmul(h, norm_mix_g[layer], w_all, splits, tm_proj)
            y_mix = _mlstm(zq, zk, zv, zo, zg, cwq, cwk, cbq, cbk, gb, a_head_g[j].reshape(1, MIX_WIDTH),
                           batch, seq, tb, A_HEADS_PER_STEP)
            w_out = a_w_out[j]
        else:
            kw = B_HEADS * B_K_DIM
            splits = (kw, kw, MIX_WIDTH, MIX_WIDTH, XA_WIDTH)
            zq, zf, zi, zg, zxq = _norm_matmul(h, norm_mix_g[layer], b_w_in[j].astype(BF16), splits, tm_proj)
            y_mix = _hgrn2(zq, zf, zi, zg, b_lb_logits, b_head_g[j].reshape(1, MIX_WIDTH), batch, seq, tb,
                           B_HEADS_PER_STEP, layer)
            w_out = b_w_out[j]
        h = _outproj(y_mix, zxq, kv, w_out.astype(BF16), h, seq, mem_len, tm_out)
        h = _ffn(h, norm_ffn_g[layer], ffn_w_up[layer].astype(BF16), ffn_conv_w[layer], ffn_conv_b[layer],
                 ffn_w_down[layer].astype(BF16), norm_out_g, seq, tm_ffn, 512, layer == depth - 1)
    return h.reshape(batch, seq, d)
```

```python
import functools

import numpy as np
import jax
import jax.numpy as jnp
from jax import lax
from jax.experimental import pallas as pl
from jax.experimental.pallas import tpu as pltpu

F32 = jnp.float32
BF16 = jnp.bfloat16

D_MODEL = 2048
N_MIXERS = 2
CHUNK = 64
EPS = 1e-6

XA_HEADS = 4
XA_WIDTH = D_MODEL // 4
XA_HEAD_DIM = XA_WIDTH // XA_HEADS
MIX_WIDTH = D_MODEL - XA_WIDTH

A_HEADS = 4
A_V_DIM = MIX_WIDTH // A_HEADS
A_QK_DIM = A_V_DIM // 2
A_QK_PAD = 256
A_CONV = 4
A_AUG = 128
A_HEADS_PER_STEP = 4

B_HEADS = 12
B_K_DIM = 128
B_V_DIM = 128
B_LEVELS = 6
B_HEADS_PER_STEP = 4

D_FF = 5632
FFN_CONV = 3

LANES = 128
SUBLANES = 8
VMEM_LIMIT_BYTES = 56 * 1024 * 1024

_NN = (((1,), (0,)), ((), ()))
_NT = (((1,), (1,)), ((), ()))
_TN = (((0,), (0,)), ((), ()))


def _dot(a, b, dims=_NN):
    return lax.dot_general(a, b, dims, preferred_element_type=F32)


def _split2(x):
    hi = x.astype(BF16)
    return hi, (x - hi.astype(F32)).astype(BF16)


def _dot_exact_lhs(a_bf, x, dims=_NN):
    return sum(_dot(a_bf, p, dims) for p in _split2(x))


def _dot_exact_rhs(x, b_bf, dims=_NN):
    return sum(_dot(p, b_bf, dims) for p in _split2(x))


def _sigmoid(x):
    e = jnp.exp(-jnp.abs(x))
    r = 1.0 / (1.0 + e)
    return jnp.where(x >= 0, r, e * r)


def _silu(x):
    return x * _sigmoid(x)


def _log_sigmoid(x):
    return jnp.minimum(x, 0.0) - jnp.log(1.0 + jnp.exp(-jnp.abs(x)))


def _params(n_grid):
    return pltpu.CompilerParams(dimension_semantics=("arbitrary",) * n_grid,
                                vmem_limit_bytes=VMEM_LIMIT_BYTES)


def _resident(shape):
    nd = len(shape)
    return pl.BlockSpec(shape, lambda *_: (0,) * nd, pipeline_mode=pl.Buffered(1))


def _norm_matmul_kernel(x_ref, g_ref, w_ref, *out_refs, splits):
    x = x_ref[...]
    ms = jnp.mean(x * x, axis=-1, keepdims=True)
    hn = (x * lax.rsqrt(ms + EPS) * g_ref[...]).astype(BF16)
    off = 0
    for o_ref, n in zip(out_refs, splits):
        o_ref[...] = _dot(hn, w_ref[:, off:off + n]).astype(o_ref.dtype)
        off += n


def _norm_matmul(x, g, w_bf, splits, tm):
    t, d = x.shape
    n = w_bf.shape[1]
    assert sum(splits) == n and t % tm == 0
    return pl.pallas_call(
        functools.partial(_norm_matmul_kernel, splits=tuple(splits)),
        grid=(t // tm,),
        in_specs=[pl.BlockSpec((tm, d), lambda i: (i, 0)),
                  _resident((1, d)),
                  _resident((d, n))],
        out_specs=[pl.BlockSpec((tm, s), lambda i: (i, 0)) for s in splits],
        out_shape=[jax.ShapeDtypeStruct((t, s), F32) for s in splits],
        compiler_params=_params(1),
        name="norm_matmul",
    )(x, g.reshape(1, d), w_bf)


def _mlstm_kernel(q_ref, k_ref, v_ref, o_ref, gt_ref, cwq_ref, cwk_ref, cbq_ref, cbk_ref, gb_ref, hg_ref,
                  tril_ref, out_ref, qbuf, kbuf, qs, ks, c_st, m_st, *, tb, hb):
    hgrp = pl.program_id(1)
    t = pl.program_id(2)
    aug_w = A_V_DIM + A_AUG
    rep = aug_w // LANES

    @pl.when(t == 0)
    def _():
        qbuf[0:SUBLANES] = jnp.zeros((SUBLANES, hb * A_QK_PAD), F32)
        kbuf[0:SUBLANES] = jnp.zeros((SUBLANES, hb * A_QK_PAD), F32)
        c_st[...] = jnp.zeros_like(c_st)
        m_st[...] = jnp.zeros_like(m_st)

    qbuf[SUBLANES:SUBLANES + tb] = q_ref[...]
    kbuf[SUBLANES:SUBLANES + tb] = k_ref[...]

    def conv_silu(buf, w_ref, b_ref):
        acc = b_ref[...]
        for j in range(A_CONV):
            lo = SUBLANES - (A_CONV - 1) + j
            acc = acc + w_ref[j:j + 1, :] * buf[lo:lo + tb, :]
        return _silu(acc)

    qs[...] = conv_silu(qbuf, cwq_ref, cbq_ref).astype(BF16)
    ks[...] = (conv_silu(kbuf, cwk_ref, cbk_ref) * (A_QK_DIM ** -0.5)).astype(BF16)
    qbuf[0:SUBLANES] = qbuf[tb:tb + SUBLANES]
    kbuf[0:SUBLANES] = kbuf[tb:tb + SUBLANES]

    tril = tril_ref[...]
    row = lax.broadcasted_iota(jnp.int32, (CHUNK, CHUNK), 0)
    col = lax.broadcasted_iota(jnp.int32, (CHUNK, CHUNK), 1)
    causal = row >= col
    wide = hb * LANES
    srow = lax.broadcasted_iota(jnp.int32, (LANES, 2 * wide), 0)
    scol = lax.broadcasted_iota(jnp.int32, (LANES, 2 * wide), 1)
    blk = scol // LANES
    want = jnp.where(blk < hb, hgrp * hb + blk, A_HEADS + hgrp * hb + (blk - hb))
    sel = jnp.where(srow == want, 1.0, 0.0).astype(BF16)
    frow = lax.broadcasted_iota(jnp.int32, (CHUNK, LANES), 1)
    first = jnp.where(frow == 0, 1.0, 0.0).astype(BF16)
    ones_aug = jnp.ones((CHUNK, A_AUG), BF16)
    heads = range(hb)

    def lanes_of(hh):
        return slice(hh * LANES, (hh + 1) * LANES)

    def chunk(c, carry):
        rows = pl.ds(pl.multiple_of(c * CHUNK, CHUNK), CHUNK)
        gates = gt_ref[rows, :] + gb_ref[...]
        g2 = _dot_exact_rhs(gates, sel)
        ic = g2[:, :wide]
        lf = _log_sigmoid(g2[:, wide:])
        gc = _dot_exact_lhs(tril, lf)
        m_prev = m_st[0:1, :]
        a = gc + m_prev
        icg = ic - gc
        r_mat = [_dot_exact_lhs(first, icg[:, lanes_of(hh)], _NT) for hh in heads]
        dmat = [jnp.where(causal, gc[:, hh * LANES:hh * LANES + CHUNK] + r_mat[hh], -jnp.inf) for hh in heads]
        mx = jnp.concatenate([jnp.broadcast_to(jnp.max(dmat[hh], axis=1, keepdims=True), (CHUNK, LANES))
                              for hh in heads], axis=1)
        m_row = jnp.maximum(a, mx)
        w_inter = jnp.exp(a - m_row)
        e_neg = jnp.exp(-m_row)
        p = [jnp.exp(dmat[hh] - m_row[:, hh * LANES:hh * LANES + CHUNK]) for hh in heads]

        g_end = gc[CHUNK - 1:CHUNK, :]
        a_end = g_end + m_prev
        w_end = g_end - gc + ic
        m_new = jnp.maximum(a_end, jnp.max(w_end, axis=0, keepdims=True))
        decay = jnp.exp(a_end - m_new)
        ws = jnp.exp(w_end - m_new)
        m_st[...] = jnp.broadcast_to(m_new, (SUBLANES, wide))

        qk_lanes = [slice(hh * A_QK_PAD, (hh + 1) * A_QK_PAD) for hh in heads]
        v_lanes = [slice(hh * A_V_DIM, (hh + 1) * A_V_DIM) for hh in heads]
        q = [qs[rows, qk_lanes[hh]] for hh in heads]
        k = [ks[rows, qk_lanes[hh]] for hh in heads]
        v_aug = [jnp.concatenate([v_ref[rows, v_lanes[hh]].astype(BF16), ones_aug], axis=1) for hh in heads]
        qk = [_dot(q[hh], k[hh], _NT) for hh in heads]
        c_old = [c_st[hh] for hh in heads]
        inter = [_dot(q[hh], c_old[hh].astype(BF16)) for hh in heads]
        intra = [_dot((qk[hh] * p[hh]).astype(BF16), v_aug[hh]) for hh in heads]
        for hh in heads:
            vw = (v_aug[hh].astype(F32) * jnp.concatenate([ws[:, lanes_of(hh)]] * rep, axis=1)).astype(BF16)
            c_st[hh] = (jnp.concatenate([decay[:, lanes_of(hh)]] * rep, axis=1) * c_old[hh]
                        + _dot(k[hh], vw, _TN))
        for hh in heads:
            tot = jnp.concatenate([w_inter[:, lanes_of(hh)]] * rep, axis=1) * inter[hh] + intra[hh]
            den = tot[:, A_V_DIM:]
            inv = 1.0 / jnp.maximum(jnp.abs(den), e_neg[:, lanes_of(hh)])
            h_out = tot[:, :A_V_DIM] * jnp.concatenate([inv] * (A_V_DIM // LANES), axis=1)
            ms = jnp.mean(h_out * h_out, axis=-1, keepdims=True)
            hn = h_out * lax.rsqrt(ms + EPS) * hg_ref[:, v_lanes[hh]]
            out_ref[rows, v_lanes[hh]] = (_sigmoid(o_ref[rows, v_lanes[hh]]) * hn).astype(out_ref.dtype)
        return carry

    lax.fori_loop(0, tb // CHUNK, chunk, 0)


def _mlstm(zq, zk, zv, zo, zg, cwq, cwk, cbq, cbk, gate_b, head_g, batch, seq, tb, hb):
    t_total = batch * seq
    nt = seq // tb
    tril = jnp.asarray(np.tril(np.ones((CHUNK, CHUNK), np.float32)), BF16)
    row_map = lambda b, h, t: (b * nt + t, h)
    head_map = lambda b, h, t: (0, h)
    fixed = lambda b, h, t: (0, 0)
    return pl.pallas_call(
        functools.partial(_mlstm_kernel, tb=tb, hb=hb),
        grid=(batch, A_HEADS // hb, nt),
        in_specs=[pl.BlockSpec((tb, hb * A_QK_PAD), row_map),
                  pl.BlockSpec((tb, hb * A_QK_PAD), row_map),
                  pl.BlockSpec((tb, hb * A_V_DIM), row_map),
                  pl.BlockSpec((tb, hb * A_V_DIM), row_map),
                  pl.BlockSpec((tb, LANES), lambda b, h, t: (b * nt + t, 0)),
                  pl.BlockSpec((A_CONV, hb * A_QK_PAD), head_map),
                  pl.BlockSpec((A_CONV, hb * A_QK_PAD), head_map),
                  pl.BlockSpec((1, hb * A_QK_PAD), head_map),
                  pl.BlockSpec((1, hb * A_QK_PAD), head_map),
                  pl.BlockSpec((1, LANES), fixed),
                  pl.BlockSpec((1, hb * A_V_DIM), head_map),
                  pl.BlockSpec((CHUNK, CHUNK), fixed)],
        out_specs=pl.BlockSpec((tb, hb * A_V_DIM), row_map),
        out_shape=jax.ShapeDtypeStruct((t_total, MIX_WIDTH), BF16),
        scratch_shapes=[pltpu.VMEM((tb + SUBLANES, hb * A_QK_PAD), F32),
                        pltpu.VMEM((tb + SUBLANES, hb * A_QK_PAD), F32),
                        pltpu.VMEM((tb, hb * A_QK_PAD), BF16),
                        pltpu.VMEM((tb, hb * A_QK_PAD), BF16),
                        pltpu.VMEM((hb, A_QK_PAD, A_V_DIM + A_AUG), F32),
                        pltpu.VMEM((SUBLANES, hb * LANES), F32)],
        compiler_params=_params(3),
        name="mlstm",
    )(zq, zk, zv, zo, zg, cwq, cwk, cbq, cbk, gate_b, head_g, tril)


def _hgrn2_pair_masks():
    n = CHUNK
    masks = [np.eye(n, dtype=np.float32)]
    for j in range(1, B_LEVELS + 1):
        c = n >> j
        pm = np.zeros((n, n), np.float32)
        for r in range(n):
            if r % (2 * c) >= c:
                mid = (r // (2 * c)) * 2 * c + c
                pm[r, mid - c:mid] = 1.0
        masks.append(pm)
    return np.stack(masks, axis=0)


def _hgrn2_kernel(q_ref, f_ref, i_ref, g_ref, lbl_ref, hg_ref, tril_ref, mask_ref, out_ref, s_st, *, tb, hb, layer):
    t = pl.program_id(2)

    @pl.when(t == 0)
    def _():
        s_st[...] = jnp.zeros_like(s_st)

    lg = lbl_ref[...]
    lg = lg - jnp.max(lg, axis=0, keepdims=True)
    pe = jnp.exp(lg)
    pr = pe / jnp.sum(pe, axis=0, keepdims=True)
    c0 = pr[0:1, :]
    cl = c0
    for r in range(1, layer + 1):
        cl = cl + pr[r:r + 1, :]
    lb_all = cl - c0

    om_all = 1.0 - lb_all
    tril = tril_ref[...]
    wide = hb * B_K_DIM
    heads = range(hb)
    rowi = lax.broadcasted_iota(jnp.int32, (CHUNK, wide), 0)
    sub = lax.broadcasted_iota(jnp.int32, (CHUNK // SUBLANES, SUBLANES, wide), 1)

    def level_exponent(gc, lf, half):
        if half >= SUBLANES:
            pieces = []
            for b0 in range(0, CHUNK, 2 * half):
                mid = b0 + half
                ref = gc[mid - 1:mid, :]
                pieces.append(ref - gc[b0:mid, :])
                pieces.append(gc[mid:mid + half, :] - ref)
            return jnp.concatenate(pieces, axis=0)
        if half == 1:
            return jnp.where((rowi & 1) != 0, lf, 0.0)
        g3 = gc.reshape(CHUNK // SUBLANES, SUBLANES, wide)
        ref = g3[:, half - 1:half, :]
        for b0 in range(2 * half, SUBLANES, 2 * half):
            ref = jnp.where(sub >= b0, g3[:, b0 + half - 1:b0 + half, :], ref)
        d = g3 - ref
        return jnp.where((sub & half) != 0, d, -d).reshape(CHUNK, wide)

    def chunk(c, carry):
        rows = pl.ds(pl.multiple_of(c * CHUNK, CHUNK), CHUNK)
        fz = f_ref[rows, :]
        e = jnp.exp(-jnp.abs(fz))
        r = 1.0 / (1.0 + e)
        sig_pos = jnp.where(fz >= 0, r, e * r)
        sig_neg = jnp.where(fz >= 0, e * r, r)
        lf = jnp.log(lb_all + om_all * sig_pos)
        kk = om_all * sig_neg
        q = _silu(q_ref[rows, :])
        v = i_ref[rows, :].astype(BF16)
        gc = _dot_exact_lhs(tril, lf)
        g_end = gc[CHUNK - 1:CHUNK, :]

        xs = []
        for j in range(1, B_LEVELS + 1):
            half = CHUNK >> j
            upper = (rowi & half) != 0
            xs.append((jnp.where(upper, q, kk) * jnp.exp(level_exponent(gc, lf, half))).astype(BF16))
        q_bf = q.astype(BF16)
        k_bf = kk.astype(BF16)
        qg = (q * jnp.exp(gc)).astype(BF16)
        kd = (kk * jnp.exp(g_end - gc)).astype(BF16)
        dec = jnp.exp(g_end)

        def hl(x, hh):
            return x[:, hh * B_K_DIM:(hh + 1) * B_K_DIM]

        prod = [[_dot(hl(q_bf, hh), hl(k_bf, hh), _NT)] + [_dot(hl(x, hh), hl(x, hh), _NT) for x in xs]
                for hh in heads]
        s_old = [s_st[hh] for hh in heads]
        o_inter = [_dot(hl(qg, hh), s_old[hh].astype(BF16), _NT) for hh in heads]
        for hh in heads:
            s_st[hh] = s_old[hh] * hl(dec, hh) + _dot(hl(v, hh), hl(kd, hh), _TN)
        att = []
        for hh in heads:
            acc = mask_ref[0] * prod[hh][0]
            for j in range(1, B_LEVELS + 1):
                acc = acc + mask_ref[j] * prod[hh][j]
            att.append(acc.astype(BF16))
        outs = []
        for hh in heads:
            o = _dot(att[hh], hl(v, hh)) + o_inter[hh]
            ms = jnp.mean(o * o, axis=-1, keepdims=True)
            outs.append(o * lax.rsqrt(ms + EPS))
        on = jnp.concatenate(outs, axis=1) * hg_ref[...]
        out_ref[rows, :] = (on * _silu(g_ref[rows, :])).astype(out_ref.dtype)
        return carry

    lax.fori_loop(0, tb // CHUNK, chunk, 0)


def _hgrn2(zq, zf, zi, zg, lb_logits, head_g, batch, seq, tb, hb, layer):
    t_total = batch * seq
    nt = seq // tb
    depth = lb_logits.shape[0]
    mask_np = _hgrn2_pair_masks()
    tril = jnp.asarray(np.tril(np.ones((CHUNK, CHUNK), np.float32)), BF16)
    row_map = lambda b, h, t: (b * nt + t, h)
    head_map = lambda b, h, t: (0, h)
    return pl.pallas_call(
        functools.partial(_hgrn2_kernel, tb=tb, hb=hb, layer=layer),
        grid=(batch, B_HEADS // hb, nt),
        in_specs=[pl.BlockSpec((tb, hb * B_K_DIM), row_map),
                  pl.BlockSpec((tb, hb * B_K_DIM), row_map),
                  pl.BlockSpec((tb, hb * B_V_DIM), row_map),
                  pl.BlockSpec((tb, hb * B_V_DIM), row_map),
                  pl.BlockSpec((depth, hb * B_K_DIM), head_map),
                  pl.BlockSpec((1, hb * B_V_DIM), head_map),
                  pl.BlockSpec((CHUNK, CHUNK), lambda b, h, t: (0, 0)),
                  pl.BlockSpec(mask_np.shape, lambda b, h, t: (0, 0, 0))],
        out_specs=pl.BlockSpec((tb, hb * B_V_DIM), row_map),
        out_shape=jax.ShapeDtypeStruct((t_total, MIX_WIDTH), BF16),
        scratch_shapes=[pltpu.VMEM((hb, B_V_DIM, B_K_DIM), F32)],
        compiler_params=_params(3),
        name="hgrn2",
    )(zq, zf, zi, zg, lb_logits, head_g, tril, jnp.asarray(mask_np, F32))


def _outproj_kernel(y_ref, xq_ref, kv_ref, wo_ref, h_ref, out_ref):
    kv = kv_ref[...]
    parts = []
    for hh in range(XA_HEADS):
        lo = hh * XA_HEAD_DIM
        qh = xq_ref[:, lo:lo + XA_HEAD_DIM].astype(BF16)
        kh = kv[:, lo:lo + XA_HEAD_DIM].astype(BF16)
        vh = kv[:, XA_WIDTH + lo:XA_WIDTH + lo + XA_HEAD_DIM].astype(BF16)
        s = _dot(qh, kh, _NT) * (XA_HEAD_DIM ** -0.5)
        e = jnp.exp(s - jnp.max(s, axis=-1, keepdims=True))
        den = jnp.sum(e, axis=-1, keepdims=True)
        parts.append((_dot(e.astype(BF16), vh) / den).astype(BF16))
    y_mem = jnp.concatenate(parts, axis=1)
    out_ref[...] = (h_ref[...] + _dot(y_ref[...], wo_ref[0:MIX_WIDTH, :])
                    + _dot(y_mem, wo_ref[MIX_WIDTH:D_MODEL, :]))


def _outproj(y_mix, zxq, kv, w_out_bf, h, seq, mem_len, tm):
    t = h.shape[0]
    per_seq = seq // tm
    return pl.pallas_call(
        _outproj_kernel,
        grid=(t // tm,),
        in_specs=[pl.BlockSpec((tm, MIX_WIDTH), lambda i: (i, 0)),
                  pl.BlockSpec((tm, XA_WIDTH), lambda i: (i, 0)),
                  pl.BlockSpec((mem_len, 2 * XA_WIDTH), lambda i: (i // per_seq, 0)),
                  _resident((D_MODEL, D_MODEL)),
                  pl.BlockSpec((tm, D_MODEL), lambda i: (i, 0))],
        out_specs=pl.BlockSpec((tm, D_MODEL), lambda i: (i, 0)),
        out_shape=jax.ShapeDtypeStruct((t, D_MODEL), F32),
        compiler_params=_params(1),
        name="outproj",
    )(y_mix, zxq, kv, w_out_bf, h)


def _ffn_kernel(h_ref, g_ref, wu_ref, wg_ref, cw_ref, cb_ref, wd_ref, og_ref, out_ref,
                hn_s, gbuf, gcarry, act_s, y_s, *, tm, nf, nd, per_seq, final_norm):
    i = pl.program_id(0)
    j = pl.program_id(1)
    tn = y_s.shape[2]

    @pl.when(j == 0)
    def _():
        x = h_ref[...]
        ms = jnp.mean(x * x, axis=-1, keepdims=True)
        hn_s[...] = (x * lax.rsqrt(ms + EPS) * g_ref[...]).astype(BF16)

    @pl.when(j < nf)
    def _():
        hn = hn_s[...]
        g = _dot(hn, wg_ref[...])
        prev = jnp.where(i % per_seq == 0, 0.0, gcarry[j])
        gbuf[0:SUBLANES] = prev
        gbuf[SUBLANES:SUBLANES + tm] = g
        gcarry[j] = g[tm - SUBLANES:tm, :]
        gc = cb_ref[...]
        for k in range(FFN_CONV):
            lo = SUBLANES - (FFN_CONV - 1) + k
            gc = gc + cw_ref[k:k + 1, :] * gbuf[lo:lo + tm, :]
        sg = _silu(gc)
        u = _dot(hn, wu_ref[...])
        act_s[j] = (sg * u).astype(BF16)

    @pl.when(j >= nf)
    def _():
        act = jnp.concatenate([act_s[f] for f in range(nf)], axis=1)
        y_s[j - nf] = _dot(act, wd_ref[...])

    @pl.when(j == nf + nd - 1)
    def _():
        ys = [h_ref[:, n * tn:(n + 1) * tn] + y_s[n] for n in range(nd)]
        if final_norm:
            ssq = ys[0] * ys[0]
            for n in range(1, nd):
                ssq = ssq + ys[n] * ys[n]
            scale = lax.rsqrt(jnp.sum(ssq, axis=-1, keepdims=True) * (1.0 / (nd * tn)) + EPS)
            ys = [ys[n] * scale * og_ref[:, n * tn:(n + 1) * tn] for n in range(nd)]
        for n in range(nd):
            out_ref[:, n * tn:(n + 1) * tn] = ys[n]


def _ffn(h, g, w_up_bf, conv_w, conv_b, w_down_bf, out_g, seq, tm, tf, tn, final_norm):
    t, d = h.shape
    nf = D_FF // tf
    nd = d // tn
    per_seq = seq // tm
    up = lambda j: jnp.minimum(j, nf - 1)
    return pl.pallas_call(
        functools.partial(_ffn_kernel, tm=tm, nf=nf, nd=nd, per_seq=per_seq, final_norm=final_norm),
        grid=(t // tm, nf + nd),
        in_specs=[pl.BlockSpec((tm, d), lambda i, j: (i, 0)),
                  pl.BlockSpec((1, d), lambda i, j: (0, 0)),
                  pl.BlockSpec((d, tf), lambda i, j: (0, up(j))),
                  pl.BlockSpec((d, tf), lambda i, j: (0, nf + up(j))),
                  pl.BlockSpec((FFN_CONV, tf), lambda i, j: (0, up(j))),
                  pl.BlockSpec((1, tf), lambda i, j: (0, up(j))),
                  pl.BlockSpec((D_FF, tn), lambda i, j: (0, jnp.maximum(j - nf, 0))),
                  pl.BlockSpec((1, d), lambda i, j: (0, 0))],
        out_specs=pl.BlockSpec((tm, d), lambda i, j: (i, 0)),
        out_shape=jax.ShapeDtypeStruct((t, d), F32),
        scratch_shapes=[pltpu.VMEM((tm, d), BF16),
                        pltpu.VMEM((tm + SUBLANES, tf), F32),
                        pltpu.VMEM((nf, SUBLANES, tf), F32),
                        pltpu.VMEM((nf, tm, tf), BF16),
                        pltpu.VMEM((nd, tm, tn), F32)],
        compiler_params=_params(2),
        name="ffn",
    )(h, g.reshape(1, d), w_up_bf, w_up_bf, conv_w, conv_b.reshape(1, D_FF), w_down_bf, out_g.reshape(1, d))


def _pad_heads(w, heads, dim, pad):
    lead = w.shape[:-1]
    w = w.reshape(lead + (heads, dim))
    w = jnp.pad(w, [(0, 0)] * len(lead) + [(0, 0), (0, pad - dim)])
    return w.reshape(lead + (heads * pad,))


def _mlstm_weights(w_in, gate_b, conv_w, conv_b):
    qk_w = A_HEADS * A_QK_DIM
    o0 = 2 * qk_w
    wq = _pad_heads(w_in[:, :qk_w], A_HEADS, A_QK_DIM, A_QK_PAD)
    wk = _pad_heads(w_in[:, qk_w:o0], A_HEADS, A_QK_DIM, A_QK_PAD)
    wv = w_in[:, o0:o0 + MIX_WIDTH]
    wo = w_in[:, o0 + MIX_WIDTH:o0 + 2 * MIX_WIDTH]
    wg = jnp.pad(w_in[:, o0 + 2 * MIX_WIDTH:o0 + 2 * MIX_WIDTH + 2 * A_HEADS], ((0, 0), (0, LANES - 2 * A_HEADS)))
    wx = w_in[:, o0 + 2 * MIX_WIDTH + 2 * A_HEADS:]
    w_all = jnp.concatenate([wq, wk, wv, wo, wx, wg], axis=1).astype(BF16)
    splits = (A_HEADS * A_QK_PAD, A_HEADS * A_QK_PAD, MIX_WIDTH, MIX_WIDTH, XA_WIDTH, LANES)
    cwq = _pad_heads(conv_w[:, :qk_w], A_HEADS, A_QK_DIM, A_QK_PAD)
    cwk = _pad_heads(conv_w[:, qk_w:], A_HEADS, A_QK_DIM, A_QK_PAD)
    cbq = _pad_heads(conv_b[None, :qk_w], A_HEADS, A_QK_DIM, A_QK_PAD)
    cbk = _pad_heads(conv_b[None, qk_w:], A_HEADS, A_QK_DIM, A_QK_PAD)
    gb = jnp.pad(gate_b[None, :], ((0, 0), (0, LANES - 2 * A_HEADS)))
    return w_all, splits, cwq, cwk, cbq, cbk, gb


def _pick(n, cap):
    b = min(n, cap)
    while n % b:
        b //= 2
    return b


def kernel(x, mem, norm_mix_g, norm_mem_g, norm_ffn_g, norm_out_g, w_mem_kv, a_w_in, a_gate_b, a_conv_w, a_conv_b, a_head_g, a_w_out, b_w_in, b_lb_logits, b_head_g, b_w_out, ffn_w_up, ffn_conv_w, ffn_conv_b, ffn_w_down):
    batch, seq, d = x.shape
    mem_len = mem.shape[1]
    depth = norm_mix_g.shape[0]
    t = batch * seq
    tm_proj = _pick(seq, 256)
    tm_out = _pick(seq, 512)
    tm_ffn = _pick(seq, 512)
    tb = _pick(seq, 256)

    h = x.reshape(t, d)
    mem2 = mem.reshape(batch * mem_len, d)
    for layer in range(depth):
        j = layer // N_MIXERS
        (kv,) = _norm_matmul(mem2, norm_mem_g[layer], w_mem_kv[layer].astype(BF16), (2 * XA_WIDTH,),
                             _pick(mem_len, 256))
        if layer % N_MIXERS == 0:
            w_all, splits, cwq, cwk, cbq, cbk, gb = _mlstm_weights(a_w_in[j], a_gate_b[j], a_conv_w[j], a_conv_b[j])
            zq, zk, zv, zo, zxq, zg = _norm_matmul(h, norm_mix_g[layer], w_all, splits, tm_proj)
            y_mix = _mlstm(zq, zk, zv, zo, zg, cwq, cwk, cbq, cbk, gb, a_head_g[j].reshape(1, MIX_WIDTH),
                           batch, seq, tb, A_HEADS_PER_STEP)
            w_out = a_w_out[j]
        else:
            kw = B_HEADS * B_K_DIM
            splits = (kw, kw, MIX_WIDTH, MIX_WIDTH, XA_WIDTH)
            zq, zf, zi, zg, zxq = _norm_matmul(h, norm_mix_g[layer], b_w_in[j].astype(BF16), splits, tm_proj)
            y_mix = _hgrn2(zq, zf, zi, zg, b_lb_logits, b_head_g[j].reshape(1, MIX_WIDTH), batch, seq, tb,
                           B_HEADS_PER_STEP, layer)
            w_out = b_w_out[j]
        h = _outproj(y_mix, zxq, kv, w_out.astype(BF16), h, seq, mem_len, tm_out)
        h = _ffn(h, norm_ffn_g[layer], ffn_w_up[layer].astype(BF16), ffn_conv_w[layer], ffn_conv_b[layer],
                 ffn_w_down[layer].astype(BF16), norm_out_g, seq, tm_ffn, 512, 512, layer == depth - 1)
    return h.reshape(batch, seq, d)
```

```python
import functools

import numpy as np
import jax
import jax.numpy as jnp
from jax import lax
from jax.experimental import pallas as pl
from jax.experimental.pallas import tpu as pltpu

F32 = jnp.float32
BF16 = jnp.bfloat16

D_MODEL = 2048
N_MIXERS = 2
CHUNK = 64
EPS = 1e-6

XA_HEADS = 4
XA_WIDTH = D_MODEL // 4
XA_HEAD_DIM = XA_WIDTH // XA_HEADS
MIX_WIDTH = D_MODEL - XA_WIDTH

A_HEADS = 4
A_V_DIM = MIX_WIDTH // A_HEADS
A_QK_DIM = A_V_DIM // 2
A_QK_PAD = 256
A_CONV = 4
A_AUG = 128
A_HEADS_PER_STEP = 4

B_HEADS = 12
B_K_DIM = 128
B_V_DIM = 128
B_LEVELS = 6
B_HEADS_PER_STEP = 12

D_FF = 5632
FFN_CONV = 3

LANES = 128
SUBLANES = 8
VMEM_LIMIT_BYTES = 56 * 1024 * 1024

_NN = (((1,), (0,)), ((), ()))
_NT = (((1,), (1,)), ((), ()))
_TN = (((0,), (0,)), ((), ()))


def _dot(a, b, dims=_NN):
    return lax.dot_general(a, b, dims, preferred_element_type=F32)


def _split2(x):
    hi = x.astype(BF16)
    return hi, (x - hi.astype(F32)).astype(BF16)


def _dot_exact_lhs(a_bf, x, dims=_NN):
    return sum(_dot(a_bf, p, dims) for p in _split2(x))


def _dot_exact_rhs(x, b_bf, dims=_NN):
    return sum(_dot(p, b_bf, dims) for p in _split2(x))


def _sigmoid(x):
    e = jnp.exp(-jnp.abs(x))
    r = 1.0 / (1.0 + e)
    return jnp.where(x >= 0, r, e * r)


def _silu(x):
    return x * _sigmoid(x)


def _log_sigmoid(x):
    return jnp.minimum(x, 0.0) - jnp.log(1.0 + jnp.exp(-jnp.abs(x)))


def _params(n_grid):
    return pltpu.CompilerParams(dimension_semantics=("arbitrary",) * n_grid,
                                vmem_limit_bytes=VMEM_LIMIT_BYTES)


def _resident(shape):
    nd = len(shape)
    return pl.BlockSpec(shape, lambda *_: (0,) * nd, pipeline_mode=pl.Buffered(1))


def _rmsnorm_bf16(x_ref, g_ref):
    x = x_ref[...]
    ms = jnp.mean(x * x, axis=-1, keepdims=True)
    return (x * lax.rsqrt(ms + EPS) * g_ref[...]).astype(BF16)


def _norm_matmul_kernel(x_ref, g_ref, w_ref, out_ref):
    out_ref[...] = _dot(_rmsnorm_bf16(x_ref, g_ref), w_ref[...])


def _norm_matmul(x, g, w_bf, tm):
    t, d = x.shape
    n = w_bf.shape[1]
    return pl.pallas_call(
        _norm_matmul_kernel,
        grid=(t // tm,),
        in_specs=[pl.BlockSpec((tm, d), lambda i: (i, 0)),
                  _resident((1, d)),
                  _resident((d, n))],
        out_specs=pl.BlockSpec((tm, n), lambda i: (i, 0)),
        out_shape=jax.ShapeDtypeStruct((t, n), F32),
        compiler_params=_params(1),
        name="norm_matmul",
    )(x, g.reshape(1, d), w_bf)


def _proj_mlstm_kernel(x_ref, g_ref, w_ref, cw_ref, cb_ref, q_out, k_out, v_out, o_out, xq_out, gt_out, cbuf,
                       *, tm, per_seq):
    i = pl.program_id(0)
    qk_w = 2 * A_HEADS * A_QK_PAD

    @pl.when(i % per_seq == 0)
    def _():
        cbuf[0:SUBLANES] = jnp.zeros((SUBLANES, qk_w), F32)

    hn = _rmsnorm_bf16(x_ref, g_ref)
    cbuf[SUBLANES:SUBLANES + tm] = _dot(hn, w_ref[:, 0:qk_w])
    acc = cb_ref[...]
    for j in range(A_CONV):
        lo = SUBLANES - (A_CONV - 1) + j
        acc = acc + cw_ref[j:j + 1, :] * cbuf[lo:lo + tm, :]
    y = _silu(acc)
    q_out[...] = y[:, :qk_w // 2].astype(BF16)
    k_out[...] = (y[:, qk_w // 2:] * (A_QK_DIM ** -0.5)).astype(BF16)
    cbuf[0:SUBLANES] = cbuf[tm:tm + SUBLANES]
    off = qk_w
    v_out[...] = _dot(hn, w_ref[:, off:off + MIX_WIDTH]).astype(BF16)
    off += MIX_WIDTH
    o_out[...] = _sigmoid(_dot(hn, w_ref[:, off:off + MIX_WIDTH]))
    off += MIX_WIDTH
    xq_out[...] = _dot(hn, w_ref[:, off:off + XA_WIDTH]).astype(BF16)
    off += XA_WIDTH
    gt_out[...] = _dot(hn, w_ref[:, off:off + LANES])


def _proj_mlstm(x, g, w_bf, conv_w, conv_b, seq, tm):
    t, d = x.shape
    n = w_bf.shape[1]
    qk_w = 2 * A_HEADS * A_QK_PAD
    widths = (qk_w // 2, qk_w // 2, MIX_WIDTH, MIX_WIDTH, XA_WIDTH, LANES)
    dtypes = (BF16, BF16, BF16, F32, BF16, F32)
    assert sum(widths) == n
    return pl.pallas_call(
        functools.partial(_proj_mlstm_kernel, tm=tm, per_seq=seq // tm),
        grid=(t // tm,),
        in_specs=[pl.BlockSpec((tm, d), lambda i: (i, 0)),
                  _resident((1, d)),
                  _resident((d, n)),
                  _resident((A_CONV, qk_w)),
                  _resident((1, qk_w))],
        out_specs=[pl.BlockSpec((tm, s), lambda i: (i, 0)) for s in widths],
        out_shape=[jax.ShapeDtypeStruct((t, s), dt) for s, dt in zip(widths, dtypes)],
        scratch_shapes=[pltpu.VMEM((tm + SUBLANES, qk_w), F32)],
        compiler_params=_params(1),
        name="proj_mlstm",
    )(x, g.reshape(1, d), w_bf, conv_w, conv_b)


def _proj_hgrn2_kernel(x_ref, g_ref, w_ref, lbl_ref, q_out, lf_out, kk_out, v_out, sg_out, xq_out, *, layer):
    lg = lbl_ref[...]
    lg = lg - jnp.max(lg, axis=0, keepdims=True)
    pe = jnp.exp(lg)
    pr = pe / jnp.sum(pe, axis=0, keepdims=True)
    c0 = pr[0:1, :]
    cl = c0
    for r in range(1, layer + 1):
        cl = cl + pr[r:r + 1, :]
    lb = cl - c0
    om = 1.0 - lb

    hn = _rmsnorm_bf16(x_ref, g_ref)
    kw = B_HEADS * B_K_DIM
    q_out[...] = _silu(_dot(hn, w_ref[:, 0:kw]))
    fz = _dot(hn, w_ref[:, kw:2 * kw])
    e = jnp.exp(-jnp.abs(fz))
    r = 1.0 / (1.0 + e)
    lf_out[...] = jnp.log2(lb + om * jnp.where(fz >= 0, r, e * r))
    kk_out[...] = om * jnp.where(fz >= 0, e * r, r)
    off = 2 * kw
    v_out[...] = _dot(hn, w_ref[:, off:off + MIX_WIDTH]).astype(BF16)
    off += MIX_WIDTH
    sg_out[...] = _silu(_dot(hn, w_ref[:, off:off + MIX_WIDTH]))
    off += MIX_WIDTH
    xq_out[...] = _dot(hn, w_ref[:, off:off + XA_WIDTH]).astype(BF16)


def _proj_hgrn2(x, g, w_bf, lb_logits, tm, layer):
    t, d = x.shape
    n = w_bf.shape[1]
    kw = B_HEADS * B_K_DIM
    widths = (kw, kw, kw, MIX_WIDTH, MIX_WIDTH, XA_WIDTH)
    dtypes = (F32, F32, F32, BF16, F32, BF16)
    assert 2 * kw + 2 * MIX_WIDTH + XA_WIDTH == n
    return pl.pallas_call(
        functools.partial(_proj_hgrn2_kernel, layer=layer),
        grid=(t // tm,),
        in_specs=[pl.BlockSpec((tm, d), lambda i: (i, 0)),
                  _resident((1, d)),
                  _resident((d, n)),
                  _resident(lb_logits.shape)],
        out_specs=[pl.BlockSpec((tm, s), lambda i: (i, 0)) for s in widths],
        out_shape=[jax.ShapeDtypeStruct((t, s), dt) for s, dt in zip(widths, dtypes)],
        compiler_params=_params(1),
        name="proj_hgrn2",
    )(x, g.reshape(1, d), w_bf, lb_logits)


def _mlstm_kernel(q_ref, k_ref, v_ref, o_ref, gt_ref, gb_ref, hg_ref, tril_ref, out_ref, c_st, m_st, *, tb, hb):
    hgrp = pl.program_id(1)
    t = pl.program_id(2)
    aug_w = A_V_DIM + A_AUG
    rep = aug_w // LANES

    @pl.when(t == 0)
    def _():
        c_st[...] = jnp.zeros_like(c_st)
        m_st[...] = jnp.zeros_like(m_st)

    tril = tril_ref[...]
    row = lax.broadcasted_iota(jnp.int32, (CHUNK, CHUNK), 0)
    col = lax.broadcasted_iota(jnp.int32, (CHUNK, CHUNK), 1)
    causal = row >= col
    wide = hb * LANES
    srow = lax.broadcasted_iota(jnp.int32, (LANES, 2 * wide), 0)
    scol = lax.broadcasted_iota(jnp.int32, (LANES, 2 * wide), 1)
    blk = scol // LANES
    want = jnp.where(blk < hb, hgrp * hb + blk, A_HEADS + hgrp * hb + (blk - hb))
    sel = jnp.where(srow == want, 1.0, 0.0).astype(BF16)
    frow = lax.broadcasted_iota(jnp.int32, (CHUNK, LANES), 1)
    first = jnp.where(frow == 0, 1.0, 0.0).astype(BF16)
    ones_aug = jnp.ones((CHUNK, A_AUG), BF16)
    heads = range(hb)

    def lanes_of(hh):
        return slice(hh * LANES, (hh + 1) * LANES)

    def chunk(c, carry):
        rows = pl.ds(pl.multiple_of(c * CHUNK, CHUNK), CHUNK)
        gates = gt_ref[rows, :] + gb_ref[...]
        g2 = _dot_exact_rhs(gates, sel)
        ic = g2[:, :wide]
        lf = _log_sigmoid(g2[:, wide:])
        gc = _dot_exact_lhs(tril, lf)
        m_prev = m_st[0:1, :]
        a = gc + m_prev
        icg = ic - gc
        r_mat = [_dot_exact_lhs(first, icg[:, lanes_of(hh)], _NT) for hh in heads]
        dmat = [jnp.where(causal, gc[:, hh * LANES:hh * LANES + CHUNK] + r_mat[hh], -jnp.inf) for hh in heads]
        mx = jnp.concatenate([jnp.broadcast_to(jnp.max(dmat[hh], axis=1, keepdims=True), (CHUNK, LANES))
                              for hh in heads], axis=1)
        m_row = jnp.maximum(a, mx)
        w_inter = jnp.exp(a - m_row)
        e_neg = jnp.exp(-m_row)
        p = [jnp.exp(dmat[hh] - m_row[:, hh * LANES:hh * LANES + CHUNK]) for hh in heads]

        g_end = gc[CHUNK - 1:CHUNK, :]
        a_end = g_end + m_prev
        w_end = g_end - gc + ic
        m_new = jnp.maximum(a_end, jnp.max(w_end, axis=0, keepdims=True))
        decay = jnp.exp(a_end - m_new)
        ws = jnp.exp(w_end - m_new).astype(BF16)
        m_st[...] = jnp.broadcast_to(m_new, (SUBLANES, wide))

        qk_lanes = [slice(hh * A_QK_PAD, (hh + 1) * A_QK_PAD) for hh in heads]
        v_lanes = [slice(hh * A_V_DIM, (hh + 1) * A_V_DIM) for hh in heads]
        q = [q_ref[rows, qk_lanes[hh]] for hh in heads]
        k = [k_ref[rows, qk_lanes[hh]] for hh in heads]
        v_aug = [jnp.concatenate([v_ref[rows, v_lanes[hh]], ones_aug], axis=1) for hh in heads]
        qk = [_dot(q[hh], k[hh], _NT) for hh in heads]
        c_old = [c_st[hh] for hh in heads]
        inter = [_dot(q[hh], c_old[hh].astype(BF16)) for hh in heads]
        intra = [_dot((qk[hh] * p[hh]).astype(BF16), v_aug[hh]) for hh in heads]
        for hh in heads:
            kw = k[hh] * jnp.concatenate([ws[:, lanes_of(hh)]] * (A_QK_PAD // LANES), axis=1)
            c_st[hh] = (jnp.concatenate([decay[:, lanes_of(hh)]] * rep, axis=1) * c_old[hh]
                        + _dot(kw, v_aug[hh], _TN))
        for hh in heads:
            tot = jnp.concatenate([w_inter[:, lanes_of(hh)]] * rep, axis=1) * inter[hh] + intra[hh]
            den = tot[:, A_V_DIM:]
            inv = 1.0 / jnp.maximum(jnp.abs(den), e_neg[:, lanes_of(hh)])
            h_out = tot[:, :A_V_DIM] * jnp.concatenate([inv] * (A_V_DIM // LANES), axis=1)
            ms = jnp.mean(h_out * h_out, axis=-1, keepdims=True)
            hn = h_out * lax.rsqrt(ms + EPS) * hg_ref[:, v_lanes[hh]]
            out_ref[rows, v_lanes[hh]] = (o_ref[rows, v_lanes[hh]] * hn).astype(out_ref.dtype)
        return carry

    lax.fori_loop(0, tb // CHUNK, chunk, 0)


def _mlstm(zq, zk, zv, zo, zg, gate_b, head_g, batch, seq, tb, hb):
    t_total = batch * seq
    nt = seq // tb
    tril = jnp.asarray(np.tril(np.ones((CHUNK, CHUNK), np.float32)), BF16)
    row_map = lambda b, h, t: (b * nt + t, h)
    head_map = lambda b, h, t: (0, h)
    fixed = lambda b, h, t: (0, 0)
    return pl.pallas_call(
        functools.partial(_mlstm_kernel, tb=tb, hb=hb),
        grid=(batch, A_HEADS // hb, nt),
        in_specs=[pl.BlockSpec((tb, hb * A_QK_PAD), row_map),
                  pl.BlockSpec((tb, hb * A_QK_PAD), row_map),
                  pl.BlockSpec((tb, hb * A_V_DIM), row_map),
                  pl.BlockSpec((tb, hb * A_V_DIM), row_map),
                  pl.BlockSpec((tb, LANES), lambda b, h, t: (b * nt + t, 0)),
                  pl.BlockSpec((1, LANES), fixed),
                  pl.BlockSpec((1, hb * A_V_DIM), head_map),
                  pl.BlockSpec((CHUNK, CHUNK), fixed)],
        out_specs=pl.BlockSpec((tb, hb * A_V_DIM), row_map),
        out_shape=jax.ShapeDtypeStruct((t_total, MIX_WIDTH), BF16),
        scratch_shapes=[pltpu.VMEM((hb, A_QK_PAD, A_V_DIM + A_AUG), F32),
                        pltpu.VMEM((SUBLANES, hb * LANES), F32)],
        compiler_params=_params(3),
        name="mlstm",
    )(zq, zk, zv, zo, zg, gate_b, head_g, tril)


def _hgrn2_pair_masks():
    n = CHUNK
    masks = [np.eye(n, dtype=np.float32)]
    for j in range(1, B_LEVELS + 1):
        c = n >> j
        pm = np.zeros((n, n), np.float32)
        for r in range(n):
            if r % (2 * c) >= c:
                mid = (r // (2 * c)) * 2 * c + c
                pm[r, mid - c:mid] = 1.0
        masks.append(pm)
    return np.stack(masks, axis=0)


def _hgrn2_kernel(q_ref, lf_ref, kk_ref, v_ref, sg_ref, hg_ref, tril_ref, mask_ref, out_ref, s_st, *, tb, hb):
    t = pl.program_id(2)

    @pl.when(t == 0)
    def _():
        s_st[...] = jnp.zeros_like(s_st)

    tril = tril_ref[...]
    wide = hb * B_K_DIM
    heads = range(hb)
    rowi = lax.broadcasted_iota(jnp.int32, (CHUNK, wide), 0)
    sub = lax.broadcasted_iota(jnp.int32, (CHUNK // SUBLANES, SUBLANES, wide), 1)

    def level_operand(q, kk, gc, lf, half):
        if half >= SUBLANES:
            pieces = []
            for b0 in range(0, CHUNK, 2 * half):
                mid = b0 + half
                ref = gc[mid - 1:mid, :]
                pieces.append(kk[b0:mid, :] * jnp.exp2(ref - gc[b0:mid, :]))
                pieces.append(q[mid:mid + half, :] * jnp.exp2(gc[mid:mid + half, :] - ref))
            return jnp.concatenate(pieces, axis=0)
        if half == 1:
            odd = (rowi & 1) != 0
            return jnp.where(odd, q * jnp.exp2(lf), kk)
        g3 = gc.reshape(CHUNK // SUBLANES, SUBLANES, wide)
        ref = g3[:, half - 1:half, :]
        for b0 in range(2 * half, SUBLANES, 2 * half):
            ref = jnp.where(sub >= b0, g3[:, b0 + half - 1:b0 + half, :], ref)
        d = g3 - ref
        upper = (sub & half) != 0
        ex = jnp.exp2(jnp.where(upper, d, -d)).reshape(CHUNK, wide)
        return jnp.where((rowi & half) != 0, q, kk) * ex

    def chunk(c, carry):
        rows = pl.ds(pl.multiple_of(c * CHUNK, CHUNK), CHUNK)
        q = q_ref[rows, :]
        lf = lf_ref[rows, :]
        kk = kk_ref[rows, :]
        v = v_ref[rows, :]
        gc = _dot_exact_lhs(tril, lf)
        g_end = gc[CHUNK - 1:CHUNK, :]

        xs = [level_operand(q, kk, gc, lf, CHUNK >> j).astype(BF16) for j in range(1, B_LEVELS + 1)]
        q_bf = q.astype(BF16)
        k_bf = kk.astype(BF16)
        qg = (q * jnp.exp2(gc)).astype(BF16)
        kd = (kk * jnp.exp2(g_end - gc)).astype(BF16)
        dec = jnp.exp2(g_end)

        def hl(x, hh):
            return x[:, hh * B_K_DIM:(hh + 1) * B_K_DIM]

        prod = [[_dot(hl(q_bf, hh), hl(k_bf, hh), _NT)] + [_dot(hl(x, hh), hl(x, hh), _NT) for x in xs]
                for hh in heads]
        s_old = [s_st[hh] for hh in heads]
        o_inter = [_dot(hl(qg, hh), s_old[hh].astype(BF16), _NT) for hh in heads]
        for hh in heads:
            s_st[hh] = s_old[hh] * hl(dec, hh) + _dot(hl(v, hh), hl(kd, hh), _TN)
        att = []
        for hh in heads:
            acc = mask_ref[0] * prod[hh][0]
            for j in range(1, B_LEVELS + 1):
                acc = acc + mask_ref[j] * prod[hh][j]
            att.append(acc.astype(BF16))
        outs = []
        for hh in heads:
            o = _dot(att[hh], hl(v, hh)) + o_inter[hh]
            ms = jnp.mean(o * o, axis=-1, keepdims=True)
            outs.append(o * lax.rsqrt(ms + EPS))
        on = jnp.concatenate(outs, axis=1) * hg_ref[...]
        out_ref[rows, :] = (on * sg_ref[rows, :]).astype(out_ref.dtype)
        return carry

    lax.fori_loop(0, tb // CHUNK, chunk, 0)


def _hgrn2(zq, zlf, zkk, zv, zsg, head_g, batch, seq, tb, hb):
    t_total = batch * seq
    nt = seq // tb
    mask_np = _hgrn2_pair_masks()
    tril = jnp.asarray(np.tril(np.ones((CHUNK, CHUNK), np.float32)), BF16)
    row_map = lambda b, h, t: (b * nt + t, h)
    head_map = lambda b, h, t: (0, h)
    return pl.pallas_call(
        functools.partial(_hgrn2_kernel, tb=tb, hb=hb),
        grid=(batch, B_HEADS // hb, nt),
        in_specs=[pl.BlockSpec((tb, hb * B_K_DIM), row_map),
                  pl.BlockSpec((tb, hb * B_K_DIM), row_map),
                  pl.BlockSpec((tb, hb * B_K_DIM), row_map),
                  pl.BlockSpec((tb, hb * B_V_DIM), row_map),
                  pl.BlockSpec((tb, hb * B_V_DIM), row_map),
                  pl.BlockSpec((1, hb * B_V_DIM), head_map),
                  pl.BlockSpec((CHUNK, CHUNK), lambda b, h, t: (0, 0)),
                  pl.BlockSpec(mask_np.shape, lambda b, h, t: (0, 0, 0))],
        out_specs=pl.BlockSpec((tb, hb * B_V_DIM), row_map),
        out_shape=jax.ShapeDtypeStruct((t_total, MIX_WIDTH), BF16),
        scratch_shapes=[pltpu.VMEM((hb, B_V_DIM, B_K_DIM), F32)],
        compiler_params=_params(3),
        name="hgrn2",
    )(zq, zlf, zkk, zv, zsg, head_g, tril, jnp.asarray(mask_np, F32))


def _outproj_kernel(y_ref, xq_ref, kv_ref, wo_ref, h_ref, out_ref):
    kv = kv_ref[...]
    parts = []
    for hh in range(XA_HEADS):
        lo = hh * XA_HEAD_DIM
        qh = xq_ref[:, lo:lo + XA_HEAD_DIM]
        kh = kv[:, lo:lo + XA_HEAD_DIM].astype(BF16)
        vh = kv[:, XA_WIDTH + lo:XA_WIDTH + lo + XA_HEAD_DIM].astype(BF16)
        s = _dot(qh, kh, _NT) * (XA_HEAD_DIM ** -0.5)
        e = jnp.exp(s - jnp.max(s, axis=-1, keepdims=True))
        den = jnp.sum(e, axis=-1, keepdims=True)
        parts.append((_dot(e.astype(BF16), vh) / den).astype(BF16))
    y_mem = jnp.concatenate(parts, axis=1)
    out_ref[...] = (h_ref[...] + _dot(y_ref[...], wo_ref[0:MIX_WIDTH, :])
                    + _dot(y_mem, wo_ref[MIX_WIDTH:D_MODEL, :]))


def _outproj(y_mix, zxq, kv, w_out_bf, h, seq, mem_len, tm):
    t = h.shape[0]
    per_seq = seq // tm
    return pl.pallas_call(
        _outproj_kernel,
        grid=(t // tm,),
        in_specs=[pl.BlockSpec((tm, MIX_WIDTH), lambda i: (i, 0)),
                  pl.BlockSpec((tm, XA_WIDTH), lambda i: (i, 0)),
                  pl.BlockSpec((mem_len, 2 * XA_WIDTH), lambda i: (i // per_seq, 0)),
                  _resident((D_MODEL, D_MODEL)),
                  pl.BlockSpec((tm, D_MODEL), lambda i: (i, 0))],
        out_specs=pl.BlockSpec((tm, D_MODEL), lambda i: (i, 0)),
        out_shape=jax.ShapeDtypeStruct((t, D_MODEL), F32),
        compiler_params=_params(1),
        name="outproj",
    )(y_mix, zxq, kv, w_out_bf, h)


def _ffn_kernel(h_ref, g_ref, wu_ref, wg_ref, cw_ref, cb_ref, wd_ref, og_ref, out_ref,
                hn_s, gbuf, gcarry, act_s, y_s, *, tm, nf, nd, per_seq, final_norm):
    i = pl.program_id(0)
    j = pl.program_id(1)
    tn = y_s.shape[2]

    @pl.when(j == 0)
    def _():
        x = h_ref[...]
        ms = jnp.mean(x * x, axis=-1, keepdims=True)
        hn_s[...] = (x * lax.rsqrt(ms + EPS) * g_ref[...]).astype(BF16)

    @pl.when(j < nf)
    def _():
        hn = hn_s[...]
        g = _dot(hn, wg_ref[...])
        prev = jnp.where(i % per_seq == 0, 0.0, gcarry[j])
        gbuf[0:SUBLANES] = prev
        gbuf[SUBLANES:SUBLANES + tm] = g
        gcarry[j] = g[tm - SUBLANES:tm, :]
        gc = cb_ref[...]
        for k in range(FFN_CONV):
            lo = SUBLANES - (FFN_CONV - 1) + k
            gc = gc + cw_ref[k:k + 1, :] * gbuf[lo:lo + tm, :]
        sg = _silu(gc)
        u = _dot(hn, wu_ref[...])
        act_s[j] = (sg * u).astype(BF16)

    @pl.when(j >= nf)
    def _():
        act = jnp.concatenate([act_s[f] for f in range(nf)], axis=1)
        y_s[j - nf] = _dot(act, wd_ref[...])

    @pl.when(j == nf + nd - 1)
    def _():
        ys = [h_ref[:, n * tn:(n + 1) * tn] + y_s[n] for n in range(nd)]
        if final_norm:
            ssq = ys[0] * ys[0]
            for n in range(1, nd):
                ssq = ssq + ys[n] * ys[n]
            scale = lax.rsqrt(jnp.sum(ssq, axis=-1, keepdims=True) * (1.0 / (nd * tn)) + EPS)
            ys = [ys[n] * scale * og_ref[:, n * tn:(n + 1) * tn] for n in range(nd)]
        for n in range(nd):
            out_ref[:, n * tn:(n + 1) * tn] = ys[n]


def _ffn(h, g, w_up_bf, conv_w, conv_b, w_down_bf, out_g, seq, tm, tf, tn, final_norm):
    t, d = h.shape
    nf = D_FF // tf
    nd = d // tn
    per_seq = seq // tm
    up = lambda j: jnp.minimum(j, nf - 1)
    return pl.pallas_call(
        functools.partial(_ffn_kernel, tm=tm, nf=nf, nd=nd, per_seq=per_seq, final_norm=final_norm),
        grid=(t // tm, nf + nd),
        in_specs=[pl.BlockSpec((tm, d), lambda i, j: (i, 0)),
                  pl.BlockSpec((1, d), lambda i, j: (0, 0)),
                  pl.BlockSpec((d, tf), lambda i, j: (0, up(j))),
                  pl.BlockSpec((d, tf), lambda i, j: (0, nf + up(j))),
                  pl.BlockSpec((FFN_CONV, tf), lambda i, j: (0, up(j))),
                  pl.BlockSpec((1, tf), lambda i, j: (0, up(j))),
                  pl.BlockSpec((D_FF, tn), lambda i, j: (0, jnp.maximum(j - nf, 0))),
                  pl.BlockSpec((1, d), lambda i, j: (0, 0))],
        out_specs=pl.BlockSpec((tm, d), lambda i, j: (i, 0)),
        out_shape=jax.ShapeDtypeStruct((t, d), F32),
        scratch_shapes=[pltpu.VMEM((tm, d), BF16),
                        pltpu.VMEM((tm + SUBLANES, tf), F32),
                        pltpu.VMEM((nf, SUBLANES, tf), F32),
                        pltpu.VMEM((nf, tm, tf), BF16),
                        pltpu.VMEM((nd, tm, tn), F32)],
        compiler_params=_params(2),
        name="ffn",
    )(h, g.reshape(1, d), w_up_bf, w_up_bf, conv_w, conv_b.reshape(1, D_FF), w_down_bf, out_g.reshape(1, d))


def _pad_heads(w, heads, dim, pad):
    lead = w.shape[:-1]
    w = w.reshape(lead + (heads, dim))
    w = jnp.pad(w, [(0, 0)] * len(lead) + [(0, 0), (0, pad - dim)])
    return w.reshape(lead + (heads * pad,))


def _mlstm_weights(w_in, gate_b, conv_w, conv_b):
    qk_w = A_HEADS * A_QK_DIM
    o0 = 2 * qk_w
    wq = _pad_heads(w_in[:, :qk_w], A_HEADS, A_QK_DIM, A_QK_PAD)
    wk = _pad_heads(w_in[:, qk_w:o0], A_HEADS, A_QK_DIM, A_QK_PAD)
    wv = w_in[:, o0:o0 + MIX_WIDTH]
    wo = w_in[:, o0 + MIX_WIDTH:o0 + 2 * MIX_WIDTH]
    wg = jnp.pad(w_in[:, o0 + 2 * MIX_WIDTH:o0 + 2 * MIX_WIDTH + 2 * A_HEADS], ((0, 0), (0, LANES - 2 * A_HEADS)))
    wx = w_in[:, o0 + 2 * MIX_WIDTH + 2 * A_HEADS:]
    w_all = jnp.concatenate([wq, wk, wv, wo, wx, wg], axis=1).astype(BF16)
    cw = jnp.concatenate([_pad_heads(conv_w[:, :qk_w], A_HEADS, A_QK_DIM, A_QK_PAD),
                          _pad_heads(conv_w[:, qk_w:], A_HEADS, A_QK_DIM, A_QK_PAD)], axis=1)
    cb = jnp.concatenate([_pad_heads(conv_b[None, :qk_w], A_HEADS, A_QK_DIM, A_QK_PAD),
                          _pad_heads(conv_b[None, qk_w:], A_HEADS, A_QK_DIM, A_QK_PAD)], axis=1)
    gb = jnp.pad(gate_b[None, :], ((0, 0), (0, LANES - 2 * A_HEADS)))
    return w_all, cw, cb, gb


def _pick(n, cap):
    b = min(n, cap)
    while n % b:
        b //= 2
    return b


def kernel(x, mem, norm_mix_g, norm_mem_g, norm_ffn_g, norm_out_g, w_mem_kv, a_w_in, a_gate_b, a_conv_w, a_conv_b, a_head_g, a_w_out, b_w_in, b_lb_logits, b_head_g, b_w_out, ffn_w_up, ffn_conv_w, ffn_conv_b, ffn_w_down):
    batch, seq, d = x.shape
    mem_len = mem.shape[1]
    depth = norm_mix_g.shape[0]
    t = batch * seq
    tm_proj = _pick(seq, 256)
    tm_out = _pick(seq, 512)
    tm_ffn = _pick(seq, 512)
    tb = _pick(seq, 256)

    h = x.reshape(t, d)
    mem2 = mem.reshape(batch * mem_len, d)
    for layer in range(depth):
        j = layer // N_MIXERS
        kv = _norm_matmul(mem2, norm_mem_g[layer], w_mem_kv[layer].astype(BF16), _pick(mem_len, 256))
        if layer % N_MIXERS == 0:
            w_all, cw, cb, gb = _mlstm_weights(a_w_in[j], a_gate_b[j], a_conv_w[j], a_conv_b[j])
            zq, zk, zv, zo, zxq, zg = _proj_mlstm(h, norm_mix_g[layer], w_all, cw, cb, seq, tm_proj)
            y_mix = _mlstm(zq, zk, zv, zo, zg, gb, a_head_g[j].reshape(1, MIX_WIDTH), batch, seq, tb,
                           A_HEADS_PER_STEP)
            w_out = a_w_out[j]
        else:
            zq, zlf, zkk, zv, zsg, zxq = _proj_hgrn2(h, norm_mix_g[layer], b_w_in[j].astype(BF16), b_lb_logits,
                                                     tm_proj, layer)
            y_mix = _hgrn2(zq, zlf, zkk, zv, zsg, b_head_g[j].reshape(1, MIX_WIDTH), batch, seq, tb,
                           B_HEADS_PER_STEP)
            w_out = b_w_out[j]
        h = _outproj(y_mix, zxq, kv, w_out.astype(BF16), h, seq, mem_len, tm_out)
        h = _ffn(h, norm_ffn_g[layer], ffn_w_up[layer].astype(BF16), ffn_conv_w[layer], ffn_conv_b[layer],
                 ffn_w_down[layer].astype(BF16), norm_out_g, seq, tm_ffn, 512, 512, layer == depth - 1)
    return h.reshape(batch, seq, d)
```

```python
import functools

import numpy as np
import jax
import jax.numpy as jnp
from jax import lax
from jax.experimental import pallas as pl
from jax.experimental.pallas import tpu as pltpu

F32 = jnp.float32
BF16 = jnp.bfloat16

D_MODEL = 2048
N_MIXERS = 2
CHUNK = 64
EPS = 1e-6

XA_HEADS = 4
XA_WIDTH = D_MODEL // 4
XA_HEAD_DIM = XA_WIDTH // XA_HEADS
MIX_WIDTH = D_MODEL - XA_WIDTH

A_HEADS = 4
A_V_DIM = MIX_WIDTH // A_HEADS
A_QK_DIM = A_V_DIM // 2
A_QK_PAD = 256
A_CONV = 4
A_AUG = 128
A_HEADS_PER_STEP = 4

B_HEADS = 12
B_K_DIM = 128
B_V_DIM = 128
B_LEVELS = 6
B_HEADS_PER_STEP = 12

D_FF = 5632
FFN_CONV = 3

LANES = 128
SUBLANES = 8
VMEM_LIMIT_BYTES = 56 * 1024 * 1024

_NN = (((1,), (0,)), ((), ()))
_NT = (((1,), (1,)), ((), ()))
_TN = (((0,), (0,)), ((), ()))


def _dot(a, b, dims=_NN):
    return lax.dot_general(a, b, dims, preferred_element_type=F32)


def _split2(x):
    hi = x.astype(BF16)
    return hi, (x - hi.astype(F32)).astype(BF16)


def _dot_exact_lhs(a_bf, x, dims=_NN):
    return sum(_dot(a_bf, p, dims) for p in _split2(x))


def _dot_exact_rhs(x, b_bf, dims=_NN):
    return sum(_dot(p, b_bf, dims) for p in _split2(x))


def _sigmoid(x):
    return 0.5 + 0.5 * jnp.tanh(0.5 * x)


def _silu(x):
    hx = 0.5 * x
    return hx + hx * jnp.tanh(hx)


def _log_sigmoid(x):
    return jnp.minimum(x, 0.0) - jnp.log(1.0 + jnp.exp(-jnp.abs(x)))


def _params(n_grid):
    return pltpu.CompilerParams(dimension_semantics=("arbitrary",) * n_grid,
                                vmem_limit_bytes=VMEM_LIMIT_BYTES)


def _resident(shape):
    nd = len(shape)
    return pl.BlockSpec(shape, lambda *_: (0,) * nd, pipeline_mode=pl.Buffered(1))


def _gain_folded(w, g):
    return (g[:, None] * w).astype(BF16)


def _rms_split(x_ref):
    x = x_ref[...]
    return x.astype(BF16), lax.rsqrt(jnp.mean(x * x, axis=-1, keepdims=True) + EPS)


def _norm_matmul_kernel(x_ref, w_ref, out_ref):
    x_bf, rs = _rms_split(x_ref)
    out_ref[...] = _dot(x_bf, w_ref[...]) * rs


def _norm_matmul(x, wg_bf, tm):
    t, d = x.shape
    n = wg_bf.shape[1]
    return pl.pallas_call(
        _norm_matmul_kernel,
        grid=(t // tm,),
        in_specs=[pl.BlockSpec((tm, d), lambda i: (i, 0)),
                  _resident((d, n))],
        out_specs=pl.BlockSpec((tm, n), lambda i: (i, 0)),
        out_shape=jax.ShapeDtypeStruct((t, n), F32),
        compiler_params=_params(1),
        name="norm_matmul",
    )(x, wg_bf)


def _proj_mlstm_kernel(x_ref, w_ref, cw_ref, cb_ref, q_out, k_out, v_out, o_out, xq_out, gt_out, cbuf,
                       *, tm, per_seq):
    i = pl.program_id(0)
    qk_w = 2 * A_HEADS * A_QK_PAD

    @pl.when(i % per_seq == 0)
    def _():
        cbuf[0:SUBLANES] = jnp.zeros((SUBLANES, qk_w), F32)

    x_bf, rs = _rms_split(x_ref)

    def proj(off, width):
        return _dot(x_bf, w_ref[:, off:off + width]) * rs

    cbuf[SUBLANES:SUBLANES + tm] = proj(0, qk_w)
    acc = cb_ref[...]
    for j in range(A_CONV):
        lo = SUBLANES - (A_CONV - 1) + j
        acc = acc + cw_ref[j:j + 1, :] * cbuf[lo:lo + tm, :]
    y = _silu(acc)
    q_out[...] = y[:, :qk_w // 2].astype(BF16)
    k_out[...] = (y[:, qk_w // 2:] * (A_QK_DIM ** -0.5)).astype(BF16)
    cbuf[0:SUBLANES] = cbuf[tm:tm + SUBLANES]
    off = qk_w
    v_out[...] = proj(off, MIX_WIDTH).astype(BF16)
    off += MIX_WIDTH
    o_out[...] = _sigmoid(proj(off, MIX_WIDTH))
    off += MIX_WIDTH
    xq_out[...] = proj(off, XA_WIDTH).astype(BF16)
    off += XA_WIDTH
    gt_out[...] = proj(off, LANES)


def _proj_mlstm(x, w_bf, conv_w, conv_b, seq, tm):
    t, d = x.shape
    n = w_bf.shape[1]
    qk_w = 2 * A_HEADS * A_QK_PAD
    widths = (qk_w // 2, qk_w // 2, MIX_WIDTH, MIX_WIDTH, XA_WIDTH, LANES)
    dtypes = (BF16, BF16, BF16, F32, BF16, F32)
    assert sum(widths) == n
    return pl.pallas_call(
        functools.partial(_proj_mlstm_kernel, tm=tm, per_seq=seq // tm),
        grid=(t // tm,),
        in_specs=[pl.BlockSpec((tm, d), lambda i: (i, 0)),
                  _resident((d, n)),
                  _resident((A_CONV, qk_w)),
                  _resident((1, qk_w))],
        out_specs=[pl.BlockSpec((tm, s), lambda i: (i, 0)) for s in widths],
        out_shape=[jax.ShapeDtypeStruct((t, s), dt) for s, dt in zip(widths, dtypes)],
        scratch_shapes=[pltpu.VMEM((tm + SUBLANES, qk_w), F32)],
        compiler_params=_params(1),
        name="proj_mlstm",
    )(x, w_bf, conv_w, conv_b)


def _proj_hgrn2_kernel(x_ref, w_ref, lbl_ref, q_out, lf_out, kk_out, v_out, sg_out, xq_out, *, layer):
    lg = lbl_ref[...]
    lg = lg - jnp.max(lg, axis=0, keepdims=True)
    pe = jnp.exp(lg)
    pr = pe / jnp.sum(pe, axis=0, keepdims=True)
    c0 = pr[0:1, :]
    cl = c0
    for r in range(1, layer + 1):
        cl = cl + pr[r:r + 1, :]
    lb = cl - c0
    om = 1.0 - lb

    x_bf, rs = _rms_split(x_ref)
    kw = B_HEADS * B_K_DIM

    def proj(off, width):
        return _dot(x_bf, w_ref[:, off:off + width]) * rs

    q_out[...] = _silu(proj(0, kw))
    th = 0.5 * jnp.tanh(0.5 * proj(kw, kw))
    lf_out[...] = jnp.log2(lb + om * (0.5 + th))
    kk_out[...] = om * (0.5 - th)
    off = 2 * kw
    v_out[...] = proj(off, MIX_WIDTH).astype(BF16)
    off += MIX_WIDTH
    sg_out[...] = _silu(proj(off, MIX_WIDTH))
    off += MIX_WIDTH
    xq_out[...] = proj(off, XA_WIDTH).astype(BF16)


def _proj_hgrn2(x, w_bf, lb_logits, tm, layer):
    t, d = x.shape
    n = w_bf.shape[1]
    kw = B_HEADS * B_K_DIM
    widths = (kw, kw, kw, MIX_WIDTH, MIX_WIDTH, XA_WIDTH)
    dtypes = (F32, F32, F32, BF16, F32, BF16)
    assert 2 * kw + 2 * MIX_WIDTH + XA_WIDTH == n
    return pl.pallas_call(
        functools.partial(_proj_hgrn2_kernel, layer=layer),
        grid=(t // tm,),
        in_specs=[pl.BlockSpec((tm, d), lambda i: (i, 0)),
                  _resident((d, n)),
                  _resident(lb_logits.shape)],
        out_specs=[pl.BlockSpec((tm, s), lambda i: (i, 0)) for s in widths],
        out_shape=[jax.ShapeDtypeStruct((t, s), dt) for s, dt in zip(widths, dtypes)],
        compiler_params=_params(1),
        name="proj_hgrn2",
    )(x, w_bf, lb_logits)


def _mlstm_kernel(q_ref, k_ref, v_ref, o_ref, gt_ref, gb_ref, hg_ref, tril_ref, out_ref, c_st, m_st, *, tb, hb):
    hgrp = pl.program_id(1)
    t = pl.program_id(2)
    aug_w = A_V_DIM + A_AUG
    rep = aug_w // LANES

    @pl.when(t == 0)
    def _():
        c_st[...] = jnp.zeros_like(c_st)
        m_st[...] = jnp.zeros_like(m_st)

    tril = tril_ref[...]
    row = lax.broadcasted_iota(jnp.int32, (CHUNK, CHUNK), 0)
    col = lax.broadcasted_iota(jnp.int32, (CHUNK, CHUNK), 1)
    causal = row >= col
    wide = hb * LANES
    srow = lax.broadcasted_iota(jnp.int32, (LANES, 2 * wide), 0)
    scol = lax.broadcasted_iota(jnp.int32, (LANES, 2 * wide), 1)
    blk = scol // LANES
    want = jnp.where(blk < hb, hgrp * hb + blk, A_HEADS + hgrp * hb + (blk - hb))
    sel = jnp.where(srow == want, 1.0, 0.0).astype(BF16)
    frow = lax.broadcasted_iota(jnp.int32, (CHUNK, LANES), 1)
    first = jnp.where(frow == 0, 1.0, 0.0).astype(BF16)
    ones_aug = jnp.ones((CHUNK, A_AUG), BF16)
    heads = range(hb)

    def lanes_of(hh):
        return slice(hh * LANES, (hh + 1) * LANES)

    def chunk(c, carry):
        rows = pl.ds(pl.multiple_of(c * CHUNK, CHUNK), CHUNK)
        gates = gt_ref[rows, :] + gb_ref[...]
        g2 = _dot_exact_rhs(gates, sel)
        ic = g2[:, :wide]
        lf = _log_sigmoid(g2[:, wide:])
        gc = _dot_exact_lhs(tril, lf)
        m_prev = m_st[0:1, :]
        a = gc + m_prev
        icg = ic - gc
        r_mat = [_dot_exact_lhs(first, icg[:, lanes_of(hh)], _NT) for hh in heads]
        dmat = [jnp.where(causal, gc[:, hh * LANES:hh * LANES + CHUNK] + r_mat[hh], -jnp.inf) for hh in heads]
        mx = jnp.concatenate([jnp.broadcast_to(jnp.max(dmat[hh], axis=1, keepdims=True), (CHUNK, LANES))
                              for hh in heads], axis=1)
        m_row = jnp.maximum(a, mx)
        w_inter = jnp.exp(a - m_row)
        e_neg = jnp.exp(-m_row)
        p = [jnp.exp(dmat[hh] - m_row[:, hh * LANES:hh * LANES + CHUNK]) for hh in heads]

        g_end = gc[CHUNK - 1:CHUNK, :]
        a_end = g_end + m_prev
        w_end = g_end - gc + ic
        m_new = jnp.maximum(a_end, jnp.max(w_end, axis=0, keepdims=True))
        decay = jnp.exp(a_end - m_new)
        ws = jnp.exp(w_end - m_new).astype(BF16)
        m_st[...] = jnp.broadcast_to(m_new, (SUBLANES, wide))

        qk_lanes = [slice(hh * A_QK_PAD, (hh + 1) * A_QK_PAD) for hh in heads]
        v_lanes = [slice(hh * A_V_DIM, (hh + 1) * A_V_DIM) for hh in heads]
        q = [q_ref[rows, qk_lanes[hh]] for hh in heads]
        k = [k_ref[rows, qk_lanes[hh]] for hh in heads]
        v_aug = [jnp.concatenate([v_ref[rows, v_lanes[hh]], ones_aug], axis=1) for hh in heads]
        qk = [_dot(q[hh], k[hh], _NT) for hh in heads]
        c_old = [c_st[hh] for hh in heads]
        inter = [_dot(q[hh], c_old[hh].astype(BF16)) for hh in heads]
        intra = [_dot((qk[hh] * p[hh]).astype(BF16), v_aug[hh]) for hh in heads]
        for hh in heads:
            kw = k[hh] * jnp.concatenate([ws[:, lanes_of(hh)]] * (A_QK_PAD // LANES), axis=1)
            c_st[hh] = (jnp.concatenate([decay[:, lanes_of(hh)]] * rep, axis=1) * c_old[hh]
                        + _dot(kw, v_aug[hh], _TN))
        for hh in heads:
            tot = jnp.concatenate([w_inter[:, lanes_of(hh)]] * rep, axis=1) * inter[hh] + intra[hh]
            den = tot[:, A_V_DIM:]
            inv = 1.0 / jnp.maximum(jnp.abs(den), e_neg[:, lanes_of(hh)])
            h_out = tot[:, :A_V_DIM] * jnp.concatenate([inv] * (A_V_DIM // LANES), axis=1)
            ms = jnp.mean(h_out * h_out, axis=-1, keepdims=True)
            hn = h_out * lax.rsqrt(ms + EPS) * hg_ref[:, v_lanes[hh]]
            out_ref[rows, v_lanes[hh]] = (o_ref[rows, v_lanes[hh]] * hn).astype(out_ref.dtype)
        return carry

    lax.fori_loop(0, tb // CHUNK, chunk, 0)


def _mlstm(zq, zk, zv, zo, zg, gate_b, head_g, batch, seq, tb, hb):
    t_total = batch * seq
    nt = seq // tb
    tril = jnp.asarray(np.tril(np.ones((CHUNK, CHUNK), np.float32)), BF16)
    row_map = lambda b, h, t: (b * nt + t, h)
    head_map = lambda b, h, t: (0, h)
    fixed = lambda b, h, t: (0, 0)
    return pl.pallas_call(
        functools.partial(_mlstm_kernel, tb=tb, hb=hb),
        grid=(batch, A_HEADS // hb, nt),
        in_specs=[pl.BlockSpec((tb, hb * A_QK_PAD), row_map),
                  pl.BlockSpec((tb, hb * A_QK_PAD), row_map),
                  pl.BlockSpec((tb, hb * A_V_DIM), row_map),
                  pl.BlockSpec((tb, hb * A_V_DIM), row_map),
                  pl.BlockSpec((tb, LANES), lambda b, h, t: (b * nt + t, 0)),
                  pl.BlockSpec((1, LANES), fixed),
                  pl.BlockSpec((1, hb * A_V_DIM), head_map),
                  pl.BlockSpec((CHUNK, CHUNK), fixed)],
        out_specs=pl.BlockSpec((tb, hb * A_V_DIM), row_map),
        out_shape=jax.ShapeDtypeStruct((t_total, MIX_WIDTH), BF16),
        scratch_shapes=[pltpu.VMEM((hb, A_QK_PAD, A_V_DIM + A_AUG), F32),
                        pltpu.VMEM((SUBLANES, hb * LANES), F32)],
        compiler_params=_params(3),
        name="mlstm",
    )(zq, zk, zv, zo, zg, gate_b, head_g, tril)


def _hgrn2_pair_masks():
    n = CHUNK
    masks = [np.eye(n, dtype=np.float32)]
    for j in range(1, B_LEVELS + 1):
        c = n >> j
        pm = np.zeros((n, n), np.float32)
        for r in range(n):
            if r % (2 * c) >= c:
                mid = (r // (2 * c)) * 2 * c + c
                pm[r, mid - c:mid] = 1.0
        masks.append(pm)
    return np.stack(masks, axis=0)


def _hgrn2_kernel(q_ref, lf_ref, kk_ref, v_ref, sg_ref, hg_ref, tril_ref, mask_ref, out_ref, s_st, *, tb, hb):
    t = pl.program_id(2)

    @pl.when(t == 0)
    def _():
        s_st[...] = jnp.zeros_like(s_st)

    tril = tril_ref[...]
    wide = hb * B_K_DIM
    heads = range(hb)
    rowi = lax.broadcasted_iota(jnp.int32, (CHUNK, wide), 0)
    sub = lax.broadcasted_iota(jnp.int32, (CHUNK // SUBLANES, SUBLANES, wide), 1)

    def level_operand(q, kk, gc, lf, half):
        if half >= SUBLANES:
            pieces = []
            for b0 in range(0, CHUNK, 2 * half):
                mid = b0 + half
                ref = gc[mid - 1:mid, :]
                pieces.append(kk[b0:mid, :] * jnp.exp2(ref - gc[b0:mid, :]))
                pieces.append(q[mid:mid + half, :] * jnp.exp2(gc[mid:mid + half, :] - ref))
            return jnp.concatenate(pieces, axis=0)
        if half == 1:
            odd = (rowi & 1) != 0
            return jnp.where(odd, q * jnp.exp2(lf), kk)
        g3 = gc.reshape(CHUNK // SUBLANES, SUBLANES, wide)
        ref = g3[:, half - 1:half, :]
        for b0 in range(2 * half, SUBLANES, 2 * half):
            ref = jnp.where(sub >= b0, g3[:, b0 + half - 1:b0 + half, :], ref)
        d = g3 - ref
        upper = (sub & half) != 0
        ex = jnp.exp2(jnp.where(upper, d, -d)).reshape(CHUNK, wide)
        return jnp.where((rowi & half) != 0, q, kk) * ex

    def chunk(c, carry):
        rows = pl.ds(pl.multiple_of(c * CHUNK, CHUNK), CHUNK)
        q = q_ref[rows, :]
        lf = lf_ref[rows, :]
        kk = kk_ref[rows, :]
        v = v_ref[rows, :]
        gc = _dot_exact_lhs(tril, lf)
        g_end = gc[CHUNK - 1:CHUNK, :]

        xs = [level_operand(q, kk, gc, lf, CHUNK >> j).astype(BF16) for j in range(1, B_LEVELS + 1)]
        q_bf = q.astype(BF16)
        k_bf = kk.astype(BF16)
        qg = (q * jnp.exp2(gc)).astype(BF16)
        kd = (kk * jnp.exp2(g_end - gc)).astype(BF16)
        dec = jnp.exp2(g_end)

        def hl(x, hh):
            return x[:, hh * B_K_DIM:(hh + 1) * B_K_DIM]

        prod = [[_dot(hl(q_bf, hh), hl(k_bf, hh), _NT)] + [_dot(hl(x, hh), hl(x, hh), _NT) for x in xs]
                for hh in heads]
        s_old = [s_st[hh] for hh in heads]
        o_inter = [_dot(hl(qg, hh), s_old[hh].astype(BF16), _NT) for hh in heads]
        for hh in heads:
            s_st[hh] = s_old[hh] * hl(dec, hh) + _dot(hl(v, hh), hl(kd, hh), _TN)
        att = []
        for hh in heads:
            acc = mask_ref[0] * prod[hh][0]
            for j in range(1, B_LEVELS + 1):
                acc = acc + mask_ref[j] * prod[hh][j]
            att.append(acc.astype(BF16))
        outs = []
        for hh in heads:
            o = _dot(att[hh], hl(v, hh)) + o_inter[hh]
            ms = jnp.mean(o * o, axis=-1, keepdims=True)
            outs.append(o * lax.rsqrt(ms + EPS))
        on = jnp.concatenate(outs, axis=1) * hg_ref[...]
        out_ref[rows, :] = (on * sg_ref[rows, :]).astype(out_ref.dtype)
        return carry

    lax.fori_loop(0, tb // CHUNK, chunk, 0)


def _hgrn2(zq, zlf, zkk, zv, zsg, head_g, batch, seq, tb, hb):
    t_total = batch * seq
    nt = seq // tb
    mask_np = _hgrn2_pair_masks()
    tril = jnp.asarray(np.tril(np.ones((CHUNK, CHUNK), np.float32)), BF16)
    row_map = lambda b, h, t: (b * nt + t, h)
    head_map = lambda b, h, t: (0, h)
    return pl.pallas_call(
        functools.partial(_hgrn2_kernel, tb=tb, hb=hb),
        grid=(batch, B_HEADS // hb, nt),
        in_specs=[pl.BlockSpec((tb, hb * B_K_DIM), row_map),
                  pl.BlockSpec((tb, hb * B_K_DIM), row_map),
                  pl.BlockSpec((tb, hb * B_K_DIM), row_map),
                  pl.BlockSpec((tb, hb * B_V_DIM), row_map),
                  pl.BlockSpec((tb, hb * B_V_DIM), row_map),
                  pl.BlockSpec((1, hb * B_V_DIM), head_map),
                  pl.BlockSpec((CHUNK, CHUNK), lambda b, h, t: (0, 0)),
                  pl.BlockSpec(mask_np.shape, lambda b, h, t: (0, 0, 0))],
        out_specs=pl.BlockSpec((tb, hb * B_V_DIM), row_map),
        out_shape=jax.ShapeDtypeStruct((t_total, MIX_WIDTH), BF16),
        scratch_shapes=[pltpu.VMEM((hb, B_V_DIM, B_K_DIM), F32)],
        compiler_params=_params(3),
        name="hgrn2",
    )(zq, zlf, zkk, zv, zsg, head_g, tril, jnp.asarray(mask_np, F32))


def _outproj_kernel(y_ref, xq_ref, kv_ref, wo_ref, h_ref, out_ref):
    kv = kv_ref[...]
    parts = []
    for hh in range(XA_HEADS):
        lo = hh * XA_HEAD_DIM
        qh = xq_ref[:, lo:lo + XA_HEAD_DIM]
        kh = kv[:, lo:lo + XA_HEAD_DIM].astype(BF16)
        vh = kv[:, XA_WIDTH + lo:XA_WIDTH + lo + XA_HEAD_DIM].astype(BF16)
        s = _dot(qh, kh, _NT) * (XA_HEAD_DIM ** -0.5)
        e = jnp.exp(s - jnp.max(s, axis=-1, keepdims=True))
        den = jnp.sum(e, axis=-1, keepdims=True)
        parts.append((_dot(e.astype(BF16), vh) / den).astype(BF16))
    y_mem = jnp.concatenate(parts, axis=1)
    out_ref[...] = (h_ref[...] + _dot(y_ref[...], wo_ref[0:MIX_WIDTH, :])
                    + _dot(y_mem, wo_ref[MIX_WIDTH:D_MODEL, :]))


def _outproj(y_mix, zxq, kv, w_out_bf, h, seq, mem_len, tm):
    t = h.shape[0]
    per_seq = seq // tm
    return pl.pallas_call(
        _outproj_kernel,
        grid=(t // tm,),
        in_specs=[pl.BlockSpec((tm, MIX_WIDTH), lambda i: (i, 0)),
                  pl.BlockSpec((tm, XA_WIDTH), lambda i: (i, 0)),
                  pl.BlockSpec((mem_len, 2 * XA_WIDTH), lambda i: (i // per_seq, 0)),
                  _resident((D_MODEL, D_MODEL)),
                  pl.BlockSpec((tm, D_MODEL), lambda i: (i, 0))],
        out_specs=pl.BlockSpec((tm, D_MODEL), lambda i: (i, 0)),
        out_shape=jax.ShapeDtypeStruct((t, D_MODEL), F32),
        compiler_params=_params(1),
        name="outproj",
    )(y_mix, zxq, kv, w_out_bf, h)


def _ffn_kernel(h_ref, wu_ref, wg_ref, cw_ref, cb_ref, wd_ref, og_ref, out_ref,
                hn_s, rs_s, gbuf, gcarry, act_s, y_s, *, tm, nf, nd, per_seq, final_norm):
    i = pl.program_id(0)
    j = pl.program_id(1)
    tn = y_s.shape[2]
    tf = act_s.shape[2]

    @pl.when(j == 0)
    def _():
        x_bf, rs = _rms_split(h_ref)
        hn_s[...] = x_bf
        rs_s[...] = jnp.broadcast_to(rs, rs_s.shape)

    @pl.when(j < nf)
    def _():
        hn = hn_s[...]
        rs = jnp.concatenate([rs_s[...]] * (tf // LANES), axis=1)
        g = _dot(hn, wg_ref[...]) * rs
        prev = jnp.where(i % per_seq == 0, 0.0, gcarry[j])
        gbuf[0:SUBLANES] = prev
        gbuf[SUBLANES:SUBLANES + tm] = g
        gcarry[j] = g[tm - SUBLANES:tm, :]
        gc = cb_ref[...]
        for k in range(FFN_CONV):
            lo = SUBLANES - (FFN_CONV - 1) + k
            gc = gc + cw_ref[k:k + 1, :] * gbuf[lo:lo + tm, :]
        sg = _silu(gc) * rs
        u = _dot(hn, wu_ref[...])
        act_s[j] = (sg * u).astype(BF16)

    @pl.when(j >= nf)
    def _():
        act = jnp.concatenate([act_s[f] for f in range(nf)], axis=1)
        y_s[j - nf] = _dot(act, wd_ref[...])

    @pl.when(j == nf + nd - 1)
    def _():
        ys = [h_ref[:, n * tn:(n + 1) * tn] + y_s[n] for n in range(nd)]
        if final_norm:
            ssq = ys[0] * ys[0]
            for n in range(1, nd):
                ssq = ssq + ys[n] * ys[n]
            scale = lax.rsqrt(jnp.sum(ssq, axis=-1, keepdims=True) * (1.0 / (nd * tn)) + EPS)
            ys = [ys[n] * scale * og_ref[:, n * tn:(n + 1) * tn] for n in range(nd)]
        for n in range(nd):
            out_ref[:, n * tn:(n + 1) * tn] = ys[n]


def _ffn(h, w_up_bf, conv_w, conv_b, w_down_bf, out_g, seq, tm, tf, tn, final_norm):
    t, d = h.shape
    nf = D_FF // tf
    nd = d // tn
    per_seq = seq // tm
    up = lambda j: jnp.minimum(j, nf - 1)
    return pl.pallas_call(
        functools.partial(_ffn_kernel, tm=tm, nf=nf, nd=nd, per_seq=per_seq, final_norm=final_norm),
        grid=(t // tm, nf + nd),
        in_specs=[pl.BlockSpec((tm, d), lambda i, j: (i, 0)),
                  pl.BlockSpec((d, tf), lambda i, j: (0, up(j))),
                  pl.BlockSpec((d, tf), lambda i, j: (0, nf + up(j))),
                  pl.BlockSpec((FFN_CONV, tf), lambda i, j: (0, up(j))),
                  pl.BlockSpec((1, tf), lambda i, j: (0, up(j))),
                  pl.BlockSpec((D_FF, tn), lambda i, j: (0, jnp.maximum(j - nf, 0))),
                  pl.BlockSpec((1, d), lambda i, j: (0, 0))],
        out_specs=pl.BlockSpec((tm, d), lambda i, j: (i, 0)),
        out_shape=jax.ShapeDtypeStruct((t, d), F32),
        scratch_shapes=[pltpu.VMEM((tm, d), BF16),
                        pltpu.VMEM((tm, LANES), F32),
                        pltpu.VMEM((tm + SUBLANES, tf), F32),
                        pltpu.VMEM((nf, SUBLANES, tf), F32),
                        pltpu.VMEM((nf, tm, tf), BF16),
                        pltpu.VMEM((nd, tm, tn), F32)],
        compiler_params=_params(2),
        name="ffn",
    )(h, w_up_bf, w_up_bf, conv_w, conv_b.reshape(1, D_FF), w_down_bf, out_g.reshape(1, d))


def _pad_heads(w, heads, dim, pad):
    lead = w.shape[:-1]
    w = w.reshape(lead + (heads, dim))
    w = jnp.pad(w, [(0, 0)] * len(lead) + [(0, 0), (0, pad - dim)])
    return w.reshape(lead + (heads * pad,))


def _mlstm_weights(w_in, gate_b, conv_w, conv_b):
    qk_w = A_HEADS * A_QK_DIM
    o0 = 2 * qk_w
    wq = _pad_heads(w_in[:, :qk_w], A_HEADS, A_QK_DIM, A_QK_PAD)
    wk = _pad_heads(w_in[:, qk_w:o0], A_HEADS, A_QK_DIM, A_QK_PAD)
    wv = w_in[:, o0:o0 + MIX_WIDTH]
    wo = w_in[:, o0 + MIX_WIDTH:o0 + 2 * MIX_WIDTH]
    wg = jnp.pad(w_in[:, o0 + 2 * MIX_WIDTH:o0 + 2 * MIX_WIDTH + 2 * A_HEADS], ((0, 0), (0, LANES - 2 * A_HEADS)))
    wx = w_in[:, o0 + 2 * MIX_WIDTH + 2 * A_HEADS:]
    w_all = jnp.concatenate([wq, wk, wv, wo, wx, wg], axis=1).astype(BF16)
    cw = jnp.concatenate([_pad_heads(conv_w[:, :qk_w], A_HEADS, A_QK_DIM, A_QK_PAD),
                          _pad_heads(conv_w[:, qk_w:], A_HEADS, A_QK_DIM, A_QK_PAD)], axis=1)
    cb = jnp.concatenate([_pad_heads(conv_b[None, :qk_w], A_HEADS, A_QK_DIM, A_QK_PAD),
                          _pad_heads(conv_b[None, qk_w:], A_HEADS, A_QK_DIM, A_QK_PAD)], axis=1)
    gb = jnp.pad(gate_b[None, :], ((0, 0), (0, LANES - 2 * A_HEADS)))
    return w_all, cw, cb, gb


def _pick(n, cap):
    b = min(n, cap)
    while n % b:
        b //= 2
    return b


def kernel(x, mem, norm_mix_g, norm_mem_g, norm_ffn_g, norm_out_g, w_mem_kv, a_w_in, a_gate_b, a_conv_w, a_conv_b, a_head_g, a_w_out, b_w_in, b_lb_logits, b_head_g, b_w_out, ffn_w_up, ffn_conv_w, ffn_conv_b, ffn_w_down):
    batch, seq, d = x.shape
    mem_len = mem.shape[1]
    depth = norm_mix_g.shape[0]
    t = batch * seq
    tm_proj = _pick(seq, 256)
    tm_out = _pick(seq, 512)
    tm_ffn = _pick(seq, 512)
    tb = _pick(seq, 256)

    h = x.reshape(t, d)
    mem2 = mem.reshape(batch * mem_len, d)
    for layer in range(depth):
        j = layer // N_MIXERS
        kv = _norm_matmul(mem2, _gain_folded(w_mem_kv[layer], norm_mem_g[layer]), _pick(mem_len, 256))
        if layer % N_MIXERS == 0:
            w_all, cw, cb, gb = _mlstm_weights(norm_mix_g[layer][:, None] * a_w_in[j], a_gate_b[j], a_conv_w[j],
                                               a_conv_b[j])
            zq, zk, zv, zo, zxq, zg = _proj_mlstm(h, w_all, cw, cb, seq, tm_proj)
            y_mix = _mlstm(zq, zk, zv, zo, zg, gb, a_head_g[j].reshape(1, MIX_WIDTH), batch, seq, tb,
                           A_HEADS_PER_STEP)
            w_out = a_w_out[j]
        else:
            zq, zlf, zkk, zv, zsg, zxq = _proj_hgrn2(h, _gain_folded(b_w_in[j], norm_mix_g[layer]), b_lb_logits,
                                                     tm_proj, layer)
            y_mix = _hgrn2(zq, zlf, zkk, zv, zsg, b_head_g[j].reshape(1, MIX_WIDTH), batch, seq, tb,
                           B_HEADS_PER_STEP)
            w_out = b_w_out[j]
        h = _outproj(y_mix, zxq, kv, w_out.astype(BF16), h, seq, mem_len, tm_out)
        h = _ffn(h, _gain_folded(ffn_w_up[layer], norm_ffn_g[layer]), ffn_conv_w[layer], ffn_conv_b[layer],
                 ffn_w_down[layer].astype(BF16), norm_out_g, seq, tm_ffn, 512, 512, layer == depth - 1)
    return h.reshape(batch, seq, d)
```

```python
import functools

import numpy as np
import jax
import jax.numpy as jnp
from jax import lax
from jax.experimental import pallas as pl
from jax.experimental.pallas import tpu as pltpu

F32 = jnp.float32
BF16 = jnp.bfloat16

D_MODEL = 2048
N_MIXERS = 2
CHUNK = 64
EPS = 1e-6

XA_HEADS = 4
XA_WIDTH = D_MODEL // 4
XA_HEAD_DIM = XA_WIDTH // XA_HEADS
MIX_WIDTH = D_MODEL - XA_WIDTH

A_HEADS = 4
A_V_DIM = MIX_WIDTH // A_HEADS
A_QK_DIM = A_V_DIM // 2
A_QK_PAD = 256
A_CONV = 4
A_AUG = 128
A_HEADS_PER_STEP = 4

B_HEADS = 12
B_K_DIM = 128
B_V_DIM = 128
B_LEVELS = 6
B_HEADS_PER_STEP = 12

D_FF = 5632
FFN_CONV = 3

LANES = 128
SUBLANES = 8
VMEM_LIMIT_BYTES = 56 * 1024 * 1024

_NN = (((1,), (0,)), ((), ()))
_NT = (((1,), (1,)), ((), ()))
_TN = (((0,), (0,)), ((), ()))


def _dot(a, b, dims=_NN):
    return lax.dot_general(a, b, dims, preferred_element_type=F32)


def _split2(x):
    hi = x.astype(BF16)
    return hi, (x - hi.astype(F32)).astype(BF16)


def _dot_exact_lhs(a_bf, x, dims=_NN):
    return sum(_dot(a_bf, p, dims) for p in _split2(x))


def _dot_exact_rhs(x, b_bf, dims=_NN):
    return sum(_dot(p, b_bf, dims) for p in _split2(x))


def _sigmoid(x):
    return 0.5 + 0.5 * jnp.tanh(0.5 * x)


def _silu(x):
    hx = 0.5 * x
    return hx + hx * jnp.tanh(hx)


def _log_sigmoid(x):
    return jnp.minimum(x, 0.0) - jnp.log(1.0 + jnp.exp(-jnp.abs(x)))


def _params(n_grid):
    return pltpu.CompilerParams(dimension_semantics=("arbitrary",) * n_grid,
                                vmem_limit_bytes=VMEM_LIMIT_BYTES)


def _resident(shape):
    nd = len(shape)
    return pl.BlockSpec(shape, lambda *_: (0,) * nd, pipeline_mode=pl.Buffered(1))


def _gain_folded(w, g):
    return (g[:, None] * w).astype(BF16)


def _rms_split(x_ref):
    x = x_ref[...]
    return x.astype(BF16), lax.rsqrt(jnp.mean(x * x, axis=-1, keepdims=True) + EPS)


def _norm_matmul_kernel(x_ref, w_ref, out_ref):
    x_bf, rs = _rms_split(x_ref)
    out_ref[...] = _dot(x_bf, w_ref[...]) * rs


def _norm_matmul(x, wg_bf, tm):
    t, d = x.shape
    n = wg_bf.shape[1]
    return pl.pallas_call(
        _norm_matmul_kernel,
        grid=(t // tm,),
        in_specs=[pl.BlockSpec((tm, d), lambda i: (i, 0)),
                  _resident((d, n))],
        out_specs=pl.BlockSpec((tm, n), lambda i: (i, 0)),
        out_shape=jax.ShapeDtypeStruct((t, n), F32),
        compiler_params=_params(1),
        name="norm_matmul",
    )(x, wg_bf)


def _proj_mlstm_kernel(x_ref, w_ref, cw_ref, cb_ref, q_out, k_out, v_out, o_out, xq_out, gt_out, cbuf,
                       *, tm, per_seq):
    i = pl.program_id(0)
    qk_w = 2 * A_HEADS * A_QK_PAD

    @pl.when(i % per_seq == 0)
    def _():
        cbuf[0:SUBLANES] = jnp.zeros((SUBLANES, qk_w), F32)

    x_bf, rs = _rms_split(x_ref)

    def proj(off, width):
        return _dot(x_bf, w_ref[:, off:off + width]) * rs

    cbuf[SUBLANES:SUBLANES + tm] = proj(0, qk_w)
    acc = cb_ref[...]
    for j in range(A_CONV):
        lo = SUBLANES - (A_CONV - 1) + j
        acc = acc + cw_ref[j:j + 1, :] * cbuf[lo:lo + tm, :]
    y = _silu(acc)
    q_out[...] = y[:, :qk_w // 2].astype(BF16)
    k_out[...] = (y[:, qk_w // 2:] * (A_QK_DIM ** -0.5)).astype(BF16)
    cbuf[0:SUBLANES] = cbuf[tm:tm + SUBLANES]
    off = qk_w
    v_out[...] = proj(off, MIX_WIDTH).astype(BF16)
    off += MIX_WIDTH
    o_out[...] = _sigmoid(proj(off, MIX_WIDTH))
    off += MIX_WIDTH
    xq_out[...] = proj(off, XA_WIDTH).astype(BF16)
    off += XA_WIDTH
    gt_out[...] = proj(off, LANES)


def _proj_mlstm(x, w_bf, conv_w, conv_b, seq, tm):
    t, d = x.shape
    n = w_bf.shape[1]
    qk_w = 2 * A_HEADS * A_QK_PAD
    widths = (qk_w // 2, qk_w // 2, MIX_WIDTH, MIX_WIDTH, XA_WIDTH, LANES)
    dtypes = (BF16, BF16, BF16, F32, BF16, F32)
    assert sum(widths) == n
    return pl.pallas_call(
        functools.partial(_proj_mlstm_kernel, tm=tm, per_seq=seq // tm),
        grid=(t // tm,),
        in_specs=[pl.BlockSpec((tm, d), lambda i: (i, 0)),
                  _resident((d, n)),
                  _resident((A_CONV, qk_w)),
                  _resident((1, qk_w))],
        out_specs=[pl.BlockSpec((tm, s), lambda i: (i, 0)) for s in widths],
        out_shape=[jax.ShapeDtypeStruct((t, s), dt) for s, dt in zip(widths, dtypes)],
        scratch_shapes=[pltpu.VMEM((tm + SUBLANES, qk_w), F32)],
        compiler_params=_params(1),
        name="proj_mlstm",
    )(x, w_bf, conv_w, conv_b)


def _proj_hgrn2_kernel(x_ref, w_ref, lbl_ref, q_out, lf_out, kk_out, v_out, sg_out, xq_out, *, layer):
    lg = lbl_ref[...]
    lg = lg - jnp.max(lg, axis=0, keepdims=True)
    pe = jnp.exp(lg)
    pr = pe / jnp.sum(pe, axis=0, keepdims=True)
    c0 = pr[0:1, :]
    cl = c0
    for r in range(1, layer + 1):
        cl = cl + pr[r:r + 1, :]
    lb = cl - c0
    om = 1.0 - lb

    x_bf, rs = _rms_split(x_ref)
    kw = B_HEADS * B_K_DIM

    def proj(off, width):
        return _dot(x_bf, w_ref[:, off:off + width]) * rs

    q_out[...] = _silu(proj(0, kw))
    th = 0.5 * jnp.tanh(0.5 * proj(kw, kw))
    lf_out[...] = jnp.log2(lb + om * (0.5 + th))
    kk_out[...] = om * (0.5 - th)
    off = 2 * kw
    v_out[...] = proj(off, MIX_WIDTH).astype(BF16)
    off += MIX_WIDTH
    sg_out[...] = _silu(proj(off, MIX_WIDTH))
    off += MIX_WIDTH
    xq_out[...] = proj(off, XA_WIDTH).astype(BF16)


def _proj_hgrn2(x, w_bf, lb_logits, tm, layer):
    t, d = x.shape
    n = w_bf.shape[1]
    kw = B_HEADS * B_K_DIM
    widths = (kw, kw, kw, MIX_WIDTH, MIX_WIDTH, XA_WIDTH)
    dtypes = (F32, F32, F32, BF16, F32, BF16)
    assert 2 * kw + 2 * MIX_WIDTH + XA_WIDTH == n
    return pl.pallas_call(
        functools.partial(_proj_hgrn2_kernel, layer=layer),
        grid=(t // tm,),
        in_specs=[pl.BlockSpec((tm, d), lambda i: (i, 0)),
                  _resident((d, n)),
                  _resident(lb_logits.shape)],
        out_specs=[pl.BlockSpec((tm, s), lambda i: (i, 0)) for s in widths],
        out_shape=[jax.ShapeDtypeStruct((t, s), dt) for s, dt in zip(widths, dtypes)],
        compiler_params=_params(1),
        name="proj_hgrn2",
    )(x, w_bf, lb_logits)


def _mlstm_kernel(q_ref, k_ref, v_ref, o_ref, gt_ref, gb_ref, hg_ref, tril_ref, out_ref, c_st, m_st, *, tb, hb):
    hgrp = pl.program_id(1)
    t = pl.program_id(2)
    aug_w = A_V_DIM + A_AUG
    rep = aug_w // LANES

    @pl.when(t == 0)
    def _():
        c_st[...] = jnp.zeros_like(c_st)
        m_st[...] = jnp.zeros_like(m_st)

    tril = tril_ref[...]
    row = lax.broadcasted_iota(jnp.int32, (CHUNK, CHUNK), 0)
    col = lax.broadcasted_iota(jnp.int32, (CHUNK, CHUNK), 1)
    causal = row >= col
    wide = hb * LANES
    srow = lax.broadcasted_iota(jnp.int32, (LANES, 2 * wide), 0)
    scol = lax.broadcasted_iota(jnp.int32, (LANES, 2 * wide), 1)
    blk = scol // LANES
    want = jnp.where(blk < hb, hgrp * hb + blk, A_HEADS + hgrp * hb + (blk - hb))
    sel = jnp.where(srow == want, 1.0, 0.0).astype(BF16)
    frow = lax.broadcasted_iota(jnp.int32, (CHUNK, LANES), 1)
    first = jnp.where(frow == 0, 1.0, 0.0).astype(BF16)
    ones_aug = jnp.ones((CHUNK, A_AUG), BF16)
    heads = range(hb)

    def lanes_of(hh):
        return slice(hh * LANES, (hh + 1) * LANES)

    def chunk(c, carry):
        rows = pl.ds(pl.multiple_of(c * CHUNK, CHUNK), CHUNK)
        gates = gt_ref[rows, :] + gb_ref[...]
        g2 = _dot_exact_rhs(gates, sel)
        ic = g2[:, :wide]
        lf = _log_sigmoid(g2[:, wide:])
        gc = _dot_exact_lhs(tril, lf)
        m_prev = m_st[0:1, :]
        a = gc + m_prev
        icg = ic - gc
        r_mat = [_dot_exact_lhs(first, icg[:, lanes_of(hh)], _NT) for hh in heads]
        dmat = [jnp.where(causal, gc[:, hh * LANES:hh * LANES + CHUNK] + r_mat[hh], -jnp.inf) for hh in heads]
        mx = jnp.concatenate([jnp.broadcast_to(jnp.max(dmat[hh], axis=1, keepdims=True), (CHUNK, LANES))
                              for hh in heads], axis=1)
        m_row = jnp.maximum(a, mx)
        w_inter = jnp.exp(a - m_row)
        e_neg = jnp.exp(-m_row)
        p = [jnp.exp(dmat[hh] - m_row[:, hh * LANES:hh * LANES + CHUNK]) for hh in heads]

        g_end = gc[CHUNK - 1:CHUNK, :]
        a_end = g_end + m_prev
        w_end = g_end - gc + ic
        m_new = jnp.maximum(a_end, jnp.max(w_end, axis=0, keepdims=True))
        decay = jnp.exp(a_end - m_new)
        ws = jnp.exp(w_end - m_new).astype(BF16)
        m_st[...] = jnp.broadcast_to(m_new, (SUBLANES, wide))

        qk_lanes = [slice(hh * A_QK_PAD, (hh + 1) * A_QK_PAD) for hh in heads]
        v_lanes = [slice(hh * A_V_DIM, (hh + 1) * A_V_DIM) for hh in heads]
        q = [q_ref[rows, qk_lanes[hh]] for hh in heads]
        k = [k_ref[rows, qk_lanes[hh]] for hh in heads]
        v_aug = [jnp.concatenate([v_ref[rows, v_lanes[hh]], ones_aug], axis=1) for hh in heads]
        qk = [_dot(q[hh], k[hh], _NT) for hh in heads]
        c_old = [c_st[hh] for hh in heads]
        inter = [_dot(q[hh], c_old[hh].astype(BF16)) for hh in heads]
        intra = [_dot((qk[hh] * p[hh]).astype(BF16), v_aug[hh]) for hh in heads]
        for hh in heads:
            kw = k[hh] * jnp.concatenate([ws[:, lanes_of(hh)]] * (A_QK_PAD // LANES), axis=1)
            c_st[hh] = (jnp.concatenate([decay[:, lanes_of(hh)]] * rep, axis=1) * c_old[hh]
                        + _dot(kw, v_aug[hh], _TN))
        for hh in heads:
            tot = jnp.concatenate([w_inter[:, lanes_of(hh)]] * rep, axis=1) * inter[hh] + intra[hh]
            den = tot[:, A_V_DIM:]
            inv = 1.0 / jnp.maximum(jnp.abs(den), e_neg[:, lanes_of(hh)])
            h_out = tot[:, :A_V_DIM] * jnp.concatenate([inv] * (A_V_DIM // LANES), axis=1)
            ms = jnp.mean(h_out * h_out, axis=-1, keepdims=True)
            hn = h_out * lax.rsqrt(ms + EPS) * hg_ref[:, v_lanes[hh]]
            out_ref[rows, v_lanes[hh]] = (o_ref[rows, v_lanes[hh]] * hn).astype(out_ref.dtype)
        return carry

    lax.fori_loop(0, tb // CHUNK, chunk, 0)


def _mlstm(zq, zk, zv, zo, zg, gate_b, head_g, batch, seq, tb, hb):
    t_total = batch * seq
    nt = seq // tb
    tril = jnp.asarray(np.tril(np.ones((CHUNK, CHUNK), np.float32)), BF16)
    row_map = lambda b, h, t: (b * nt + t, h)
    head_map = lambda b, h, t: (0, h)
    fixed = lambda b, h, t: (0, 0)
    return pl.pallas_call(
        functools.partial(_mlstm_kernel, tb=tb, hb=hb),
        grid=(batch, A_HEADS // hb, nt),
        in_specs=[pl.BlockSpec((tb, hb * A_QK_PAD), row_map),
                  pl.BlockSpec((tb, hb * A_QK_PAD), row_map),
                  pl.BlockSpec((tb, hb * A_V_DIM), row_map),
                  pl.BlockSpec((tb, hb * A_V_DIM), row_map),
                  pl.BlockSpec((tb, LANES), lambda b, h, t: (b * nt + t, 0)),
                  pl.BlockSpec((1, LANES), fixed),
                  pl.BlockSpec((1, hb * A_V_DIM), head_map),
                  pl.BlockSpec((CHUNK, CHUNK), fixed)],
        out_specs=pl.BlockSpec((tb, hb * A_V_DIM), row_map),
        out_shape=jax.ShapeDtypeStruct((t_total, MIX_WIDTH), BF16),
        scratch_shapes=[pltpu.VMEM((hb, A_QK_PAD, A_V_DIM + A_AUG), F32),
                        pltpu.VMEM((SUBLANES, hb * LANES), F32)],
        compiler_params=_params(3),
        name="mlstm",
    )(zq, zk, zv, zo, zg, gate_b, head_g, tril)


def _hgrn2_pair_masks():
    n = CHUNK
    masks = [np.eye(n, dtype=np.float32)]
    for j in range(1, B_LEVELS + 1):
        c = n >> j
        pm = np.zeros((n, n), np.float32)
        for r in range(n):
            if r % (2 * c) >= c:
                mid = (r // (2 * c)) * 2 * c + c
                pm[r, mid - c:mid] = 1.0
        masks.append(pm)
    return np.stack(masks, axis=0)


def _hgrn2_kernel(q_ref, lf_ref, kk_ref, v_ref, sg_ref, hg_ref, tril_ref, mask_ref, out_ref, s_st, *, tb, hb):
    t = pl.program_id(2)

    @pl.when(t == 0)
    def _():
        s_st[...] = jnp.zeros_like(s_st)

    tril = tril_ref[...]
    wide = hb * B_K_DIM
    heads = range(hb)
    rowi = lax.broadcasted_iota(jnp.int32, (CHUNK, wide), 0)
    sub = lax.broadcasted_iota(jnp.int32, (CHUNK // SUBLANES, SUBLANES, wide), 1)

    def level_operand(q, kk, gc, lf, half):
        if half >= SUBLANES:
            pieces = []
            for b0 in range(0, CHUNK, 2 * half):
                mid = b0 + half
                ref = gc[mid - 1:mid, :]
                pieces.append(kk[b0:mid, :] * jnp.exp2(ref - gc[b0:mid, :]))
                pieces.append(q[mid:mid + half, :] * jnp.exp2(gc[mid:mid + half, :] - ref))
            return jnp.concatenate(pieces, axis=0)
        if half == 1:
            odd = (rowi & 1) != 0
            return jnp.where(odd, q * jnp.exp2(lf), kk)
        g3 = gc.reshape(CHUNK // SUBLANES, SUBLANES, wide)
        ref = g3[:, half - 1:half, :]
        for b0 in range(2 * half, SUBLANES, 2 * half):
            ref = jnp.where(sub >= b0, g3[:, b0 + half - 1:b0 + half, :], ref)
        d = g3 - ref
        upper = (sub & half) != 0
        ex = jnp.exp2(jnp.where(upper, d, -d)).reshape(CHUNK, wide)
        return jnp.where((rowi & half) != 0, q, kk) * ex

    def chunk(c, carry):
        rows = pl.ds(pl.multiple_of(c * CHUNK, CHUNK), CHUNK)
        q = q_ref[rows, :]
        lf = lf_ref[rows, :]
        kk = kk_ref[rows, :]
        v = v_ref[rows, :]
        gc = _dot_exact_lhs(tril, lf)
        g_end = gc[CHUNK - 1:CHUNK, :]

        xs = [level_operand(q, kk, gc, lf, CHUNK >> j).astype(BF16) for j in range(1, B_LEVELS + 1)]
        q_bf = q.astype(BF16)
        k_bf = kk.astype(BF16)
        qg = (q * jnp.exp2(gc)).astype(BF16)
        kd = (kk * jnp.exp2(g_end - gc)).astype(BF16)
        dec = jnp.exp2(g_end)

        def hl(x, hh):
            return x[:, hh * B_K_DIM:(hh + 1) * B_K_DIM]

        prod = [[_dot(hl(q_bf, hh), hl(k_bf, hh), _NT)] + [_dot(hl(x, hh), hl(x, hh), _NT) for x in xs]
                for hh in heads]
        s_old = [s_st[hh] for hh in heads]
        o_inter = [_dot(hl(qg, hh), s_old[hh].astype(BF16), _NT) for hh in heads]
        for hh in heads:
            s_st[hh] = s_old[hh] * hl(dec, hh) + _dot(hl(v, hh), hl(kd, hh), _TN)
        att = []
        for hh in heads:
            acc = mask_ref[0] * prod[hh][0]
            for j in range(1, B_LEVELS + 1):
                acc = acc + mask_ref[j] * prod[hh][j]
            att.append(acc.astype(BF16))
        outs = []
        for hh in heads:
            o = _dot(att[hh], hl(v, hh)) + o_inter[hh]
            ms = jnp.mean(o * o, axis=-1, keepdims=True)
            outs.append(o * lax.rsqrt(ms + EPS))
        on = jnp.concatenate(outs, axis=1) * hg_ref[...]
        out_ref[rows, :] = (on * sg_ref[rows, :]).astype(out_ref.dtype)
        return carry

    lax.fori_loop(0, tb // CHUNK, chunk, 0)


def _hgrn2(zq, zlf, zkk, zv, zsg, head_g, batch, seq, tb, hb):
    t_total = batch * seq
    nt = seq // tb
    mask_np = _hgrn2_pair_masks()
    tril = jnp.asarray(np.tril(np.ones((CHUNK, CHUNK), np.float32)), BF16)
    row_map = lambda b, h, t: (b * nt + t, h)
    head_map = lambda b, h, t: (0, h)
    return pl.pallas_call(
        functools.partial(_hgrn2_kernel, tb=tb, hb=hb),
        grid=(batch, B_HEADS // hb, nt),
        in_specs=[pl.BlockSpec((tb, hb * B_K_DIM), row_map),
                  pl.BlockSpec((tb, hb * B_K_DIM), row_map),
                  pl.BlockSpec((tb, hb * B_K_DIM), row_map),
                  pl.BlockSpec((tb, hb * B_V_DIM), row_map),
                  pl.BlockSpec((tb, hb * B_V_DIM), row_map),
                  pl.BlockSpec((1, hb * B_V_DIM), head_map),
                  pl.BlockSpec((CHUNK, CHUNK), lambda b, h, t: (0, 0)),
                  pl.BlockSpec(mask_np.shape, lambda b, h, t: (0, 0, 0))],
        out_specs=pl.BlockSpec((tb, hb * B_V_DIM), row_map),
        out_shape=jax.ShapeDtypeStruct((t_total, MIX_WIDTH), BF16),
        scratch_shapes=[pltpu.VMEM((hb, B_V_DIM, B_K_DIM), F32)],
        compiler_params=_params(3),
        name="hgrn2",
    )(zq, zlf, zkk, zv, zsg, head_g, tril, jnp.asarray(mask_np, F32))


def _outproj_kernel(y_ref, xq_ref, kv_ref, wo_ref, h_ref, out_ref, hn_ref, rs_ref):
    kv = kv_ref[...]
    parts = []
    for hh in range(XA_HEADS):
        lo = hh * XA_HEAD_DIM
        qh = xq_ref[:, lo:lo + XA_HEAD_DIM]
        kh = kv[:, lo:lo + XA_HEAD_DIM].astype(BF16)
        vh = kv[:, XA_WIDTH + lo:XA_WIDTH + lo + XA_HEAD_DIM].astype(BF16)
        s = _dot(qh, kh, _NT) * (XA_HEAD_DIM ** -0.5)
        e = jnp.exp(s - jnp.max(s, axis=-1, keepdims=True))
        den = jnp.sum(e, axis=-1, keepdims=True)
        parts.append((_dot(e.astype(BF16), vh) / den).astype(BF16))
    y_mem = jnp.concatenate(parts, axis=1)
    h_new = (h_ref[...] + _dot(y_ref[...], wo_ref[0:MIX_WIDTH, :])
             + _dot(y_mem, wo_ref[MIX_WIDTH:D_MODEL, :]))
    out_ref[...] = h_new
    hn_ref[...] = h_new.astype(BF16)
    rs = lax.rsqrt(jnp.mean(h_new * h_new, axis=-1, keepdims=True) + EPS)
    rs_ref[...] = jnp.broadcast_to(rs, rs_ref.shape)


def _outproj(y_mix, zxq, kv, w_out_bf, h, seq, mem_len, tm):
    t = h.shape[0]
    per_seq = seq // tm
    return pl.pallas_call(
        _outproj_kernel,
        grid=(t // tm,),
        in_specs=[pl.BlockSpec((tm, MIX_WIDTH), lambda i: (i, 0)),
                  pl.BlockSpec((tm, XA_WIDTH), lambda i: (i, 0)),
                  pl.BlockSpec((mem_len, 2 * XA_WIDTH), lambda i: (i // per_seq, 0)),
                  _resident((D_MODEL, D_MODEL)),
                  pl.BlockSpec((tm, D_MODEL), lambda i: (i, 0))],
        out_specs=[pl.BlockSpec((tm, D_MODEL), lambda i: (i, 0)),
                   pl.BlockSpec((tm, D_MODEL), lambda i: (i, 0)),
                   pl.BlockSpec((tm, LANES), lambda i: (i, 0))],
        out_shape=[jax.ShapeDtypeStruct((t, D_MODEL), F32),
                   jax.ShapeDtypeStruct((t, D_MODEL), BF16),
                   jax.ShapeDtypeStruct((t, LANES), F32)],
        compiler_params=_params(1),
        name="outproj",
    )(y_mix, zxq, kv, w_out_bf, h)


def _ffn_kernel(hn_ref, rs_ref, h_ref, wu_ref, wg_ref, cw_ref, cb_ref, wd_ref, og_ref, out_ref,
                gbuf, gcarry, act_s, *y_scratch, tm, nf, nd, per_seq, final_norm):
    i = pl.program_id(0)
    j = pl.program_id(1)
    tf = act_s.shape[2]
    tn = wd_ref.shape[1]

    @pl.when(j < nf)
    def _():
        hn = hn_ref[...]
        rs = jnp.concatenate([rs_ref[...]] * (tf // LANES), axis=1)
        g = _dot(hn, wg_ref[...]) * rs
        prev = jnp.where(i % per_seq == 0, 0.0, gcarry[j])
        gbuf[0:SUBLANES] = prev
        gbuf[SUBLANES:SUBLANES + tm] = g
        gcarry[j] = g[tm - SUBLANES:tm, :]
        gc = cb_ref[...]
        for k in range(FFN_CONV):
            lo = SUBLANES - (FFN_CONV - 1) + k
            gc = gc + cw_ref[k:k + 1, :] * gbuf[lo:lo + tm, :]
        sg = _silu(gc) * rs
        u = _dot(hn, wu_ref[...])
        act_s[j] = (sg * u).astype(BF16)

    @pl.when(j >= nf)
    def _():
        act = jnp.concatenate([act_s[f] for f in range(nf)], axis=1)
        y = h_ref[...] + _dot(act, wd_ref[...])
        if final_norm:
            y_scratch[0][j - nf] = y
        else:
            out_ref[...] = y

    if final_norm:
        @pl.when(j == nf + nd - 1)
        def _():
            ys = [y_scratch[0][n] for n in range(nd)]
            ssq = ys[0] * ys[0]
            for n in range(1, nd):
                ssq = ssq + ys[n] * ys[n]
            scale = lax.rsqrt(jnp.sum(ssq, axis=-1, keepdims=True) * (1.0 / (nd * tn)) + EPS)
            for n in range(nd):
                out_ref[:, n * tn:(n + 1) * tn] = ys[n] * scale * og_ref[:, n * tn:(n + 1) * tn]


def _ffn(hn, rs, h, w_up_bf, conv_w, conv_b, w_down_bf, out_g, seq, tm, tf, tn, final_norm):
    t, d = h.shape
    nf = D_FF // tf
    nd = d // tn
    per_seq = seq // tm
    up = lambda j: jnp.minimum(j, nf - 1)
    down = lambda j: jnp.maximum(j - nf, 0)
    if final_norm:
        out_spec = pl.BlockSpec((tm, d), lambda i, j: (i, 0))
        y_scratch = [pltpu.VMEM((nd, tm, tn), F32)]
    else:
        out_spec = pl.BlockSpec((tm, tn), lambda i, j: (i, down(j)))
        y_scratch = []
    return pl.pallas_call(
        functools.partial(_ffn_kernel, tm=tm, nf=nf, nd=nd, per_seq=per_seq, final_norm=final_norm),
        grid=(t // tm, nf + nd),
        in_specs=[pl.BlockSpec((tm, d), lambda i, j: (i, 0)),
                  pl.BlockSpec((tm, LANES), lambda i, j: (i, 0)),
                  pl.BlockSpec((tm, tn), lambda i, j: (i, down(j))),
                  pl.BlockSpec((d, tf), lambda i, j: (0, up(j))),
                  pl.BlockSpec((d, tf), lambda i, j: (0, nf + up(j))),
                  pl.BlockSpec((FFN_CONV, tf), lambda i, j: (0, up(j))),
                  pl.BlockSpec((1, tf), lambda i, j: (0, up(j))),
                  pl.BlockSpec((D_FF, tn), lambda i, j: (0, down(j))),
                  pl.BlockSpec((1, d), lambda i, j: (0, 0))],
        out_specs=out_spec,
        out_shape=jax.ShapeDtypeStruct((t, d), F32),
        scratch_shapes=[pltpu.VMEM((tm + SUBLANES, tf), F32),
                        pltpu.VMEM((nf, SUBLANES, tf), F32),
                        pltpu.VMEM((nf, tm, tf), BF16)] + y_scratch,
        compiler_params=_params(2),
        name="ffn",
    )(hn, rs, h, w_up_bf, w_up_bf, conv_w, conv_b.reshape(1, D_FF), w_down_bf, out_g.reshape(1, d))


def _pad_heads(w, heads, dim, pad):
    lead = w.shape[:-1]
    w = w.reshape(lead + (heads, dim))
    w = jnp.pad(w, [(0, 0)] * len(lead) + [(0, 0), (0, pad - dim)])
    return w.reshape(lead + (heads * pad,))


def _mlstm_weights(w_in, gate_b, conv_w, conv_b):
    qk_w = A_HEADS * A_QK_DIM
    o0 = 2 * qk_w
    wq = _pad_heads(w_in[:, :qk_w], A_HEADS, A_QK_DIM, A_QK_PAD)
    wk = _pad_heads(w_in[:, qk_w:o0], A_HEADS, A_QK_DIM, A_QK_PAD)
    wv = w_in[:, o0:o0 + MIX_WIDTH]
    wo = w_in[:, o0 + MIX_WIDTH:o0 + 2 * MIX_WIDTH]
    wg = jnp.pad(w_in[:, o0 + 2 * MIX_WIDTH:o0 + 2 * MIX_WIDTH + 2 * A_HEADS], ((0, 0), (0, LANES - 2 * A_HEADS)))
    wx = w_in[:, o0 + 2 * MIX_WIDTH + 2 * A_HEADS:]
    w_all = jnp.concatenate([wq, wk, wv, wo, wx, wg], axis=1).astype(BF16)
    cw = jnp.concatenate([_pad_heads(conv_w[:, :qk_w], A_HEADS, A_QK_DIM, A_QK_PAD),
                          _pad_heads(conv_w[:, qk_w:], A_HEADS, A_QK_DIM, A_QK_PAD)], axis=1)
    cb = jnp.concatenate([_pad_heads(conv_b[None, :qk_w], A_HEADS, A_QK_DIM, A_QK_PAD),
                          _pad_heads(conv_b[None, qk_w:], A_HEADS, A_QK_DIM, A_QK_PAD)], axis=1)
    gb = jnp.pad(gate_b[None, :], ((0, 0), (0, LANES - 2 * A_HEADS)))
    return w_all, cw, cb, gb


def _pick(n, cap):
    b = min(n, cap)
    while n % b:
        b //= 2
    return b


def kernel(x, mem, norm_mix_g, norm_mem_g, norm_ffn_g, norm_out_g, w_mem_kv, a_w_in, a_gate_b, a_conv_w, a_conv_b, a_head_g, a_w_out, b_w_in, b_lb_logits, b_head_g, b_w_out, ffn_w_up, ffn_conv_w, ffn_conv_b, ffn_w_down):
    batch, seq, d = x.shape
    mem_len = mem.shape[1]
    depth = norm_mix_g.shape[0]
    t = batch * seq
    tm_proj = _pick(seq, 256)
    tm_out = _pick(seq, 512)
    tm_ffn = _pick(seq, 1024)
    tm_ffn_last = _pick(seq, 512)
    tb = _pick(seq, 256)

    h = x.reshape(t, d)
    mem2 = mem.reshape(batch * mem_len, d)
    for layer in range(depth):
        j = layer // N_MIXERS
        kv = _norm_matmul(mem2, _gain_folded(w_mem_kv[layer], norm_mem_g[layer]), _pick(mem_len, 256))
        if layer % N_MIXERS == 0:
            w_all, cw, cb, gb = _mlstm_weights(norm_mix_g[layer][:, None] * a_w_in[j], a_gate_b[j], a_conv_w[j],
                                               a_conv_b[j])
            zq, zk, zv, zo, zxq, zg = _proj_mlstm(h, w_all, cw, cb, seq, tm_proj)
            y_mix = _mlstm(zq, zk, zv, zo, zg, gb, a_head_g[j].reshape(1, MIX_WIDTH), batch, seq, tb,
                           A_HEADS_PER_STEP)
            w_out = a_w_out[j]
        else:
            zq, zlf, zkk, zv, zsg, zxq = _proj_hgrn2(h, _gain_folded(b_w_in[j], norm_mix_g[layer]), b_lb_logits,
                                                     tm_proj, layer)
            y_mix = _hgrn2(zq, zlf, zkk, zv, zsg, b_head_g[j].reshape(1, MIX_WIDTH), batch, seq, tb,
                           B_HEADS_PER_STEP)
            w_out = b_w_out[j]
        h, hn, rs = _outproj(y_mix, zxq, kv, w_out.astype(BF16), h, seq, mem_len, tm_out)
        last = layer == depth - 1
        h = _ffn(hn, rs, h, _gain_folded(ffn_w_up[layer], norm_ffn_g[layer]), ffn_conv_w[layer], ffn_conv_b[layer],
                 ffn_w_down[layer].astype(BF16), norm_out_g, seq, tm_ffn_last if last else tm_ffn, 512, 512, last)
    return h.reshape(batch, seq, d)
```

```python
import functools

import numpy as np
import jax
import jax.numpy as jnp
from jax import lax
from jax.experimental import pallas as pl
from jax.experimental.pallas import tpu as pltpu

F32 = jnp.float32
BF16 = jnp.bfloat16

D_MODEL = 2048
N_MIXERS = 2
CHUNK = 64
EPS = 1e-6

XA_HEADS = 4
XA_WIDTH = D_MODEL // 4
XA_HEAD_DIM = XA_WIDTH // XA_HEADS
MIX_WIDTH = D_MODEL - XA_WIDTH

A_HEADS = 4
A_V_DIM = MIX_WIDTH // A_HEADS
A_QK_DIM = A_V_DIM // 2
A_QK_PAD = 256
A_CONV = 4
A_AUG = 128
A_HEADS_PER_STEP = 4
A_CHUNK = 128
B_HEADS = 12
B_K_DIM = 128
B_V_DIM = 128
B_LEVELS = 6
B_HEADS_PER_STEP = 12

D_FF = 5632
FFN_CONV = 3

LANES = 128
SUBLANES = 8
VMEM_LIMIT_BYTES = 56 * 1024 * 1024

_NN = (((1,), (0,)), ((), ()))
_NT = (((1,), (1,)), ((), ()))
_TN = (((0,), (0,)), ((), ()))


def _dot(a, b, dims=_NN):
    return lax.dot_general(a, b, dims, preferred_element_type=F32)


def _split2(x):
    hi = x.astype(BF16)
    return hi, (x - hi.astype(F32)).astype(BF16)


def _dot_exact_lhs(a_bf, x, dims=_NN):
    return sum(_dot(a_bf, p, dims) for p in _split2(x))


def _dot_exact_rhs(x, b_bf, dims=_NN):
    return sum(_dot(p, b_bf, dims) for p in _split2(x))


def _sigmoid(x):
    return 0.5 + 0.5 * jnp.tanh(0.5 * x)


def _silu(x):
    hx = 0.5 * x
    return hx + hx * jnp.tanh(hx)


def _log_sigmoid(x):
    return jnp.minimum(x, 0.0) - jnp.log(1.0 + jnp.exp(-jnp.abs(x)))


def _params(n_grid):
    return pltpu.CompilerParams(dimension_semantics=("arbitrary",) * n_grid,
                                vmem_limit_bytes=VMEM_LIMIT_BYTES)


def _resident(shape):
    nd = len(shape)
    return pl.BlockSpec(shape, lambda *_: (0,) * nd, pipeline_mode=pl.Buffered(1))


def _gain_folded(w, g):
    return (g[:, None] * w).astype(BF16)


def _rms_split(x_ref):
    x = x_ref[...]
    return x.astype(BF16), lax.rsqrt(jnp.mean(x * x, axis=-1, keepdims=True) + EPS)


def _norm_matmul_kernel(x_ref, w_ref, out_ref):
    x_bf, rs = _rms_split(x_ref)
    out_ref[...] = _dot(x_bf, w_ref[...]) * rs


def _norm_matmul(x, wg_bf, tm):
    t, d = x.shape
    n = wg_bf.shape[1]
    return pl.pallas_call(
        _norm_matmul_kernel,
        grid=(t // tm,),
        in_specs=[pl.BlockSpec((tm, d), lambda i: (i, 0)),
                  _resident((d, n))],
        out_specs=pl.BlockSpec((tm, n), lambda i: (i, 0)),
        out_shape=jax.ShapeDtypeStruct((t, n), F32),
        compiler_params=_params(1),
        name="norm_matmul",
    )(x, wg_bf)


def _proj_mlstm_kernel(x_ref, w_ref, cw_ref, cb_ref, q_out, k_out, v_out, o_out, xq_out, gt_out, cbuf,
                       *, tm, per_seq):
    i = pl.program_id(0)
    qk_w = 2 * A_HEADS * A_QK_PAD

    @pl.when(i % per_seq == 0)
    def _():
        cbuf[0:SUBLANES] = jnp.zeros((SUBLANES, qk_w), F32)

    x_bf, rs = _rms_split(x_ref)

    def proj(off, width):
        return _dot(x_bf, w_ref[:, off:off + width]) * rs

    cbuf[SUBLANES:SUBLANES + tm] = proj(0, qk_w)
    acc = cb_ref[...]
    for j in range(A_CONV):
        lo = SUBLANES - (A_CONV - 1) + j
        acc = acc + cw_ref[j:j + 1, :] * cbuf[lo:lo + tm, :]
    y = _silu(acc)
    q_out[...] = y[:, :qk_w // 2].astype(BF16)
    k_out[...] = (y[:, qk_w // 2:] * (A_QK_DIM ** -0.5)).astype(BF16)
    cbuf[0:SUBLANES] = cbuf[tm:tm + SUBLANES]
    off = qk_w
    v_out[...] = proj(off, MIX_WIDTH).astype(BF16)
    off += MIX_WIDTH
    o_out[...] = _sigmoid(proj(off, MIX_WIDTH))
    off += MIX_WIDTH
    xq_out[...] = proj(off, XA_WIDTH).astype(BF16)
    off += XA_WIDTH
    gt_out[...] = proj(off, LANES)


def _proj_mlstm(x, w_bf, conv_w, conv_b, seq, tm):
    t, d = x.shape
    n = w_bf.shape[1]
    qk_w = 2 * A_HEADS * A_QK_PAD
    widths = (qk_w // 2, qk_w // 2, MIX_WIDTH, MIX_WIDTH, XA_WIDTH, LANES)
    dtypes = (BF16, BF16, BF16, F32, BF16, F32)
    assert sum(widths) == n
    return pl.pallas_call(
        functools.partial(_proj_mlstm_kernel, tm=tm, per_seq=seq // tm),
        grid=(t // tm,),
        in_specs=[pl.BlockSpec((tm, d), lambda i: (i, 0)),
                  _resident((d, n)),
                  _resident((A_CONV, qk_w)),
                  _resident((1, qk_w))],
        out_specs=[pl.BlockSpec((tm, s), lambda i: (i, 0)) for s in widths],
        out_shape=[jax.ShapeDtypeStruct((t, s), dt) for s, dt in zip(widths, dtypes)],
        scratch_shapes=[pltpu.VMEM((tm + SUBLANES, qk_w), F32)],
        compiler_params=_params(1),
        name="proj_mlstm",
    )(x, w_bf, conv_w, conv_b)


def _proj_hgrn2_kernel(x_ref, w_ref, lbl_ref, q_out, lf_out, kk_out, v_out, sg_out, xq_out, *, layer):
    lg = lbl_ref[...]
    lg = lg - jnp.max(lg, axis=0, keepdims=True)
    pe = jnp.exp(lg)
    pr = pe / jnp.sum(pe, axis=0, keepdims=True)
    c0 = pr[0:1, :]
    cl = c0
    for r in range(1, layer + 1):
        cl = cl + pr[r:r + 1, :]
    lb = cl - c0
    om = 1.0 - lb

    x_bf, rs = _rms_split(x_ref)
    kw = B_HEADS * B_K_DIM

    def proj(off, width):
        return _dot(x_bf, w_ref[:, off:off + width]) * rs

    q_out[...] = _silu(proj(0, kw))
    th = 0.5 * jnp.tanh(0.5 * proj(kw, kw))
    lf_out[...] = jnp.log2(lb + om * (0.5 + th))
    kk_out[...] = om * (0.5 - th)
    off = 2 * kw
    v_out[...] = proj(off, MIX_WIDTH).astype(BF16)
    off += MIX_WIDTH
    sg_out[...] = _silu(proj(off, MIX_WIDTH))
    off += MIX_WIDTH
    xq_out[...] = proj(off, XA_WIDTH).astype(BF16)


def _proj_hgrn2(x, w_bf, lb_logits, tm, layer):
    t, d = x.shape
    n = w_bf.shape[1]
    kw = B_HEADS * B_K_DIM
    widths = (kw, kw, kw, MIX_WIDTH, MIX_WIDTH, XA_WIDTH)
    dtypes = (F32, F32, F32, BF16, F32, BF16)
    assert 2 * kw + 2 * MIX_WIDTH + XA_WIDTH == n
    return pl.pallas_call(
        functools.partial(_proj_hgrn2_kernel, layer=layer),
        grid=(t // tm,),
        in_specs=[pl.BlockSpec((tm, d), lambda i: (i, 0)),
                  _resident((d, n)),
                  _resident(lb_logits.shape)],
        out_specs=[pl.BlockSpec((tm, s), lambda i: (i, 0)) for s in widths],
        out_shape=[jax.ShapeDtypeStruct((t, s), dt) for s, dt in zip(widths, dtypes)],
        compiler_params=_params(1),
        name="proj_hgrn2",
    )(x, w_bf, lb_logits)


def _mlstm_kernel(q_ref, k_ref, v_ref, o_ref, gt_ref, gb_ref, hg_ref, tril_ref, out_ref, c_st, m_st, *, tb, hb):
    hgrp = pl.program_id(1)
    t = pl.program_id(2)
    aug_w = A_V_DIM + A_AUG
    rep = aug_w // LANES

    @pl.when(t == 0)
    def _():
        c_st[...] = jnp.zeros_like(c_st)
        m_st[...] = jnp.zeros_like(m_st)

    tril = tril_ref[...]
    row = lax.broadcasted_iota(jnp.int32, (A_CHUNK, A_CHUNK), 0)
    col = lax.broadcasted_iota(jnp.int32, (A_CHUNK, A_CHUNK), 1)
    causal = row >= col
    wide = hb * LANES
    srow = lax.broadcasted_iota(jnp.int32, (LANES, 2 * wide), 0)
    scol = lax.broadcasted_iota(jnp.int32, (LANES, 2 * wide), 1)
    blk = scol // LANES
    want = jnp.where(blk < hb, hgrp * hb + blk, A_HEADS + hgrp * hb + (blk - hb))
    sel = jnp.where(srow == want, 1.0, 0.0).astype(BF16)
    frow = lax.broadcasted_iota(jnp.int32, (A_CHUNK, LANES), 1)
    first = jnp.where(frow == 0, 1.0, 0.0).astype(BF16)
    ones_aug = jnp.ones((A_CHUNK, A_AUG), BF16)
    heads = range(hb)

    def lanes_of(hh):
        return slice(hh * LANES, (hh + 1) * LANES)

    def chunk(c, carry):
        rows = pl.ds(pl.multiple_of(c * A_CHUNK, A_CHUNK), A_CHUNK)
        gates = gt_ref[rows, :] + gb_ref[...]
        g2 = _dot_exact_rhs(gates, sel)
        ic = g2[:, :wide]
        lf = _log_sigmoid(g2[:, wide:])
        gc = _dot_exact_lhs(tril, lf)
        m_prev = m_st[0:1, :]
        a = gc + m_prev
        icg = ic - gc
        r_mat = [_dot_exact_lhs(first, icg[:, lanes_of(hh)], _NT) for hh in heads]
        dmat = [jnp.where(causal, gc[:, hh * LANES:hh * LANES + A_CHUNK] + r_mat[hh], -jnp.inf) for hh in heads]
        mx = jnp.concatenate([jnp.broadcast_to(jnp.max(dmat[hh], axis=1, keepdims=True), (A_CHUNK, LANES))
                              for hh in heads], axis=1)
        m_row = jnp.maximum(a, mx)
        w_inter = jnp.exp(a - m_row)
        e_neg = jnp.exp(-m_row)
        p = [jnp.exp(dmat[hh] - m_row[:, hh * LANES:hh * LANES + A_CHUNK]) for hh in heads]

        g_end = gc[A_CHUNK - 1:A_CHUNK, :]
        a_end = g_end + m_prev
        w_end = g_end - gc + ic
        m_new = jnp.maximum(a_end, jnp.max(w_end, axis=0, keepdims=True))
        decay = jnp.exp(a_end - m_new)
        ws = jnp.exp(w_end - m_new).astype(BF16)
        m_st[...] = jnp.broadcast_to(m_new, (SUBLANES, wide))

        qk_lanes = [slice(hh * A_QK_PAD, (hh + 1) * A_QK_PAD) for hh in heads]
        v_lanes = [slice(hh * A_V_DIM, (hh + 1) * A_V_DIM) for hh in heads]
        q = [q_ref[rows, qk_lanes[hh]] for hh in heads]
        k = [k_ref[rows, qk_lanes[hh]] for hh in heads]
        v_aug = [jnp.concatenate([v_ref[rows, v_lanes[hh]], ones_aug], axis=1) for hh in heads]
        qk = [_dot(q[hh], k[hh], _NT) for hh in heads]
        c_old = [c_st[hh] for hh in heads]
        inter = [_dot(q[hh], c_old[hh].astype(BF16)) for hh in heads]
        intra = [_dot((qk[hh] * p[hh]).astype(BF16), v_aug[hh]) for hh in heads]
        for hh in heads:
            kw = k[hh] * jnp.concatenate([ws[:, lanes_of(hh)]] * (A_QK_PAD // LANES), axis=1)
            c_st[hh] = (jnp.concatenate([decay[:, lanes_of(hh)]] * rep, axis=1) * c_old[hh]
                        + _dot(kw, v_aug[hh], _TN))
        for hh in heads:
            tot = jnp.concatenate([w_inter[:, lanes_of(hh)]] * rep, axis=1) * inter[hh] + intra[hh]
            den = tot[:, A_V_DIM:]
            inv = 1.0 / jnp.maximum(jnp.abs(den), e_neg[:, lanes_of(hh)])
            h_out = tot[:, :A_V_DIM] * jnp.concatenate([inv] * (A_V_DIM // LANES), axis=1)
            ms = jnp.mean(h_out * h_out, axis=-1, keepdims=True)
            hn = h_out * lax.rsqrt(ms + EPS) * hg_ref[:, v_lanes[hh]]
            out_ref[rows, v_lanes[hh]] = (o_ref[rows, v_lanes[hh]] * hn).astype(out_ref.dtype)
        return carry

    lax.fori_loop(0, tb // A_CHUNK, chunk, 0)


def _mlstm(zq, zk, zv, zo, zg, gate_b, head_g, batch, seq, tb, hb):
    t_total = batch * seq
    nt = seq // tb
    tril = jnp.asarray(np.tril(np.ones((A_CHUNK, A_CHUNK), np.float32)), BF16)
    row_map = lambda b, h, t: (b * nt + t, h)
    head_map = lambda b, h, t: (0, h)
    fixed = lambda b, h, t: (0, 0)
    return pl.pallas_call(
        functools.partial(_mlstm_kernel, tb=tb, hb=hb),
        grid=(batch, A_HEADS // hb, nt),
        in_specs=[pl.BlockSpec((tb, hb * A_QK_PAD), row_map),
                  pl.BlockSpec((tb, hb * A_QK_PAD), row_map),
                  pl.BlockSpec((tb, hb * A_V_DIM), row_map),
                  pl.BlockSpec((tb, hb * A_V_DIM), row_map),
                  pl.BlockSpec((tb, LANES), lambda b, h, t: (b * nt + t, 0)),
                  pl.BlockSpec((1, LANES), fixed),
                  pl.BlockSpec((1, hb * A_V_DIM), head_map),
                  pl.BlockSpec((A_CHUNK, A_CHUNK), fixed)],
        out_specs=pl.BlockSpec((tb, hb * A_V_DIM), row_map),
        out_shape=jax.ShapeDtypeStruct((t_total, MIX_WIDTH), BF16),
        scratch_shapes=[pltpu.VMEM((hb, A_QK_PAD, A_V_DIM + A_AUG), F32),
                        pltpu.VMEM((SUBLANES, hb * LANES), F32)],
        compiler_params=_params(3),
        name="mlstm",
    )(zq, zk, zv, zo, zg, gate_b, head_g, tril)


def _hgrn2_pair_masks():
    n = CHUNK
    masks = [np.eye(n, dtype=np.float32)]
    for j in range(1, B_LEVELS + 1):
        c = n >> j
        pm = np.zeros((n, n), np.float32)
        for r in range(n):
            if r % (2 * c) >= c:
                mid = (r // (2 * c)) * 2 * c + c
                pm[r, mid - c:mid] = 1.0
        masks.append(pm)
    return np.stack(masks, axis=0)


def _hgrn2_kernel(q_ref, lf_ref, kk_ref, v_ref, sg_ref, hg_ref, tril_ref, mask_ref, out_ref, s_st, *, tb, hb):
    t = pl.program_id(2)

    @pl.when(t == 0)
    def _():
        s_st[...] = jnp.zeros_like(s_st)

    tril = tril_ref[...]
    wide = hb * B_K_DIM
    heads = range(hb)
    rowi = lax.broadcasted_iota(jnp.int32, (CHUNK, wide), 0)
    sub = lax.broadcasted_iota(jnp.int32, (CHUNK // SUBLANES, SUBLANES, wide), 1)

    def level_operand(q, kk, gc, lf, half):
        if half >= SUBLANES:
            pieces = []
            for b0 in range(0, CHUNK, 2 * half):
                mid = b0 + half
                ref = gc[mid - 1:mid, :]
                pieces.append(kk[b0:mid, :] * jnp.exp2(ref - gc[b0:mid, :]))
                pieces.append(q[mid:mid + half, :] * jnp.exp2(gc[mid:mid + half, :] - ref))
            return jnp.concatenate(pieces, axis=0)
        if half == 1:
            odd = (rowi & 1) != 0
            return jnp.where(odd, q * jnp.exp2(lf), kk)
        g3 = gc.reshape(CHUNK // SUBLANES, SUBLANES, wide)
        ref = g3[:, half - 1:half, :]
        for b0 in range(2 * half, SUBLANES, 2 * half):
            ref = jnp.where(sub >= b0, g3[:, b0 + half - 1:b0 + half, :], ref)
        d = g3 - ref
        upper = (sub & half) != 0
        ex = jnp.exp2(jnp.where(upper, d, -d)).reshape(CHUNK, wide)
        return jnp.where((rowi & half) != 0, q, kk) * ex

    def chunk(c, carry):
        rows = pl.ds(pl.multiple_of(c * CHUNK, CHUNK), CHUNK)
        q = q_ref[rows, :]
        lf = lf_ref[rows, :]
        kk = kk_ref[rows, :]
        v = v_ref[rows, :]
        gc = _dot_exact_lhs(tril, lf)
        g_end = gc[CHUNK - 1:CHUNK, :]

        xs = [level_operand(q, kk, gc, lf, CHUNK >> j).astype(BF16) for j in range(1, B_LEVELS + 1)]
        q_bf = q.astype(BF16)
        k_bf = kk.astype(BF16)
        qg = (q * jnp.exp2(gc)).astype(BF16)
        kd = (kk * jnp.exp2(g_end - gc)).astype(BF16)
        dec = jnp.exp2(g_end)

        def hl(x, hh):
            return x[:, hh * B_K_DIM:(hh + 1) * B_K_DIM]

        prod = [[_dot(hl(q_bf, hh), hl(k_bf, hh), _NT)] + [_dot(hl(x, hh), hl(x, hh), _NT) for x in xs]
                for hh in heads]
        s_old = [s_st[hh] for hh in heads]
        o_inter = [_dot(hl(qg, hh), s_old[hh].astype(BF16), _NT) for hh in heads]
        for hh in heads:
            s_st[hh] = s_old[hh] * hl(dec, hh) + _dot(hl(v, hh), hl(kd, hh), _TN)
        att = []
        for hh in heads:
            acc = mask_ref[0] * prod[hh][0]
            for j in range(1, B_LEVELS + 1):
                acc = acc + mask_ref[j] * prod[hh][j]
            att.append(acc.astype(BF16))
        outs = []
        for hh in heads:
            o = _dot(att[hh], hl(v, hh)) + o_inter[hh]
            ms = jnp.mean(o * o, axis=-1, keepdims=True)
            outs.append(o * lax.rsqrt(ms + EPS))
        on = jnp.concatenate(outs, axis=1) * hg_ref[...]
        out_ref[rows, :] = (on * sg_ref[rows, :]).astype(out_ref.dtype)
        return carry

    lax.fori_loop(0, tb // CHUNK, chunk, 0)


def _hgrn2(zq, zlf, zkk, zv, zsg, head_g, batch, seq, tb, hb):
    t_total = batch * seq
    nt = seq // tb
    mask_np = _hgrn2_pair_masks()
    tril = jnp.asarray(np.tril(np.ones((CHUNK, CHUNK), np.float32)), BF16)
    row_map = lambda b, h, t: (b * nt + t, h)
    head_map = lambda b, h, t: (0, h)
    return pl.pallas_call(
        functools.partial(_hgrn2_kernel, tb=tb, hb=hb),
        grid=(batch, B_HEADS // hb, nt),
        in_specs=[pl.BlockSpec((tb, hb * B_K_DIM), row_map),
                  pl.BlockSpec((tb, hb * B_K_DIM), row_map),
                  pl.BlockSpec((tb, hb * B_K_DIM), row_map),
                  pl.BlockSpec((tb, hb * B_V_DIM), row_map),
                  pl.BlockSpec((tb, hb * B_V_DIM), row_map),
                  pl.BlockSpec((1, hb * B_V_DIM), head_map),
                  pl.BlockSpec((CHUNK, CHUNK), lambda b, h, t: (0, 0)),
                  pl.BlockSpec(mask_np.shape, lambda b, h, t: (0, 0, 0))],
        out_specs=pl.BlockSpec((tb, hb * B_V_DIM), row_map),
        out_shape=jax.ShapeDtypeStruct((t_total, MIX_WIDTH), BF16),
        scratch_shapes=[pltpu.VMEM((hb, B_V_DIM, B_K_DIM), F32)],
        compiler_params=_params(3),
        name="hgrn2",
    )(zq, zlf, zkk, zv, zsg, head_g, tril, jnp.asarray(mask_np, F32))


def _outproj_kernel(y_ref, xq_ref, kv_ref, wo_ref, h_ref, out_ref, hn_ref, rs_ref):
    kv = kv_ref[...]
    parts = []
    for hh in range(XA_HEADS):
        lo = hh * XA_HEAD_DIM
        qh = xq_ref[:, lo:lo + XA_HEAD_DIM]
        kh = kv[:, lo:lo + XA_HEAD_DIM].astype(BF16)
        vh = kv[:, XA_WIDTH + lo:XA_WIDTH + lo + XA_HEAD_DIM].astype(BF16)
        s = _dot(qh, kh, _NT) * (XA_HEAD_DIM ** -0.5)
        e = jnp.exp(s - jnp.max(s, axis=-1, keepdims=True))
        den = jnp.sum(e, axis=-1, keepdims=True)
        parts.append((_dot(e.astype(BF16), vh) / den).astype(BF16))
    y_mem = jnp.concatenate(parts, axis=1)
    h_new = (h_ref[...] + _dot(y_ref[...], wo_ref[0:MIX_WIDTH, :])
             + _dot(y_mem, wo_ref[MIX_WIDTH:D_MODEL, :]))
    out_ref[...] = h_new
    hn_ref[...] = h_new.astype(BF16)
    rs = lax.rsqrt(jnp.mean(h_new * h_new, axis=-1, keepdims=True) + EPS)
    rs_ref[...] = jnp.broadcast_to(rs, rs_ref.shape)


def _outproj(y_mix, zxq, kv, w_out_bf, h, seq, mem_len, tm):
    t = h.shape[0]
    per_seq = seq // tm
    return pl.pallas_call(
        _outproj_kernel,
        grid=(t // tm,),
        in_specs=[pl.BlockSpec((tm, MIX_WIDTH), lambda i: (i, 0)),
                  pl.BlockSpec((tm, XA_WIDTH), lambda i: (i, 0)),
                  pl.BlockSpec((mem_len, 2 * XA_WIDTH), lambda i: (i // per_seq, 0)),
                  _resident((D_MODEL, D_MODEL)),
                  pl.BlockSpec((tm, D_MODEL), lambda i: (i, 0))],
        out_specs=[pl.BlockSpec((tm, D_MODEL), lambda i: (i, 0)),
                   pl.BlockSpec((tm, D_MODEL), lambda i: (i, 0)),
                   pl.BlockSpec((tm, LANES), lambda i: (i, 0))],
        out_shape=[jax.ShapeDtypeStruct((t, D_MODEL), F32),
                   jax.ShapeDtypeStruct((t, D_MODEL), BF16),
                   jax.ShapeDtypeStruct((t, LANES), F32)],
        compiler_params=_params(1),
        name="outproj",
    )(y_mix, zxq, kv, w_out_bf, h)


def _ffn_kernel(hn_ref, rs_ref, h_ref, wu_ref, wg_ref, cw_ref, cb_ref, wd_ref, out_ref,
                gbuf, gcarry, act_s, *, tm, nf, per_seq):
    i = pl.program_id(0)
    j = pl.program_id(1)
    tf = act_s.shape[2]

    @pl.when(j < nf)
    def _():
        hn = hn_ref[...]
        rs = jnp.concatenate([rs_ref[...]] * (tf // LANES), axis=1)
        g = _dot(hn, wg_ref[...]) * rs
        prev = jnp.where(i % per_seq == 0, 0.0, gcarry[j])
        gbuf[0:SUBLANES] = prev
        gbuf[SUBLANES:SUBLANES + tm] = g
        gcarry[j] = g[tm - SUBLANES:tm, :]
        gc = cb_ref[...]
        for k in range(FFN_CONV):
            lo = SUBLANES - (FFN_CONV - 1) + k
            gc = gc + cw_ref[k:k + 1, :] * gbuf[lo:lo + tm, :]
        sg = _silu(gc) * rs
        u = _dot(hn, wu_ref[...])
        act_s[j] = (sg * u).astype(BF16)

    @pl.when(j >= nf)
    def _():
        act = jnp.concatenate([act_s[f] for f in range(nf)], axis=1)
        out_ref[...] = h_ref[...] + _dot(act, wd_ref[...])


def _ffn(hn, rs, h, w_up_bf, conv_w, conv_b, w_down_bf, seq, tm, tf, tn):
    t, d = h.shape
    nf = D_FF // tf
    nd = d // tn
    per_seq = seq // tm
    up = lambda j: jnp.minimum(j, nf - 1)
    down = lambda j: jnp.maximum(j - nf, 0)
    return pl.pallas_call(
        functools.partial(_ffn_kernel, tm=tm, nf=nf, per_seq=per_seq),
        grid=(t // tm, nf + nd),
        in_specs=[pl.BlockSpec((tm, d), lambda i, j: (i, 0)),
                  pl.BlockSpec((tm, LANES), lambda i, j: (i, 0)),
                  pl.BlockSpec((tm, tn), lambda i, j: (i, down(j))),
                  pl.BlockSpec((d, tf), lambda i, j: (0, up(j))),
                  pl.BlockSpec((d, tf), lambda i, j: (0, nf + up(j))),
                  pl.BlockSpec((FFN_CONV, tf), lambda i, j: (0, up(j))),
                  pl.BlockSpec((1, tf), lambda i, j: (0, up(j))),
                  pl.BlockSpec((D_FF, tn), lambda i, j: (0, down(j)))],
        out_specs=pl.BlockSpec((tm, tn), lambda i, j: (i, down(j))),
        out_shape=jax.ShapeDtypeStruct((t, d), F32),
        scratch_shapes=[pltpu.VMEM((tm + SUBLANES, tf), F32),
                        pltpu.VMEM((nf, SUBLANES, tf), F32),
                        pltpu.VMEM((nf, tm, tf), BF16)],
        compiler_params=_params(2),
        name="ffn",
    )(hn, rs, h, w_up_bf, w_up_bf, conv_w, conv_b.reshape(1, D_FF), w_down_bf)


def _final_norm_kernel(x_ref, g_ref, out_ref):
    x = x_ref[...]
    out_ref[...] = x * lax.rsqrt(jnp.mean(x * x, axis=-1, keepdims=True) + EPS) * g_ref[...]


def _final_norm(x, g, tm):
    t, d = x.shape
    return pl.pallas_call(
        _final_norm_kernel,
        grid=(t // tm,),
        in_specs=[pl.BlockSpec((tm, d), lambda i: (i, 0)),
                  pl.BlockSpec((1, d), lambda i: (0, 0))],
        out_specs=pl.BlockSpec((tm, d), lambda i: (i, 0)),
        out_shape=jax.ShapeDtypeStruct((t, d), F32),
        compiler_params=_params(1),
        name="final_norm",
    )(x, g.reshape(1, d))


def _pad_heads(w, heads, dim, pad):
    lead = w.shape[:-1]
    w = w.reshape(lead + (heads, dim))
    w = jnp.pad(w, [(0, 0)] * len(lead) + [(0, 0), (0, pad - dim)])
    return w.reshape(lead + (heads * pad,))


def _mlstm_weights(w_in, gate_b, conv_w, conv_b):
    qk_w = A_HEADS * A_QK_DIM
    o0 = 2 * qk_w
    wq = _pad_heads(w_in[:, :qk_w], A_HEADS, A_QK_DIM, A_QK_PAD)
    wk = _pad_heads(w_in[:, qk_w:o0], A_HEADS, A_QK_DIM, A_QK_PAD)
    wv = w_in[:, o0:o0 + MIX_WIDTH]
    wo = w_in[:, o0 + MIX_WIDTH:o0 + 2 * MIX_WIDTH]
    wg = jnp.pad(w_in[:, o0 + 2 * MIX_WIDTH:o0 + 2 * MIX_WIDTH + 2 * A_HEADS], ((0, 0), (0, LANES - 2 * A_HEADS)))
    wx = w_in[:, o0 + 2 * MIX_WIDTH + 2 * A_HEADS:]
    w_all = jnp.concatenate([wq, wk, wv, wo, wx, wg], axis=1).astype(BF16)
    cw = jnp.concatenate([_pad_heads(conv_w[:, :qk_w], A_HEADS, A_QK_DIM, A_QK_PAD),
                          _pad_heads(conv_w[:, qk_w:], A_HEADS, A_QK_DIM, A_QK_PAD)], axis=1)
    cb = jnp.concatenate([_pad_heads(conv_b[None, :qk_w], A_HEADS, A_QK_DIM, A_QK_PAD),
                          _pad_heads(conv_b[None, qk_w:], A_HEADS, A_QK_DIM, A_QK_PAD)], axis=1)
    gb = jnp.pad(gate_b[None, :], ((0, 0), (0, LANES - 2 * A_HEADS)))
    return w_all, cw, cb, gb


def _pick(n, cap):
    b = min(n, cap)
    while n % b:
        b //= 2
    return b


def kernel(x, mem, norm_mix_g, norm_mem_g, norm_ffn_g, norm_out_g, w_mem_kv, a_w_in, a_gate_b, a_conv_w, a_conv_b, a_head_g, a_w_out, b_w_in, b_lb_logits, b_head_g, b_w_out, ffn_w_up, ffn_conv_w, ffn_conv_b, ffn_w_down):
    batch, seq, d = x.shape
    mem_len = mem.shape[1]
    depth = norm_mix_g.shape[0]
    t = batch * seq
    tm_proj = _pick(seq, 256)
    tm_out = _pick(seq, 512)
    tm_ffn = _pick(seq, 1024)
    tb = _pick(seq, 256)

    h = x.reshape(t, d)
    mem2 = mem.reshape(batch * mem_len, d)
    for layer in range(depth):
        j = layer // N_MIXERS
        kv = _norm_matmul(mem2, _gain_folded(w_mem_kv[layer], norm_mem_g[layer]), _pick(mem_len, 256))
        if layer % N_MIXERS == 0:
            w_all, cw, cb, gb = _mlstm_weights(norm_mix_g[layer][:, None] * a_w_in[j], a_gate_b[j], a_conv_w[j],
                                               a_conv_b[j])
            zq, zk, zv, zo, zxq, zg = _proj_mlstm(h, w_all, cw, cb, seq, tm_proj)
            y_mix = _mlstm(zq, zk, zv, zo, zg, gb, a_head_g[j].reshape(1, MIX_WIDTH), batch, seq, tb,
                           A_HEADS_PER_STEP)
            w_out = a_w_out[j]
        else:
            zq, zlf, zkk, zv, zsg, zxq = _proj_hgrn2(h, _gain_folded(b_w_in[j], norm_mix_g[layer]), b_lb_logits,
                                                     tm_proj, layer)
            y_mix = _hgrn2(zq, zlf, zkk, zv, zsg, b_head_g[j].reshape(1, MIX_WIDTH), batch, seq, tb,
                           B_HEADS_PER_STEP)
            w_out = b_w_out[j]
        h, hn, rs = _outproj(y_mix, zxq, kv, w_out.astype(BF16), h, seq, mem_len, tm_out)
        h = _ffn(hn, rs, h, _gain_folded(ffn_w_up[layer], norm_ffn_g[layer]), ffn_conv_w[layer], ffn_conv_b[layer],
                 ffn_w_down[layer].astype(BF16), seq, tm_ffn, 512, 512)
    return _final_norm(h, norm_out_g, tm_out).reshape(batch, seq, d)
```

```python
import functools

import numpy as np
import jax
import jax.numpy as jnp
from jax import lax
from jax.experimental import pallas as pl
from jax.experimental.pallas import tpu as pltpu

F32 = jnp.float32
BF16 = jnp.bfloat16

D_MODEL = 2048
N_MIXERS = 2
CHUNK = 64
EPS = 1e-6

XA_HEADS = 4
XA_WIDTH = D_MODEL // 4
XA_HEAD_DIM = XA_WIDTH // XA_HEADS
MIX_WIDTH = D_MODEL - XA_WIDTH

A_HEADS = 4
A_V_DIM = MIX_WIDTH // A_HEADS
A_QK_DIM = A_V_DIM // 2
A_QK_PAD = 256
A_CONV = 4
A_AUG = 128
A_HEADS_PER_STEP = 4
A_CHUNK = 128
B_HEADS = 12
B_K_DIM = 128
B_V_DIM = 128
B_LEVELS = 6
B_HEADS_PER_STEP = 12
B_CHUNKS_PER_ITER = 2

D_FF = 5632
FFN_CONV = 3

LANES = 128
SUBLANES = 8
VMEM_LIMIT_BYTES = 56 * 1024 * 1024

_NN = (((1,), (0,)), ((), ()))
_NT = (((1,), (1,)), ((), ()))
_TN = (((0,), (0,)), ((), ()))


def _dot(a, b, dims=_NN):
    return lax.dot_general(a, b, dims, preferred_element_type=F32)


def _split2(x):
    hi = x.astype(BF16)
    return hi, (x - hi.astype(F32)).astype(BF16)


def _dot_exact_lhs(a_bf, x, dims=_NN):
    return sum(_dot(a_bf, p, dims) for p in _split2(x))


def _dot_exact_rhs(x, b_bf, dims=_NN):
    return sum(_dot(p, b_bf, dims) for p in _split2(x))


def _sigmoid(x):
    return 0.5 + 0.5 * jnp.tanh(0.5 * x)


def _silu(x):
    hx = 0.5 * x
    return hx + hx * jnp.tanh(hx)


def _log_sigmoid(x):
    return jnp.minimum(x, 0.0) - jnp.log(1.0 + jnp.exp(-jnp.abs(x)))


def _params(n_grid):
    return pltpu.CompilerParams(dimension_semantics=("arbitrary",) * n_grid,
                                vmem_limit_bytes=VMEM_LIMIT_BYTES)


def _resident(shape):
    nd = len(shape)
    return pl.BlockSpec(shape, lambda *_: (0,) * nd, pipeline_mode=pl.Buffered(1))


def _gain_folded(w, g):
    return (g[:, None] * w).astype(BF16)


def _rms_split(x_ref):
    x = x_ref[...]
    return x.astype(BF16), lax.rsqrt(jnp.mean(x * x, axis=-1, keepdims=True) + EPS)


def _norm_matmul_kernel(x_ref, w_ref, out_ref):
    x_bf, rs = _rms_split(x_ref)
    out_ref[...] = _dot(x_bf, w_ref[...]) * rs


def _norm_matmul(x, wg_bf, tm):
    t, d = x.shape
    n = wg_bf.shape[1]
    return pl.pallas_call(
        _norm_matmul_kernel,
        grid=(t // tm,),
        in_specs=[pl.BlockSpec((tm, d), lambda i: (i, 0)),
                  _resident((d, n))],
        out_specs=pl.BlockSpec((tm, n), lambda i: (i, 0)),
        out_shape=jax.ShapeDtypeStruct((t, n), F32),
        compiler_params=_params(1),
        name="norm_matmul",
    )(x, wg_bf)


def _proj_mlstm_kernel(x_ref, w_ref, cw_ref, cb_ref, q_out, k_out, v_out, o_out, xq_out, gt_out, cbuf,
                       *, tm, per_seq):
    i = pl.program_id(0)
    qk_w = 2 * A_HEADS * A_QK_PAD

    @pl.when(i % per_seq == 0)
    def _():
        cbuf[0:SUBLANES] = jnp.zeros((SUBLANES, qk_w), F32)

    x_bf, rs = _rms_split(x_ref)

    def proj(off, width):
        return _dot(x_bf, w_ref[:, off:off + width]) * rs

    cbuf[SUBLANES:SUBLANES + tm] = proj(0, qk_w)
    acc = cb_ref[...]
    for j in range(A_CONV):
        lo = SUBLANES - (A_CONV - 1) + j
        acc = acc + cw_ref[j:j + 1, :] * cbuf[lo:lo + tm, :]
    y = _silu(acc)
    q_out[...] = y[:, :qk_w // 2].astype(BF16)
    k_out[...] = (y[:, qk_w // 2:] * (A_QK_DIM ** -0.5)).astype(BF16)
    cbuf[0:SUBLANES] = cbuf[tm:tm + SUBLANES]
    off = qk_w
    v_out[...] = proj(off, MIX_WIDTH).astype(BF16)
    off += MIX_WIDTH
    o_out[...] = _sigmoid(proj(off, MIX_WIDTH))
    off += MIX_WIDTH
    xq_out[...] = proj(off, XA_WIDTH).astype(BF16)
    off += XA_WIDTH
    gt_out[...] = proj(off, LANES)


def _proj_mlstm(x, w_bf, conv_w, conv_b, seq, tm):
    t, d = x.shape
    n = w_bf.shape[1]
    qk_w = 2 * A_HEADS * A_QK_PAD
    widths = (qk_w // 2, qk_w // 2, MIX_WIDTH, MIX_WIDTH, XA_WIDTH, LANES)
    dtypes = (BF16, BF16, BF16, F32, BF16, F32)
    assert sum(widths) == n
    return pl.pallas_call(
        functools.partial(_proj_mlstm_kernel, tm=tm, per_seq=seq // tm),
        grid=(t // tm,),
        in_specs=[pl.BlockSpec((tm, d), lambda i: (i, 0)),
                  _resident((d, n)),
                  _resident((A_CONV, qk_w)),
                  _resident((1, qk_w))],
        out_specs=[pl.BlockSpec((tm, s), lambda i: (i, 0)) for s in widths],
        out_shape=[jax.ShapeDtypeStruct((t, s), dt) for s, dt in zip(widths, dtypes)],
        scratch_shapes=[pltpu.VMEM((tm + SUBLANES, qk_w), F32)],
        compiler_params=_params(1),
        name="proj_mlstm",
    )(x, w_bf, conv_w, conv_b)


def _proj_hgrn2_kernel(x_ref, w_ref, lbl_ref, q_out, lf_out, kk_out, v_out, sg_out, xq_out, *, layer):
    lg = lbl_ref[...]
    lg = lg - jnp.max(lg, axis=0, keepdims=True)
    pe = jnp.exp(lg)
    pr = pe / jnp.sum(pe, axis=0, keepdims=True)
    c0 = pr[0:1, :]
    cl = c0
    for r in range(1, layer + 1):
        cl = cl + pr[r:r + 1, :]
    lb = cl - c0
    om = 1.0 - lb

    x_bf, rs = _rms_split(x_ref)
    kw = B_HEADS * B_K_DIM

    def proj(off, width):
        return _dot(x_bf, w_ref[:, off:off + width]) * rs

    q_out[...] = _silu(proj(0, kw))
    th = 0.5 * jnp.tanh(0.5 * proj(kw, kw))
    lf_out[...] = jnp.log2(lb + om * (0.5 + th))
    kk_out[...] = om * (0.5 - th)
    off = 2 * kw
    v_out[...] = proj(off, MIX_WIDTH).astype(BF16)
    off += MIX_WIDTH
    sg_out[...] = _silu(proj(off, MIX_WIDTH))
    off += MIX_WIDTH
    xq_out[...] = proj(off, XA_WIDTH).astype(BF16)


def _proj_hgrn2(x, w_bf, lb_logits, tm, layer):
    t, d = x.shape
    n = w_bf.shape[1]
    kw = B_HEADS * B_K_DIM
    widths = (kw, kw, kw, MIX_WIDTH, MIX_WIDTH, XA_WIDTH)
    dtypes = (F32, F32, F32, BF16, F32, BF16)
    assert 2 * kw + 2 * MIX_WIDTH + XA_WIDTH == n
    return pl.pallas_call(
        functools.partial(_proj_hgrn2_kernel, layer=layer),
        grid=(t // tm,),
        in_specs=[pl.BlockSpec((tm, d), lambda i: (i, 0)),
                  _resident((d, n)),
                  _resident(lb_logits.shape)],
        out_specs=[pl.BlockSpec((tm, s), lambda i: (i, 0)) for s in widths],
        out_shape=[jax.ShapeDtypeStruct((t, s), dt) for s, dt in zip(widths, dtypes)],
        compiler_params=_params(1),
        name="proj_hgrn2",
    )(x, w_bf, lb_logits)


def _mlstm_kernel(q_ref, k_ref, v_ref, o_ref, gt_ref, gb_ref, hg_ref, tril_ref, out_ref, c_st, m_st, *, tb, hb):
    hgrp = pl.program_id(1)
    t = pl.program_id(2)
    aug_w = A_V_DIM + A_AUG
    rep = aug_w // LANES

    @pl.when(t == 0)
    def _():
        c_st[...] = jnp.zeros_like(c_st)
        m_st[...] = jnp.zeros_like(m_st)

    tril = tril_ref[...]
    row = lax.broadcasted_iota(jnp.int32, (A_CHUNK, A_CHUNK), 0)
    col = lax.broadcasted_iota(jnp.int32, (A_CHUNK, A_CHUNK), 1)
    causal = row >= col
    wide = hb * LANES
    srow = lax.broadcasted_iota(jnp.int32, (LANES, 2 * wide), 0)
    scol = lax.broadcasted_iota(jnp.int32, (LANES, 2 * wide), 1)
    blk = scol // LANES
    want = jnp.where(blk < hb, hgrp * hb + blk, A_HEADS + hgrp * hb + (blk - hb))
    sel = jnp.where(srow == want, 1.0, 0.0).astype(BF16)
    frow = lax.broadcasted_iota(jnp.int32, (A_CHUNK, LANES), 1)
    first = jnp.where(frow == 0, 1.0, 0.0).astype(BF16)
    ones_aug = jnp.ones((A_CHUNK, A_AUG), BF16)
    heads = range(hb)

    def lanes_of(hh):
        return slice(hh * LANES, (hh + 1) * LANES)

    def chunk(c, carry):
        rows = pl.ds(pl.multiple_of(c * A_CHUNK, A_CHUNK), A_CHUNK)
        gates = gt_ref[rows, :] + gb_ref[...]
        g2 = _dot_exact_rhs(gates, sel)
        ic = g2[:, :wide]
        lf = _log_sigmoid(g2[:, wide:])
        gc = _dot_exact_lhs(tril, lf)
        m_prev = m_st[0:1, :]
        a = gc + m_prev
        icg = ic - gc
        r_mat = [_dot_exact_lhs(first, icg[:, lanes_of(hh)], _NT) for hh in heads]
        dmat = [jnp.where(causal, gc[:, hh * LANES:hh * LANES + A_CHUNK] + r_mat[hh], -jnp.inf) for hh in heads]
        mx = jnp.concatenate([jnp.broadcast_to(jnp.max(dmat[hh], axis=1, keepdims=True), (A_CHUNK, LANES))
                              for hh in heads], axis=1)
        m_row = jnp.maximum(a, mx)
        w_inter = jnp.exp(a - m_row)
        e_neg = jnp.exp(-m_row)
        p = [jnp.exp(dmat[hh] - m_row[:, hh * LANES:hh * LANES + A_CHUNK]) for hh in heads]

        g_end = gc[A_CHUNK - 1:A_CHUNK, :]
        a_end = g_end + m_prev
        w_end = g_end - gc + ic
        m_new = jnp.maximum(a_end, jnp.max(w_end, axis=0, keepdims=True))
        decay = jnp.exp(a_end - m_new)
        ws = jnp.exp(w_end - m_new).astype(BF16)
        m_st[...] = jnp.broadcast_to(m_new, (SUBLANES, wide))

        qk_lanes = [slice(hh * A_QK_PAD, (hh + 1) * A_QK_PAD) for hh in heads]
        v_lanes = [slice(hh * A_V_DIM, (hh + 1) * A_V_DIM) for hh in heads]
        q = [q_ref[rows, qk_lanes[hh]] for hh in heads]
        k = [k_ref[rows, qk_lanes[hh]] for hh in heads]
        v_aug = [jnp.concatenate([v_ref[rows, v_lanes[hh]], ones_aug], axis=1) for hh in heads]
        qk = [_dot(q[hh], k[hh], _NT) for hh in heads]
        c_old = [c_st[hh] for hh in heads]
        inter = [_dot(q[hh], c_old[hh].astype(BF16)) for hh in heads]
        intra = [_dot((qk[hh] * p[hh]).astype(BF16), v_aug[hh]) for hh in heads]
        for hh in heads:
            kw = k[hh] * jnp.concatenate([ws[:, lanes_of(hh)]] * (A_QK_PAD // LANES), axis=1)
            c_st[hh] = (jnp.concatenate([decay[:, lanes_of(hh)]] * rep, axis=1) * c_old[hh]
                        + _dot(kw, v_aug[hh], _TN))
        for hh in heads:
            tot = jnp.concatenate([w_inter[:, lanes_of(hh)]] * rep, axis=1) * inter[hh] + intra[hh]
            den = tot[:, A_V_DIM:]
            inv = 1.0 / jnp.maximum(jnp.abs(den), e_neg[:, lanes_of(hh)])
            h_out = tot[:, :A_V_DIM] * jnp.concatenate([inv] * (A_V_DIM // LANES), axis=1)
            ms = jnp.mean(h_out * h_out, axis=-1, keepdims=True)
            hn = h_out * lax.rsqrt(ms + EPS) * hg_ref[:, v_lanes[hh]]
            out_ref[rows, v_lanes[hh]] = (o_ref[rows, v_lanes[hh]] * hn).astype(out_ref.dtype)
        return carry

    lax.fori_loop(0, tb // A_CHUNK, chunk, 0)


def _mlstm(zq, zk, zv, zo, zg, gate_b, head_g, batch, seq, tb, hb):
    t_total = batch * seq
    nt = seq // tb
    tril = jnp.asarray(np.tril(np.ones((A_CHUNK, A_CHUNK), np.float32)), BF16)
    row_map = lambda b, h, t: (b * nt + t, h)
    head_map = lambda b, h, t: (0, h)
    fixed = lambda b, h, t: (0, 0)
    return pl.pallas_call(
        functools.partial(_mlstm_kernel, tb=tb, hb=hb),
        grid=(batch, A_HEADS // hb, nt),
        in_specs=[pl.BlockSpec((tb, hb * A_QK_PAD), row_map),
                  pl.BlockSpec((tb, hb * A_QK_PAD), row_map),
                  pl.BlockSpec((tb, hb * A_V_DIM), row_map),
                  pl.BlockSpec((tb, hb * A_V_DIM), row_map),
                  pl.BlockSpec((tb, LANES), lambda b, h, t: (b * nt + t, 0)),
                  pl.BlockSpec((1, LANES), fixed),
                  pl.BlockSpec((1, hb * A_V_DIM), head_map),
                  pl.BlockSpec((A_CHUNK, A_CHUNK), fixed)],
        out_specs=pl.BlockSpec((tb, hb * A_V_DIM), row_map),
        out_shape=jax.ShapeDtypeStruct((t_total, MIX_WIDTH), BF16),
        scratch_shapes=[pltpu.VMEM((hb, A_QK_PAD, A_V_DIM + A_AUG), F32),
                        pltpu.VMEM((SUBLANES, hb * LANES), F32)],
        compiler_params=_params(3),
        name="mlstm",
    )(zq, zk, zv, zo, zg, gate_b, head_g, tril)


def _hgrn2_pair_masks():
    n = CHUNK
    masks = [np.eye(n, dtype=np.float32)]
    for j in range(1, B_LEVELS + 1):
        c = n >> j
        pm = np.zeros((n, n), np.float32)
        for r in range(n):
            if r % (2 * c) >= c:
                mid = (r // (2 * c)) * 2 * c + c
                pm[r, mid - c:mid] = 1.0
        masks.append(pm)
    return np.stack(masks, axis=0)


def _hgrn2_kernel(q_ref, lf_ref, kk_ref, v_ref, sg_ref, hg_ref, tril_ref, mask_ref, out_ref, s_st, *, tb, hb):
    t = pl.program_id(2)

    @pl.when(t == 0)
    def _():
        s_st[...] = jnp.zeros_like(s_st)

    tril = tril_ref[...]
    wide = hb * B_K_DIM
    heads = range(hb)
    rowi = lax.broadcasted_iota(jnp.int32, (CHUNK, wide), 0)
    sub = lax.broadcasted_iota(jnp.int32, (CHUNK // SUBLANES, SUBLANES, wide), 1)

    def level_operand(q, kk, gc, lf, half):
        if half >= SUBLANES:
            pieces = []
            for b0 in range(0, CHUNK, 2 * half):
                mid = b0 + half
                ref = gc[mid - 1:mid, :]
                pieces.append(kk[b0:mid, :] * jnp.exp2(ref - gc[b0:mid, :]))
                pieces.append(q[mid:mid + half, :] * jnp.exp2(gc[mid:mid + half, :] - ref))
            return jnp.concatenate(pieces, axis=0)
        if half == 1:
            odd = (rowi & 1) != 0
            return jnp.where(odd, q * jnp.exp2(lf), kk)
        g3 = gc.reshape(CHUNK // SUBLANES, SUBLANES, wide)
        ref = g3[:, half - 1:half, :]
        for b0 in range(2 * half, SUBLANES, 2 * half):
            ref = jnp.where(sub >= b0, g3[:, b0 + half - 1:b0 + half, :], ref)
        d = g3 - ref
        upper = (sub & half) != 0
        ex = jnp.exp2(jnp.where(upper, d, -d)).reshape(CHUNK, wide)
        return jnp.where((rowi & half) != 0, q, kk) * ex

    def hl(x, hh):
        return x[:, hh * B_K_DIM:(hh + 1) * B_K_DIM]

    def chunk_group(c, carry):
        group = range(B_CHUNKS_PER_ITER)
        rows = [pl.ds(pl.multiple_of((c * B_CHUNKS_PER_ITER + i) * CHUNK, CHUNK), CHUNK) for i in group]
        q = [q_ref[r, :] for r in rows]
        lf = [lf_ref[r, :] for r in rows]
        kk = [kk_ref[r, :] for r in rows]
        v = [v_ref[r, :] for r in rows]
        gc = [_dot_exact_lhs(tril, lf[i]) for i in group]
        g_end = [gc[i][CHUNK - 1:CHUNK, :] for i in group]

        xs = [[level_operand(q[i], kk[i], gc[i], lf[i], CHUNK >> j).astype(BF16) for j in range(1, B_LEVELS + 1)]
              for i in group]
        q_bf = [q[i].astype(BF16) for i in group]
        k_bf = [kk[i].astype(BF16) for i in group]
        qg = [(q[i] * jnp.exp2(gc[i])).astype(BF16) for i in group]
        kd = [(kk[i] * jnp.exp2(g_end[i] - gc[i])).astype(BF16) for i in group]
        dec = [jnp.exp2(g_end[i]) for i in group]

        prod = [[[_dot(hl(q_bf[i], hh), hl(k_bf[i], hh), _NT)] + [_dot(hl(x, hh), hl(x, hh), _NT) for x in xs[i]]
                 for hh in heads] for i in group]
        upd = [[_dot(hl(v[i], hh), hl(kd[i], hh), _TN) for hh in heads] for i in group]
        state = [s_st[hh] for hh in heads]
        o_inter = []
        for i in group:
            o_inter.append([_dot(hl(qg[i], hh), state[hh].astype(BF16), _NT) for hh in heads])
            state = [state[hh] * hl(dec[i], hh) + upd[i][hh] for hh in heads]
        for hh in heads:
            s_st[hh] = state[hh]
        for i in group:
            outs = []
            for hh in heads:
                acc = mask_ref[0] * prod[i][hh][0]
                for j in range(1, B_LEVELS + 1):
                    acc = acc + mask_ref[j] * prod[i][hh][j]
                o = _dot(acc.astype(BF16), hl(v[i], hh)) + o_inter[i][hh]
                ms = jnp.mean(o * o, axis=-1, keepdims=True)
                outs.append(o * lax.rsqrt(ms + EPS))
            on = jnp.concatenate(outs, axis=1) * hg_ref[...]
            out_ref[rows[i], :] = (on * sg_ref[rows[i], :]).astype(out_ref.dtype)
        return carry

    lax.fori_loop(0, tb // (CHUNK * B_CHUNKS_PER_ITER), chunk_group, 0)


def _hgrn2(zq, zlf, zkk, zv, zsg, head_g, batch, seq, tb, hb):
    t_total = batch * seq
    nt = seq // tb
    mask_np = _hgrn2_pair_masks()
    tril = jnp.asarray(np.tril(np.ones((CHUNK, CHUNK), np.float32)), BF16)
    row_map = lambda b, h, t: (b * nt + t, h)
    head_map = lambda b, h, t: (0, h)
    return pl.pallas_call(
        functools.partial(_hgrn2_kernel, tb=tb, hb=hb),
        grid=(batch, B_HEADS // hb, nt),
        in_specs=[pl.BlockSpec((tb, hb * B_K_DIM), row_map),
                  pl.BlockSpec((tb, hb * B_K_DIM), row_map),
                  pl.BlockSpec((tb, hb * B_K_DIM), row_map),
                  pl.BlockSpec((tb, hb * B_V_DIM), row_map),
                  pl.BlockSpec((tb, hb * B_V_DIM), row_map),
                  pl.BlockSpec((1, hb * B_V_DIM), head_map),
                  pl.BlockSpec((CHUNK, CHUNK), lambda b, h, t: (0, 0)),
                  pl.BlockSpec(mask_np.shape, lambda b, h, t: (0, 0, 0))],
        out_specs=pl.BlockSpec((tb, hb * B_V_DIM), row_map),
        out_shape=jax.ShapeDtypeStruct((t_total, MIX_WIDTH), BF16),
        scratch_shapes=[pltpu.VMEM((hb, B_V_DIM, B_K_DIM), F32)],
        compiler_params=_params(3),
        name="hgrn2",
    )(zq, zlf, zkk, zv, zsg, head_g, tril, jnp.asarray(mask_np, F32))


def _outproj_kernel(y_ref, xq_ref, kv_ref, wo_ref, h_ref, out_ref, hn_ref, rs_ref):
    kv = kv_ref[...]
    parts = []
    for hh in range(XA_HEADS):
        lo = hh * XA_HEAD_DIM
        qh = xq_ref[:, lo:lo + XA_HEAD_DIM]
        kh = kv[:, lo:lo + XA_HEAD_DIM].astype(BF16)
        vh = kv[:, XA_WIDTH + lo:XA_WIDTH + lo + XA_HEAD_DIM].astype(BF16)
        s = _dot(qh, kh, _NT) * (XA_HEAD_DIM ** -0.5)
        e = jnp.exp(s - jnp.max(s, axis=-1, keepdims=True))
        den = jnp.sum(e, axis=-1, keepdims=True)
        parts.append((_dot(e.astype(BF16), vh) / den).astype(BF16))
    y_mem = jnp.concatenate(parts, axis=1)
    h_new = (h_ref[...] + _dot(y_ref[...], wo_ref[0:MIX_WIDTH, :])
             + _dot(y_mem, wo_ref[MIX_WIDTH:D_MODEL, :]))
    out_ref[...] = h_new
    hn_ref[...] = h_new.astype(BF16)
    rs = lax.rsqrt(jnp.mean(h_new * h_new, axis=-1, keepdims=True) + EPS)
    rs_ref[...] = jnp.broadcast_to(rs, rs_ref.shape)


def _outproj(y_mix, zxq, kv, w_out_bf, h, seq, mem_len, tm):
    t = h.shape[0]
    per_seq = seq // tm
    return pl.pallas_call(
        _outproj_kernel,
        grid=(t // tm,),
        in_specs=[pl.BlockSpec((tm, MIX_WIDTH), lambda i: (i, 0)),
                  pl.BlockSpec((tm, XA_WIDTH), lambda i: (i, 0)),
                  pl.BlockSpec((mem_len, 2 * XA_WIDTH), lambda i: (i // per_seq, 0)),
                  _resident((D_MODEL, D_MODEL)),
                  pl.BlockSpec((tm, D_MODEL), lambda i: (i, 0))],
        out_specs=[pl.BlockSpec((tm, D_MODEL), lambda i: (i, 0)),
                   pl.BlockSpec((tm, D_MODEL), lambda i: (i, 0)),
                   pl.BlockSpec((tm, LANES), lambda i: (i, 0))],
        out_shape=[jax.ShapeDtypeStruct((t, D_MODEL), F32),
                   jax.ShapeDtypeStruct((t, D_MODEL), BF16),
                   jax.ShapeDtypeStruct((t, LANES), F32)],
        compiler_params=_params(1),
        name="outproj",
    )(y_mix, zxq, kv, w_out_bf, h)


def _ffn_kernel(hn_ref, rs_ref, h_ref, wu_ref, wg_ref, cw_ref, cb_ref, wd_ref, out_ref,
                gbuf, gcarry, act_s, *, tm, nf, per_seq):
    i = pl.program_id(0)
    j = pl.program_id(1)
    tf = act_s.shape[2]

    @pl.when(j < nf)
    def _():
        hn = hn_ref[...]
        rs = jnp.concatenate([rs_ref[...]] * (tf // LANES), axis=1)
        g = _dot(hn, wg_ref[...]) * rs
        prev = jnp.where(i % per_seq == 0, 0.0, gcarry[j])
        gbuf[0:SUBLANES] = prev
        gbuf[SUBLANES:SUBLANES + tm] = g
        gcarry[j] = g[tm - SUBLANES:tm, :]
        gc = cb_ref[...]
        for k in range(FFN_CONV):
            lo = SUBLANES - (FFN_CONV - 1) + k
            gc = gc + cw_ref[k:k + 1, :] * gbuf[lo:lo + tm, :]
        sg = _silu(gc) * rs
        u = _dot(hn, wu_ref[...])
        act_s[j] = (sg * u).astype(BF16)

    @pl.when(j >= nf)
    def _():
        act = jnp.concatenate([act_s[f] for f in range(nf)], axis=1)
        out_ref[...] = h_ref[...] + _dot(act, wd_ref[...])


def _ffn(hn, rs, h, w_up_bf, conv_w, conv_b, w_down_bf, seq, tm, tf, tn):
    t, d = h.shape
    nf = D_FF // tf
    nd = d // tn
    per_seq = seq // tm
    up = lambda j: jnp.minimum(j, nf - 1)
    down = lambda j: jnp.maximum(j - nf, 0)
    return pl.pallas_call(
        functools.partial(_ffn_kernel, tm=tm, nf=nf, per_seq=per_seq),
        grid=(t // tm, nf + nd),
        in_specs=[pl.BlockSpec((tm, d), lambda i, j: (i, 0)),
                  pl.BlockSpec((tm, LANES), lambda i, j: (i, 0)),
                  pl.BlockSpec((tm, tn), lambda i, j: (i, down(j))),
                  pl.BlockSpec((d, tf), lambda i, j: (0, up(j))),
                  pl.BlockSpec((d, tf), lambda i, j: (0, nf + up(j))),
                  pl.BlockSpec((FFN_CONV, tf), lambda i, j: (0, up(j))),
                  pl.BlockSpec((1, tf), lambda i, j: (0, up(j))),
                  pl.BlockSpec((D_FF, tn), lambda i, j: (0, down(j)))],
        out_specs=pl.BlockSpec((tm, tn), lambda i, j: (i, down(j))),
        out_shape=jax.ShapeDtypeStruct((t, d), F32),
        scratch_shapes=[pltpu.VMEM((tm + SUBLANES, tf), F32),
                        pltpu.VMEM((nf, SUBLANES, tf), F32),
                        pltpu.VMEM((nf, tm, tf), BF16)],
        compiler_params=_params(2),
        name="ffn",
    )(hn, rs, h, w_up_bf, w_up_bf, conv_w, conv_b.reshape(1, D_FF), w_down_bf)


def _final_norm_kernel(x_ref, g_ref, out_ref):
    x = x_ref[...]
    out_ref[...] = x * lax.rsqrt(jnp.mean(x * x, axis=-1, keepdims=True) + EPS) * g_ref[...]


def _final_norm(x, g, tm):
    t, d = x.shape
    return pl.pallas_call(
        _final_norm_kernel,
        grid=(t // tm,),
        in_specs=[pl.BlockSpec((tm, d), lambda i: (i, 0)),
                  pl.BlockSpec((1, d), lambda i: (0, 0))],
        out_specs=pl.BlockSpec((tm, d), lambda i: (i, 0)),
        out_shape=jax.ShapeDtypeStruct((t, d), F32),
        compiler_params=_params(1),
        name="final_norm",
    )(x, g.reshape(1, d))


def _pad_heads(w, heads, dim, pad):
    lead = w.shape[:-1]
    w = w.reshape(lead + (heads, dim))
    w = jnp.pad(w, [(0, 0)] * len(lead) + [(0, 0), (0, pad - dim)])
    return w.reshape(lead + (heads * pad,))


def _mlstm_weights(w_in, gate_b, conv_w, conv_b):
    qk_w = A_HEADS * A_QK_DIM
    o0 = 2 * qk_w
    wq = _pad_heads(w_in[:, :qk_w], A_HEADS, A_QK_DIM, A_QK_PAD)
    wk = _pad_heads(w_in[:, qk_w:o0], A_HEADS, A_QK_DIM, A_QK_PAD)
    wv = w_in[:, o0:o0 + MIX_WIDTH]
    wo = w_in[:, o0 + MIX_WIDTH:o0 + 2 * MIX_WIDTH]
    wg = jnp.pad(w_in[:, o0 + 2 * MIX_WIDTH:o0 + 2 * MIX_WIDTH + 2 * A_HEADS], ((0, 0), (0, LANES - 2 * A_HEADS)))
    wx = w_in[:, o0 + 2 * MIX_WIDTH + 2 * A_HEADS:]
    w_all = jnp.concatenate([wq, wk, wv, wo, wx, wg], axis=1)
    cw = jnp.concatenate([_pad_heads(conv_w[:, :qk_w], A_HEADS, A_QK_DIM, A_QK_PAD),
                          _pad_heads(conv_w[:, qk_w:], A_HEADS, A_QK_DIM, A_QK_PAD)], axis=1)
    cb = jnp.concatenate([_pad_heads(conv_b[None, :qk_w], A_HEADS, A_QK_DIM, A_QK_PAD),
                          _pad_heads(conv_b[None, qk_w:], A_HEADS, A_QK_DIM, A_QK_PAD)], axis=1)
    gb = jnp.pad(gate_b[None, :], ((0, 0), (0, LANES - 2 * A_HEADS)))
    return w_all, cw, cb, gb


def _pick(n, cap):
    b = min(n, cap)
    while n % b:
        b //= 2
    return b


def kernel(x, mem, norm_mix_g, norm_mem_g, norm_ffn_g, norm_out_g, w_mem_kv, a_w_in, a_gate_b, a_conv_w, a_conv_b, a_head_g, a_w_out, b_w_in, b_lb_logits, b_head_g, b_w_out, ffn_w_up, ffn_conv_w, ffn_conv_b, ffn_w_down):
    batch, seq, d = x.shape
    mem_len = mem.shape[1]
    depth = norm_mix_g.shape[0]
    t = batch * seq
    tm_proj = _pick(seq, 256)
    tm_out = _pick(seq, 512)
    tm_ffn = _pick(seq, 1024)
    tb = _pick(seq, 512)

    h = x.reshape(t, d)
    mem2 = mem.reshape(batch * mem_len, d)
    for layer in range(depth):
        j = layer // N_MIXERS
        kv = _norm_matmul(mem2, _gain_folded(w_mem_kv[layer], norm_mem_g[layer]), _pick(mem_len, 256))
        if layer % N_MIXERS == 0:
            w_all, cw, cb, gb = _mlstm_weights(_gain_folded(a_w_in[j], norm_mix_g[layer]), a_gate_b[j], a_conv_w[j],
                                               a_conv_b[j])
            zq, zk, zv, zo, zxq, zg = _proj_mlstm(h, w_all, cw, cb, seq, tm_proj)
            y_mix = _mlstm(zq, zk, zv, zo, zg, gb, a_head_g[j].reshape(1, MIX_WIDTH), batch, seq, tb,
                           A_HEADS_PER_STEP)
            w_out = a_w_out[j]
        else:
            zq, zlf, zkk, zv, zsg, zxq = _proj_hgrn2(h, _gain_folded(b_w_in[j], norm_mix_g[layer]), b_lb_logits,
                                                     tm_proj, layer)
            y_mix = _hgrn2(zq, zlf, zkk, zv, zsg, b_head_g[j].reshape(1, MIX_WIDTH), batch, seq, tb,
                           B_HEADS_PER_STEP)
            w_out = b_w_out[j]
        h, hn, rs = _outproj(y_mix, zxq, kv, w_out.astype(BF16), h, seq, mem_len, tm_out)
        h = _ffn(hn, rs, h, _gain_folded(ffn_w_up[layer], norm_ffn_g[layer]), ffn_conv_w[layer], ffn_conv_b[layer],
                 ffn_w_down[layer].astype(BF16), seq, tm_ffn, 512, 512)
    return _final_norm(h, norm_out_g, tm_out).reshape(batch, seq, d)
```

```python
import functools

import numpy as np
import jax
import jax.numpy as jnp
from jax import lax
from jax.experimental import pallas as pl
from jax.experimental.pallas import tpu as pltpu

F32 = jnp.float32
BF16 = jnp.bfloat16

D_MODEL = 2048
N_MIXERS = 2
CHUNK = 64
EPS = 1e-6

XA_HEADS = 4
XA_WIDTH = D_MODEL // 4
XA_HEAD_DIM = XA_WIDTH // XA_HEADS
MIX_WIDTH = D_MODEL - XA_WIDTH

A_HEADS = 4
A_V_DIM = MIX_WIDTH // A_HEADS
A_QK_DIM = A_V_DIM // 2
A_QK_PAD = 256
A_CONV = 4
A_AUG = 128
A_HEADS_PER_STEP = 4
A_CHUNK = 128
B_HEADS = 12
B_K_DIM = 128
B_V_DIM = 128
B_LEVELS = 6
B_HEADS_PER_STEP = 12
B_CHUNKS_PER_ITER = 2

D_FF = 5632
FFN_CONV = 3

LANES = 128
SUBLANES = 8
VMEM_LIMIT_BYTES = 56 * 1024 * 1024

_NN = (((1,), (0,)), ((), ()))
_NT = (((1,), (1,)), ((), ()))
_TN = (((0,), (0,)), ((), ()))


def _dot(a, b, dims=_NN):
    return lax.dot_general(a, b, dims, preferred_element_type=F32)


def _split2(x):
    hi = x.astype(BF16)
    return hi, (x - hi.astype(F32)).astype(BF16)


def _dot_exact_lhs(a_bf, x, dims=_NN):
    return sum(_dot(a_bf, p, dims) for p in _split2(x))


def _dot_exact_rhs(x, b_bf, dims=_NN):
    return sum(_dot(p, b_bf, dims) for p in _split2(x))


def _sigmoid(x):
    return 0.5 + 0.5 * jnp.tanh(0.5 * x)


def _silu(x):
    hx = 0.5 * x
    return hx + hx * jnp.tanh(hx)


def _log_sigmoid(x):
    return jnp.minimum(x, 0.0) - jnp.log(1.0 + jnp.exp(-jnp.abs(x)))


def _params(n_grid):
    return pltpu.CompilerParams(dimension_semantics=("arbitrary",) * n_grid,
                                vmem_limit_bytes=VMEM_LIMIT_BYTES)


def _resident(shape):
    nd = len(shape)
    return pl.BlockSpec(shape, lambda *_: (0,) * nd, pipeline_mode=pl.Buffered(1))


def _gain_folded(w, g):
    return (g[:, None] * w).astype(BF16)


def _rms_split(x_ref):
    x = x_ref[...]
    return x.astype(BF16), lax.rsqrt(jnp.mean(x * x, axis=-1, keepdims=True) + EPS)


def _norm_matmul_kernel(x_ref, w_ref, out_ref):
    x_bf, rs = _rms_split(x_ref)
    out_ref[...] = _dot(x_bf, w_ref[...]) * rs


def _norm_matmul(x, wg_bf, tm):
    t, d = x.shape
    n = wg_bf.shape[1]
    return pl.pallas_call(
        _norm_matmul_kernel,
        grid=(t // tm,),
        in_specs=[pl.BlockSpec((tm, d), lambda i: (i, 0)),
                  _resident((d, n))],
        out_specs=pl.BlockSpec((tm, n), lambda i: (i, 0)),
        out_shape=jax.ShapeDtypeStruct((t, n), F32),
        compiler_params=_params(1),
        name="norm_matmul",
    )(x, wg_bf)


def _proj_mlstm_kernel(x_ref, w_ref, cw_ref, cb_ref, q_out, k_out, v_out, o_out, xq_out, gt_out, cbuf,
                       *, tm, per_seq):
    i = pl.program_id(0)
    qk_w = 2 * A_HEADS * A_QK_DIM

    @pl.when(i % per_seq == 0)
    def _():
        cbuf[0:SUBLANES] = jnp.zeros((SUBLANES, qk_w), F32)

    x_bf, rs = _rms_split(x_ref)

    def proj(off, width):
        return _dot(x_bf, w_ref[:, off:off + width]) * rs

    cbuf[SUBLANES:SUBLANES + tm] = proj(0, qk_w)
    acc = cb_ref[...]
    for j in range(A_CONV):
        lo = SUBLANES - (A_CONV - 1) + j
        acc = acc + cw_ref[j:j + 1, :] * cbuf[lo:lo + tm, :]
    y = _silu(acc)
    cbuf[0:SUBLANES] = cbuf[tm:tm + SUBLANES]
    pad = jnp.zeros((tm, A_QK_PAD - A_QK_DIM), F32)
    for hh in range(A_HEADS):
        qh = y[:, hh * A_QK_DIM:(hh + 1) * A_QK_DIM]
        kh = y[:, (A_HEADS + hh) * A_QK_DIM:(A_HEADS + hh + 1) * A_QK_DIM] * (A_QK_DIM ** -0.5)
        q_out[:, hh * A_QK_PAD:(hh + 1) * A_QK_PAD] = jnp.concatenate([qh, pad], axis=1).astype(BF16)
        k_out[:, hh * A_QK_PAD:(hh + 1) * A_QK_PAD] = jnp.concatenate([kh, pad], axis=1).astype(BF16)
    off = qk_w
    v_out[...] = proj(off, MIX_WIDTH).astype(BF16)
    off += MIX_WIDTH
    o_out[...] = _sigmoid(proj(off, MIX_WIDTH))
    off += MIX_WIDTH
    xq_out[...] = proj(off, XA_WIDTH).astype(BF16)
    off += XA_WIDTH
    gt_out[...] = proj(off, LANES)


def _proj_mlstm(x, w_bf, conv_w, conv_b, seq, tm):
    t, d = x.shape
    n = w_bf.shape[1]
    qk_w = 2 * A_HEADS * A_QK_DIM
    widths = (A_HEADS * A_QK_PAD, A_HEADS * A_QK_PAD, MIX_WIDTH, MIX_WIDTH, XA_WIDTH, LANES)
    dtypes = (BF16, BF16, BF16, F32, BF16, F32)
    assert qk_w + 2 * MIX_WIDTH + XA_WIDTH + LANES == n
    return pl.pallas_call(
        functools.partial(_proj_mlstm_kernel, tm=tm, per_seq=seq // tm),
        grid=(t // tm,),
        in_specs=[pl.BlockSpec((tm, d), lambda i: (i, 0)),
                  _resident((d, n)),
                  _resident((A_CONV, qk_w)),
                  _resident((1, qk_w))],
        out_specs=[pl.BlockSpec((tm, s), lambda i: (i, 0)) for s in widths],
        out_shape=[jax.ShapeDtypeStruct((t, s), dt) for s, dt in zip(widths, dtypes)],
        scratch_shapes=[pltpu.VMEM((tm + SUBLANES, qk_w), F32)],
        compiler_params=_params(1),
        name="proj_mlstm",
    )(x, w_bf, conv_w, conv_b)


def _proj_hgrn2_kernel(x_ref, w_ref, lbl_ref, q_out, lf_out, kk_out, v_out, sg_out, xq_out, *, layer):
    lg = lbl_ref[...]
    lg = lg - jnp.max(lg, axis=0, keepdims=True)
    pe = jnp.exp(lg)
    pr = pe / jnp.sum(pe, axis=0, keepdims=True)
    c0 = pr[0:1, :]
    cl = c0
    for r in range(1, layer + 1):
        cl = cl + pr[r:r + 1, :]
    lb = cl - c0
    om = 1.0 - lb

    x_bf, rs = _rms_split(x_ref)
    kw = B_HEADS * B_K_DIM

    def proj(off, width):
        return _dot(x_bf, w_ref[:, off:off + width]) * rs

    q_out[...] = _silu(proj(0, kw))
    th = 0.5 * jnp.tanh(0.5 * proj(kw, kw))
    lf_out[...] = jnp.log2(lb + om * (0.5 + th))
    kk_out[...] = om * (0.5 - th)
    off = 2 * kw
    v_out[...] = proj(off, MIX_WIDTH).astype(BF16)
    off += MIX_WIDTH
    sg_out[...] = _silu(proj(off, MIX_WIDTH))
    off += MIX_WIDTH
    xq_out[...] = proj(off, XA_WIDTH).astype(BF16)


def _proj_hgrn2(x, w_bf, lb_logits, tm, layer):
    t, d = x.shape
    n = w_bf.shape[1]
    kw = B_HEADS * B_K_DIM
    widths = (kw, kw, kw, MIX_WIDTH, MIX_WIDTH, XA_WIDTH)
    dtypes = (F32, F32, F32, BF16, F32, BF16)
    assert 2 * kw + 2 * MIX_WIDTH + XA_WIDTH == n
    return pl.pallas_call(
        functools.partial(_proj_hgrn2_kernel, layer=layer),
        grid=(t // tm,),
        in_specs=[pl.BlockSpec((tm, d), lambda i: (i, 0)),
                  _resident((d, n)),
                  _resident(lb_logits.shape)],
        out_specs=[pl.BlockSpec((tm, s), lambda i: (i, 0)) for s in widths],
        out_shape=[jax.ShapeDtypeStruct((t, s), dt) for s, dt in zip(widths, dtypes)],
        compiler_params=_params(1),
        name="proj_hgrn2",
    )(x, w_bf, lb_logits)


def _mlstm_kernel(q_ref, k_ref, v_ref, o_ref, gt_ref, gb_ref, hg_ref, tril_ref, out_ref, c_st, m_st, *, tb, hb):
    hgrp = pl.program_id(1)
    t = pl.program_id(2)
    aug_w = A_V_DIM + A_AUG
    rep = aug_w // LANES

    @pl.when(t == 0)
    def _():
        c_st[...] = jnp.zeros_like(c_st)
        m_st[...] = jnp.zeros_like(m_st)

    tril = tril_ref[...]
    row = lax.broadcasted_iota(jnp.int32, (A_CHUNK, A_CHUNK), 0)
    col = lax.broadcasted_iota(jnp.int32, (A_CHUNK, A_CHUNK), 1)
    causal = row >= col
    wide = hb * LANES
    srow = lax.broadcasted_iota(jnp.int32, (LANES, 2 * wide), 0)
    scol = lax.broadcasted_iota(jnp.int32, (LANES, 2 * wide), 1)
    blk = scol // LANES
    want = jnp.where(blk < hb, hgrp * hb + blk, A_HEADS + hgrp * hb + (blk - hb))
    sel = jnp.where(srow == want, 1.0, 0.0).astype(BF16)
    frow = lax.broadcasted_iota(jnp.int32, (A_CHUNK, LANES), 1)
    first = jnp.where(frow == 0, 1.0, 0.0).astype(BF16)
    ones_aug = jnp.ones((A_CHUNK, A_AUG), BF16)
    heads = range(hb)

    def lanes_of(hh):
        return slice(hh * LANES, (hh + 1) * LANES)

    def chunk(c, carry):
        rows = pl.ds(pl.multiple_of(c * A_CHUNK, A_CHUNK), A_CHUNK)
        gates = gt_ref[rows, :] + gb_ref[...]
        g2 = _dot_exact_rhs(gates, sel)
        ic = g2[:, :wide]
        lf = _log_sigmoid(g2[:, wide:])
        gc = _dot_exact_lhs(tril, lf)
        m_prev = m_st[0:1, :]
        a = gc + m_prev
        icg = ic - gc
        r_mat = [_dot_exact_lhs(first, icg[:, lanes_of(hh)], _NT) for hh in heads]
        dmat = [jnp.where(causal, gc[:, hh * LANES:hh * LANES + A_CHUNK] + r_mat[hh], -jnp.inf) for hh in heads]
        mx = jnp.concatenate([jnp.broadcast_to(jnp.max(dmat[hh], axis=1, keepdims=True), (A_CHUNK, LANES))
                              for hh in heads], axis=1)
        m_row = jnp.maximum(a, mx)
        w_inter = jnp.exp(a - m_row)
        e_neg = jnp.exp(-m_row)
        p = [jnp.exp(dmat[hh] - m_row[:, hh * LANES:hh * LANES + A_CHUNK]) for hh in heads]

        g_end = gc[A_CHUNK - 1:A_CHUNK, :]
        a_end = g_end + m_prev
        w_end = g_end - gc + ic
        m_new = jnp.maximum(a_end, jnp.max(w_end, axis=0, keepdims=True))
        decay = jnp.exp(a_end - m_new)
        ws = jnp.exp(w_end - m_new).astype(BF16)
        m_st[...] = jnp.broadcast_to(m_new, (SUBLANES, wide))

        qk_lanes = [slice(hh * A_QK_PAD, (hh + 1) * A_QK_PAD) for hh in heads]
        v_lanes = [slice(hh * A_V_DIM, (hh + 1) * A_V_DIM) for hh in heads]
        q = [q_ref[rows, qk_lanes[hh]] for hh in heads]
        k = [k_ref[rows, qk_lanes[hh]] for hh in heads]
        v_aug = [jnp.concatenate([v_ref[rows, v_lanes[hh]], ones_aug], axis=1) for hh in heads]
        qk = [_dot(q[hh], k[hh], _NT) for hh in heads]
        c_old = [c_st[hh] for hh in heads]
        inter = [_dot(q[hh], c_old[hh].astype(BF16)) for hh in heads]
        intra = [_dot((qk[hh] * p[hh]).astype(BF16), v_aug[hh]) for hh in heads]
        for hh in heads:
            kw = k[hh] * jnp.concatenate([ws[:, lanes_of(hh)]] * (A_QK_PAD // LANES), axis=1)
            c_st[hh] = (jnp.concatenate([decay[:, lanes_of(hh)]] * rep, axis=1) * c_old[hh]
                        + _dot(kw, v_aug[hh], _TN))
        for hh in heads:
            tot = jnp.concatenate([w_inter[:, lanes_of(hh)]] * rep, axis=1) * inter[hh] + intra[hh]
            den = tot[:, A_V_DIM:]
            inv = 1.0 / jnp.maximum(jnp.abs(den), e_neg[:, lanes_of(hh)])
            h_out = tot[:, :A_V_DIM] * jnp.concatenate([inv] * (A_V_DIM // LANES), axis=1)
            ms = jnp.mean(h_out * h_out, axis=-1, keepdims=True)
            hn = h_out * lax.rsqrt(ms + EPS) * hg_ref[:, v_lanes[hh]]
            out_ref[rows, v_lanes[hh]] = (o_ref[rows, v_lanes[hh]] * hn).astype(out_ref.dtype)
        return carry

    lax.fori_loop(0, tb // A_CHUNK, chunk, 0)


def _mlstm(zq, zk, zv, zo, zg, gate_b, head_g, batch, seq, tb, hb):
    t_total = batch * seq
    nt = seq // tb
    tril = jnp.asarray(np.tril(np.ones((A_CHUNK, A_CHUNK), np.float32)), BF16)
    row_map = lambda b, h, t: (b * nt + t, h)
    head_map = lambda b, h, t: (0, h)
    fixed = lambda b, h, t: (0, 0)
    return pl.pallas_call(
        functools.partial(_mlstm_kernel, tb=tb, hb=hb),
        grid=(batch, A_HEADS // hb, nt),
        in_specs=[pl.BlockSpec((tb, hb * A_QK_PAD), row_map),
                  pl.BlockSpec((tb, hb * A_QK_PAD), row_map),
                  pl.BlockSpec((tb, hb * A_V_DIM), row_map),
                  pl.BlockSpec((tb, hb * A_V_DIM), row_map),
                  pl.BlockSpec((tb, LANES), lambda b, h, t: (b * nt + t, 0)),
                  pl.BlockSpec((1, LANES), fixed),
                  pl.BlockSpec((1, hb * A_V_DIM), head_map),
                  pl.BlockSpec((A_CHUNK, A_CHUNK), fixed)],
        out_specs=pl.BlockSpec((tb, hb * A_V_DIM), row_map),
        out_shape=jax.ShapeDtypeStruct((t_total, MIX_WIDTH), BF16),
        scratch_shapes=[pltpu.VMEM((hb, A_QK_PAD, A_V_DIM + A_AUG), F32),
                        pltpu.VMEM((SUBLANES, hb * LANES), F32)],
        compiler_params=_params(3),
        name="mlstm",
    )(zq, zk, zv, zo, zg, gate_b, head_g, tril)


def _hgrn2_pair_masks():
    n = CHUNK
    masks = [np.eye(n, dtype=np.float32)]
    for j in range(1, B_LEVELS + 1):
        c = n >> j
        pm = np.zeros((n, n), np.float32)
        for r in range(n):
            if r % (2 * c) >= c:
                mid = (r // (2 * c)) * 2 * c + c
                pm[r, mid - c:mid] = 1.0
        masks.append(pm)
    return np.stack(masks, axis=0)


def _hgrn2_kernel(q_ref, lf_ref, kk_ref, v_ref, sg_ref, hg_ref, tril_ref, mask_ref, out_ref, s_st, *, tb, hb):
    t = pl.program_id(2)

    @pl.when(t == 0)
    def _():
        s_st[...] = jnp.zeros_like(s_st)

    tril = tril_ref[...]
    wide = hb * B_K_DIM
    heads = range(hb)
    rowi = lax.broadcasted_iota(jnp.int32, (CHUNK, wide), 0)
    sub = lax.broadcasted_iota(jnp.int32, (CHUNK // SUBLANES, SUBLANES, wide), 1)

    def level_operand(q, kk, gc, lf, half):
        if half >= SUBLANES:
            pieces = []
            for b0 in range(0, CHUNK, 2 * half):
                mid = b0 + half
                ref = gc[mid - 1:mid, :]
                pieces.append(kk[b0:mid, :] * jnp.exp2(ref - gc[b0:mid, :]))
                pieces.append(q[mid:mid + half, :] * jnp.exp2(gc[mid:mid + half, :] - ref))
            return jnp.concatenate(pieces, axis=0)
        if half == 1:
            odd = (rowi & 1) != 0
            return jnp.where(odd, q * jnp.exp2(lf), kk)
        g3 = gc.reshape(CHUNK // SUBLANES, SUBLANES, wide)
        ref = g3[:, half - 1:half, :]
        for b0 in range(2 * half, SUBLANES, 2 * half):
            ref = jnp.where(sub >= b0, g3[:, b0 + half - 1:b0 + half, :], ref)
        d = g3 - ref
        upper = (sub & half) != 0
        ex = jnp.exp2(jnp.where(upper, d, -d)).reshape(CHUNK, wide)
        return jnp.where((rowi & half) != 0, q, kk) * ex

    def hl(x, hh):
        return x[:, hh * B_K_DIM:(hh + 1) * B_K_DIM]

    def chunk_group(c, carry):
        group = range(B_CHUNKS_PER_ITER)
        rows = [pl.ds(pl.multiple_of((c * B_CHUNKS_PER_ITER + i) * CHUNK, CHUNK), CHUNK) for i in group]
        q = [q_ref[r, :] for r in rows]
        lf = [lf_ref[r, :] for r in rows]
        kk = [kk_ref[r, :] for r in rows]
        v = [v_ref[r, :] for r in rows]
        gc = [_dot_exact_lhs(tril, lf[i]) for i in group]
        g_end = [gc[i][CHUNK - 1:CHUNK, :] for i in group]

        xs = [[level_operand(q[i], kk[i], gc[i], lf[i], CHUNK >> j).astype(BF16) for j in range(1, B_LEVELS + 1)]
              for i in group]
        q_bf = [q[i].astype(BF16) for i in group]
        k_bf = [kk[i].astype(BF16) for i in group]
        qg = [(q[i] * jnp.exp2(gc[i])).astype(BF16) for i in group]
        kd = [(kk[i] * jnp.exp2(g_end[i] - gc[i])).astype(BF16) for i in group]
        dec = [jnp.exp2(g_end[i]) for i in group]

        prod = [[[_dot(hl(q_bf[i], hh), hl(k_bf[i], hh), _NT)] + [_dot(hl(x, hh), hl(x, hh), _NT) for x in xs[i]]
                 for hh in heads] for i in group]
        upd = [[_dot(hl(v[i], hh), hl(kd[i], hh), _TN) for hh in heads] for i in group]
        state = [s_st[hh] for hh in heads]
        o_inter = []
        for i in group:
            o_inter.append([_dot(hl(qg[i], hh), state[hh].astype(BF16), _NT) for hh in heads])
            state = [state[hh] * hl(dec[i], hh) + upd[i][hh] for hh in heads]
        for hh in heads:
            s_st[hh] = state[hh]
        for i in group:
            outs = []
            for hh in heads:
                acc = mask_ref[0] * prod[i][hh][0]
                for j in range(1, B_LEVELS + 1):
                    acc = acc + mask_ref[j] * prod[i][hh][j]
                o = _dot(acc.astype(BF16), hl(v[i], hh)) + o_inter[i][hh]
                ms = jnp.mean(o * o, axis=-1, keepdims=True)
                outs.append(o * lax.rsqrt(ms + EPS))
            on = jnp.concatenate(outs, axis=1) * hg_ref[...]
            out_ref[rows[i], :] = (on * sg_ref[rows[i], :]).astype(out_ref.dtype)
        return carry

    lax.fori_loop(0, tb // (CHUNK * B_CHUNKS_PER_ITER), chunk_group, 0)


def _hgrn2(zq, zlf, zkk, zv, zsg, head_g, batch, seq, tb, hb):
    t_total = batch * seq
    nt = seq // tb
    mask_np = _hgrn2_pair_masks()
    tril = jnp.asarray(np.tril(np.ones((CHUNK, CHUNK), np.float32)), BF16)
    row_map = lambda b, h, t: (b * nt + t, h)
    head_map = lambda b, h, t: (0, h)
    return pl.pallas_call(
        functools.partial(_hgrn2_kernel, tb=tb, hb=hb),
        grid=(batch, B_HEADS // hb, nt),
        in_specs=[pl.BlockSpec((tb, hb * B_K_DIM), row_map),
                  pl.BlockSpec((tb, hb * B_K_DIM), row_map),
                  pl.BlockSpec((tb, hb * B_K_DIM), row_map),
                  pl.BlockSpec((tb, hb * B_V_DIM), row_map),
                  pl.BlockSpec((tb, hb * B_V_DIM), row_map),
                  pl.BlockSpec((1, hb * B_V_DIM), head_map),
                  pl.BlockSpec((CHUNK, CHUNK), lambda b, h, t: (0, 0)),
                  pl.BlockSpec(mask_np.shape, lambda b, h, t: (0, 0, 0))],
        out_specs=pl.BlockSpec((tb, hb * B_V_DIM), row_map),
        out_shape=jax.ShapeDtypeStruct((t_total, MIX_WIDTH), BF16),
        scratch_shapes=[pltpu.VMEM((hb, B_V_DIM, B_K_DIM), F32)],
        compiler_params=_params(3),
        name="hgrn2",
    )(zq, zlf, zkk, zv, zsg, head_g, tril, jnp.asarray(mask_np, F32))


def _outproj_kernel(y_ref, xq_ref, kv_ref, wo_ref, h_ref, out_ref, hn_ref, rs_ref):
    kv = kv_ref[...]
    parts = []
    for hh in range(XA_HEADS):
        lo = hh * XA_HEAD_DIM
        qh = xq_ref[:, lo:lo + XA_HEAD_DIM]
        kh = kv[:, lo:lo + XA_HEAD_DIM].astype(BF16)
        vh = kv[:, XA_WIDTH + lo:XA_WIDTH + lo + XA_HEAD_DIM].astype(BF16)
        s = _dot(qh, kh, _NT) * (XA_HEAD_DIM ** -0.5)
        e = jnp.exp(s - jnp.max(s, axis=-1, keepdims=True))
        den = jnp.sum(e, axis=-1, keepdims=True)
        parts.append((_dot(e.astype(BF16), vh) / den).astype(BF16))
    y_mem = jnp.concatenate(parts, axis=1)
    h_new = (h_ref[...] + _dot(y_ref[...], wo_ref[0:MIX_WIDTH, :])
             + _dot(y_mem, wo_ref[MIX_WIDTH:D_MODEL, :]))
    out_ref[...] = h_new
    hn_ref[...] = h_new.astype(BF16)
    rs = lax.rsqrt(jnp.mean(h_new * h_new, axis=-1, keepdims=True) + EPS)
    rs_ref[...] = jnp.broadcast_to(rs, rs_ref.shape)


def _outproj(y_mix, zxq, kv, w_out_bf, h, seq, mem_len, tm):
    t = h.shape[0]
    per_seq = seq // tm
    return pl.pallas_call(
        _outproj_kernel,
        grid=(t // tm,),
        in_specs=[pl.BlockSpec((tm, MIX_WIDTH), lambda i: (i, 0)),
                  pl.BlockSpec((tm, XA_WIDTH), lambda i: (i, 0)),
                  pl.BlockSpec((mem_len, 2 * XA_WIDTH), lambda i: (i // per_seq, 0)),
                  _resident((D_MODEL, D_MODEL)),
                  pl.BlockSpec((tm, D_MODEL), lambda i: (i, 0))],
        out_specs=[pl.BlockSpec((tm, D_MODEL), lambda i: (i, 0)),
                   pl.BlockSpec((tm, D_MODEL), lambda i: (i, 0)),
                   pl.BlockSpec((tm, LANES), lambda i: (i, 0))],
        out_shape=[jax.ShapeDtypeStruct((t, D_MODEL), F32),
                   jax.ShapeDtypeStruct((t, D_MODEL), BF16),
                   jax.ShapeDtypeStruct((t, LANES), F32)],
        compiler_params=_params(1),
        name="outproj",
    )(y_mix, zxq, kv, w_out_bf, h)


def _ffn_kernel(hn_ref, rs_ref, h_ref, wu_ref, wg_ref, cw_ref, cb_ref, wd_ref, out_ref,
                gbuf, gcarry, act_s, *, tm, nf, per_seq):
    i = pl.program_id(0)
    j = pl.program_id(1)
    tf = act_s.shape[2]

    @pl.when(j < nf)
    def _():
        hn = hn_ref[...]
        rs = jnp.concatenate([rs_ref[...]] * (tf // LANES), axis=1)
        g = _dot(hn, wg_ref[...]) * rs
        prev = jnp.where(i % per_seq == 0, 0.0, gcarry[j])
        gbuf[0:SUBLANES] = prev
        gbuf[SUBLANES:SUBLANES + tm] = g
        gcarry[j] = g[tm - SUBLANES:tm, :]
        gc = cb_ref[...]
        for k in range(FFN_CONV):
            lo = SUBLANES - (FFN_CONV - 1) + k
            gc = gc + cw_ref[k:k + 1, :] * gbuf[lo:lo + tm, :]
        sg = _silu(gc) * rs
        u = _dot(hn, wu_ref[...])
        act_s[j] = (sg * u).astype(BF16)

    @pl.when(j >= nf)
    def _():
        act = jnp.concatenate([act_s[f] for f in range(nf)], axis=1)
        out_ref[...] = h_ref[...] + _dot(act, wd_ref[...])


def _ffn(hn, rs, h, w_up_bf, conv_w, conv_b, w_down_bf, seq, tm, tf, tn):
    t, d = h.shape
    nf = D_FF // tf
    nd = d // tn
    per_seq = seq // tm
    up = lambda j: jnp.minimum(j, nf - 1)
    down = lambda j: jnp.maximum(j - nf, 0)
    return pl.pallas_call(
        functools.partial(_ffn_kernel, tm=tm, nf=nf, per_seq=per_seq),
        grid=(t // tm, nf + nd),
        in_specs=[pl.BlockSpec((tm, d), lambda i, j: (i, 0)),
                  pl.BlockSpec((tm, LANES), lambda i, j: (i, 0)),
                  pl.BlockSpec((tm, tn), lambda i, j: (i, down(j))),
                  pl.BlockSpec((d, tf), lambda i, j: (0, up(j))),
                  pl.BlockSpec((d, tf), lambda i, j: (0, nf + up(j))),
                  pl.BlockSpec((FFN_CONV, tf), lambda i, j: (0, up(j))),
                  pl.BlockSpec((1, tf), lambda i, j: (0, up(j))),
                  pl.BlockSpec((D_FF, tn), lambda i, j: (0, down(j)))],
        out_specs=pl.BlockSpec((tm, tn), lambda i, j: (i, down(j))),
        out_shape=jax.ShapeDtypeStruct((t, d), F32),
        scratch_shapes=[pltpu.VMEM((tm + SUBLANES, tf), F32),
                        pltpu.VMEM((nf, SUBLANES, tf), F32),
                        pltpu.VMEM((nf, tm, tf), BF16)],
        compiler_params=_params(2),
        name="ffn",
    )(hn, rs, h, w_up_bf, w_up_bf, conv_w, conv_b.reshape(1, D_FF), w_down_bf)


def _final_norm_kernel(x_ref, g_ref, out_ref):
    x = x_ref[...]
    out_ref[...] = x * lax.rsqrt(jnp.mean(x * x, axis=-1, keepdims=True) + EPS) * g_ref[...]


def _final_norm(x, g, tm):
    t, d = x.shape
    return pl.pallas_call(
        _final_norm_kernel,
        grid=(t // tm,),
        in_specs=[pl.BlockSpec((tm, d), lambda i: (i, 0)),
                  pl.BlockSpec((1, d), lambda i: (0, 0))],
        out_specs=pl.BlockSpec((tm, d), lambda i: (i, 0)),
        out_shape=jax.ShapeDtypeStruct((t, d), F32),
        compiler_params=_params(1),
        name="final_norm",
    )(x, g.reshape(1, d))


def _mlstm_weights(w_in, gate_b):
    g0 = 2 * A_HEADS * A_QK_DIM + 2 * MIX_WIDTH
    wg = jnp.pad(w_in[:, g0:g0 + 2 * A_HEADS], ((0, 0), (0, LANES - 2 * A_HEADS)))
    w_all = jnp.concatenate([w_in[:, :g0], w_in[:, g0 + 2 * A_HEADS:], wg], axis=1)
    gb = jnp.pad(gate_b[None, :], ((0, 0), (0, LANES - 2 * A_HEADS)))
    return w_all, gb


def _pick(n, cap):
    b = min(n, cap)
    while n % b:
        b //= 2
    return b


def kernel(x, mem, norm_mix_g, norm_mem_g, norm_ffn_g, norm_out_g, w_mem_kv, a_w_in, a_gate_b, a_conv_w, a_conv_b, a_head_g, a_w_out, b_w_in, b_lb_logits, b_head_g, b_w_out, ffn_w_up, ffn_conv_w, ffn_conv_b, ffn_w_down):
    batch, seq, d = x.shape
    mem_len = mem.shape[1]
    depth = norm_mix_g.shape[0]
    t = batch * seq
    tm_proj = _pick(seq, 256)
    tm_out = _pick(seq, 512)
    tm_ffn = _pick(seq, 1024)
    tb = _pick(seq, 512)

    h = x.reshape(t, d)
    mem2 = mem.reshape(batch * mem_len, d)
    for layer in range(depth):
        j = layer // N_MIXERS
        kv = _norm_matmul(mem2, _gain_folded(w_mem_kv[layer], norm_mem_g[layer]), _pick(mem_len, 256))
        if layer % N_MIXERS == 0:
            w_all, gb = _mlstm_weights(_gain_folded(a_w_in[j], norm_mix_g[layer]), a_gate_b[j])
            zq, zk, zv, zo, zxq, zg = _proj_mlstm(h, w_all, a_conv_w[j], a_conv_b[j][None, :], seq, tm_proj)
            y_mix = _mlstm(zq, zk, zv, zo, zg, gb, a_head_g[j].reshape(1, MIX_WIDTH), batch, seq, tb,
                           A_HEADS_PER_STEP)
            w_out = a_w_out[j]
        else:
            zq, zlf, zkk, zv, zsg, zxq = _proj_hgrn2(h, _gain_folded(b_w_in[j], norm_mix_g[layer]), b_lb_logits,
                                                     tm_proj, layer)
            y_mix = _hgrn2(zq, zlf, zkk, zv, zsg, b_head_g[j].reshape(1, MIX_WIDTH), batch, seq, tb,
                           B_HEADS_PER_STEP)
            w_out = b_w_out[j]
        h, hn, rs = _outproj(y_mix, zxq, kv, w_out.astype(BF16), h, seq, mem_len, tm_out)
        h = _ffn(hn, rs, h, _gain_folded(ffn_w_up[layer], norm_ffn_g[layer]), ffn_conv_w[layer], ffn_conv_b[layer],
                 ffn_w_down[layer].astype(BF16), seq, tm_ffn, 512, 512)
    return _final_norm(h, norm_out_g, tm_out).reshape(batch, seq, d)
```

```python
import functools

import numpy as np
import jax
import jax.numpy as jnp
from jax import lax
from jax.experimental import pallas as pl
from jax.experimental.pallas import tpu as pltpu

F32 = jnp.float32
BF16 = jnp.bfloat16

D_MODEL = 2048
N_MIXERS = 2
CHUNK = 64
EPS = 1e-6

XA_HEADS = 4
XA_WIDTH = D_MODEL // 4
XA_HEAD_DIM = XA_WIDTH // XA_HEADS
MIX_WIDTH = D_MODEL - XA_WIDTH

A_HEADS = 4
A_V_DIM = MIX_WIDTH // A_HEADS
A_QK_DIM = A_V_DIM // 2
A_QK_PAD = 256
A_CONV = 4
A_AUG = 128
A_HEADS_PER_STEP = 4
A_CHUNK = 128
B_HEADS = 12
B_K_DIM = 128
B_V_DIM = 128
B_LEVELS = 6
B_HEADS_PER_STEP = 12
B_CHUNKS_PER_ITER = 2

D_FF = 5632
FFN_CONV = 3

LANES = 128
SUBLANES = 8
VMEM_LIMIT_BYTES = 56 * 1024 * 1024

_NN = (((1,), (0,)), ((), ()))
_NT = (((1,), (1,)), ((), ()))
_TN = (((0,), (0,)), ((), ()))


def _dot(a, b, dims=_NN):
    return lax.dot_general(a, b, dims, preferred_element_type=F32)


def _split2(x):
    hi = x.astype(BF16)
    return hi, (x - hi.astype(F32)).astype(BF16)


def _dot_exact_lhs(a_bf, x, dims=_NN):
    return sum(_dot(a_bf, p, dims) for p in _split2(x))


def _dot_exact_rhs(x, b_bf, dims=_NN):
    return sum(_dot(p, b_bf, dims) for p in _split2(x))


def _sigmoid(x):
    return 0.5 + 0.5 * jnp.tanh(0.5 * x)


def _silu(x):
    hx = 0.5 * x
    return hx + hx * jnp.tanh(hx)


def _log_sigmoid(x):
    return jnp.minimum(x, 0.0) - jnp.log(1.0 + jnp.exp(-jnp.abs(x)))


def _params(n_grid):
    return pltpu.CompilerParams(dimension_semantics=("arbitrary",) * n_grid,
                                vmem_limit_bytes=VMEM_LIMIT_BYTES)


def _resident(shape):
    nd = len(shape)
    return pl.BlockSpec(shape, lambda *_: (0,) * nd, pipeline_mode=pl.Buffered(1))


def _gain_folded(w, g):
    return (g[:, None] * w).astype(BF16)


def _rms_split(x_ref):
    x = x_ref[...]
    return x.astype(BF16), lax.rsqrt(jnp.mean(x * x, axis=-1, keepdims=True) + EPS)


def _norm_matmul_kernel(x_ref, w_ref, out_ref):
    x_bf, rs = _rms_split(x_ref)
    out_ref[...] = _dot(x_bf, w_ref[...]) * rs


def _norm_matmul(x, wg_bf, tm):
    t, d = x.shape
    n = wg_bf.shape[1]
    return pl.pallas_call(
        _norm_matmul_kernel,
        grid=(t // tm,),
        in_specs=[pl.BlockSpec((tm, d), lambda i: (i, 0)),
                  _resident((d, n))],
        out_specs=pl.BlockSpec((tm, n), lambda i: (i, 0)),
        out_shape=jax.ShapeDtypeStruct((t, n), F32),
        compiler_params=_params(1),
        name="norm_matmul",
    )(x, wg_bf)


def _proj_mlstm_kernel(x_ref, w_ref, wx_ref, wg_ref, cw_ref, cb_ref, q_out, k_out, v_out, o_out, xq_out, gt_out,
                       cbuf, *, tm, per_seq):
    i = pl.program_id(0)
    qk_w = 2 * A_HEADS * A_QK_DIM

    @pl.when(i % per_seq == 0)
    def _():
        cbuf[0:SUBLANES] = jnp.zeros((SUBLANES, qk_w), F32)

    x_bf, rs = _rms_split(x_ref)

    def proj(off, width):
        return _dot(x_bf, w_ref[:, off:off + width]) * rs

    cbuf[SUBLANES:SUBLANES + tm] = proj(0, qk_w)
    acc = cb_ref[...]
    for j in range(A_CONV):
        lo = SUBLANES - (A_CONV - 1) + j
        acc = acc + cw_ref[j:j + 1, :] * cbuf[lo:lo + tm, :]
    y = _silu(acc)
    cbuf[0:SUBLANES] = cbuf[tm:tm + SUBLANES]
    pad = jnp.zeros((tm, A_QK_PAD - A_QK_DIM), F32)
    for hh in range(A_HEADS):
        qh = y[:, hh * A_QK_DIM:(hh + 1) * A_QK_DIM]
        kh = y[:, (A_HEADS + hh) * A_QK_DIM:(A_HEADS + hh + 1) * A_QK_DIM] * (A_QK_DIM ** -0.5)
        q_out[:, hh * A_QK_PAD:(hh + 1) * A_QK_PAD] = jnp.concatenate([qh, pad], axis=1).astype(BF16)
        k_out[:, hh * A_QK_PAD:(hh + 1) * A_QK_PAD] = jnp.concatenate([kh, pad], axis=1).astype(BF16)
    off = qk_w
    v_out[...] = proj(off, MIX_WIDTH).astype(BF16)
    off += MIX_WIDTH
    o_out[...] = _sigmoid(proj(off, MIX_WIDTH))
    xq_out[...] = (_dot(x_bf, wx_ref[...]) * rs).astype(BF16)
    gt_out[...] = _dot(x_bf, wg_ref[...]) * rs


def _proj_mlstm(x, w_bf, conv_w, conv_b, seq, tm):
    t, d = x.shape
    qk_w = 2 * A_HEADS * A_QK_DIM
    g0 = qk_w + 2 * MIX_WIDTH
    n_gates = 2 * A_HEADS
    assert w_bf.shape[1] == g0 + n_gates + XA_WIDTH
    w_g = jnp.pad(w_bf[:, g0:g0 + n_gates], ((0, 0), (0, LANES - n_gates)))
    w_x = w_bf[:, g0 + n_gates:]
    widths = (A_HEADS * A_QK_PAD, A_HEADS * A_QK_PAD, MIX_WIDTH, MIX_WIDTH, XA_WIDTH, LANES)
    dtypes = (BF16, BF16, BF16, F32, BF16, F32)
    return pl.pallas_call(
        functools.partial(_proj_mlstm_kernel, tm=tm, per_seq=seq // tm),
        grid=(t // tm,),
        in_specs=[pl.BlockSpec((tm, d), lambda i: (i, 0)),
                  _resident((d, g0)),
                  _resident((d, XA_WIDTH)),
                  _resident((d, LANES)),
                  _resident((A_CONV, qk_w)),
                  _resident((1, qk_w))],
        out_specs=[pl.BlockSpec((tm, s), lambda i: (i, 0)) for s in widths],
        out_shape=[jax.ShapeDtypeStruct((t, s), dt) for s, dt in zip(widths, dtypes)],
        scratch_shapes=[pltpu.VMEM((tm + SUBLANES, qk_w), F32)],
        compiler_params=_params(1),
        name="proj_mlstm",
    )(x, w_bf, w_x, w_g, conv_w, conv_b)


def _proj_hgrn2_kernel(x_ref, w_ref, lbl_ref, q_out, lf_out, kk_out, v_out, sg_out, xq_out, *, layer):
    lg = lbl_ref[...]
    lg = lg - jnp.max(lg, axis=0, keepdims=True)
    pe = jnp.exp(lg)
    pr = pe / jnp.sum(pe, axis=0, keepdims=True)
    c0 = pr[0:1, :]
    cl = c0
    for r in range(1, layer + 1):
        cl = cl + pr[r:r + 1, :]
    lb = cl - c0
    om = 1.0 - lb

    x_bf, rs = _rms_split(x_ref)
    kw = B_HEADS * B_K_DIM

    def proj(off, width):
        return _dot(x_bf, w_ref[:, off:off + width]) * rs

    q_out[...] = _silu(proj(0, kw))
    th = 0.5 * jnp.tanh(0.5 * proj(kw, kw))
    lf_out[...] = jnp.log2(lb + om * (0.5 + th))
    kk_out[...] = om * (0.5 - th)
    off = 2 * kw
    v_out[...] = proj(off, MIX_WIDTH).astype(BF16)
    off += MIX_WIDTH
    sg_out[...] = _silu(proj(off, MIX_WIDTH))
    off += MIX_WIDTH
    xq_out[...] = proj(off, XA_WIDTH).astype(BF16)


def _proj_hgrn2(x, w_bf, lb_logits, tm, layer):
    t, d = x.shape
    n = w_bf.shape[1]
    kw = B_HEADS * B_K_DIM
    widths = (kw, kw, kw, MIX_WIDTH, MIX_WIDTH, XA_WIDTH)
    dtypes = (F32, F32, F32, BF16, F32, BF16)
    assert 2 * kw + 2 * MIX_WIDTH + XA_WIDTH == n
    return pl.pallas_call(
        functools.partial(_proj_hgrn2_kernel, layer=layer),
        grid=(t // tm,),
        in_specs=[pl.BlockSpec((tm, d), lambda i: (i, 0)),
                  _resident((d, n)),
                  _resident(lb_logits.shape)],
        out_specs=[pl.BlockSpec((tm, s), lambda i: (i, 0)) for s in widths],
        out_shape=[jax.ShapeDtypeStruct((t, s), dt) for s, dt in zip(widths, dtypes)],
        compiler_params=_params(1),
        name="proj_hgrn2",
    )(x, w_bf, lb_logits)


def _mlstm_kernel(q_ref, k_ref, v_ref, o_ref, gt_ref, gb_ref, hg_ref, tril_ref, out_ref, c_st, m_st, *, tb, hb):
    hgrp = pl.program_id(1)
    t = pl.program_id(2)
    aug_w = A_V_DIM + A_AUG
    rep = aug_w // LANES

    @pl.when(t == 0)
    def _():
        c_st[...] = jnp.zeros_like(c_st)
        m_st[...] = jnp.zeros_like(m_st)

    tril = tril_ref[...]
    row = lax.broadcasted_iota(jnp.int32, (A_CHUNK, A_CHUNK), 0)
    col = lax.broadcasted_iota(jnp.int32, (A_CHUNK, A_CHUNK), 1)
    causal = row >= col
    wide = hb * LANES
    srow = lax.broadcasted_iota(jnp.int32, (LANES, 2 * wide), 0)
    scol = lax.broadcasted_iota(jnp.int32, (LANES, 2 * wide), 1)
    blk = scol // LANES
    want = jnp.where(blk < hb, hgrp * hb + blk, A_HEADS + hgrp * hb + (blk - hb))
    sel = jnp.where(srow == want, 1.0, 0.0).astype(BF16)
    frow = lax.broadcasted_iota(jnp.int32, (A_CHUNK, LANES), 1)
    first = jnp.where(frow == 0, 1.0, 0.0).astype(BF16)
    ones_aug = jnp.ones((A_CHUNK, A_AUG), BF16)
    heads = range(hb)

    def lanes_of(hh):
        return slice(hh * LANES, (hh + 1) * LANES)

    def chunk(c, carry):
        rows = pl.ds(pl.multiple_of(c * A_CHUNK, A_CHUNK), A_CHUNK)
        gates = gt_ref[rows, :] + gb_ref[...]
        g2 = _dot_exact_rhs(gates, sel)
        ic = g2[:, :wide]
        lf = _log_sigmoid(g2[:, wide:])
        gc = _dot_exact_lhs(tril, lf)
        m_prev = m_st[0:1, :]
        a = gc + m_prev
        icg = ic - gc
        r_mat = [_dot_exact_lhs(first, icg[:, lanes_of(hh)], _NT) for hh in heads]
        dmat = [jnp.where(causal, gc[:, hh * LANES:hh * LANES + A_CHUNK] + r_mat[hh], -jnp.inf) for hh in heads]
        mx = jnp.concatenate([jnp.broadcast_to(jnp.max(dmat[hh], axis=1, keepdims=True), (A_CHUNK, LANES))
                              for hh in heads], axis=1)
        m_row = jnp.maximum(a, mx)
        w_inter = jnp.exp(a - m_row)
        e_neg = jnp.exp(-m_row)
        p = [jnp.exp(dmat[hh] - m_row[:, hh * LANES:hh * LANES + A_CHUNK]) for hh in heads]

        g_end = gc[A_CHUNK - 1:A_CHUNK, :]
        a_end = g_end + m_prev
        w_end = g_end - gc + ic
        m_new = jnp.maximum(a_end, jnp.max(w_end, axis=0, keepdims=True))
        decay = jnp.exp(a_end - m_new)
        ws = jnp.exp(w_end - m_new).astype(BF16)
        m_st[...] = jnp.broadcast_to(m_new, (SUBLANES, wide))

        qk_lanes = [slice(hh * A_QK_PAD, (hh + 1) * A_QK_PAD) for hh in heads]
        v_lanes = [slice(hh * A_V_DIM, (hh + 1) * A_V_DIM) for hh in heads]
        q = [q_ref[rows, qk_lanes[hh]] for hh in heads]
        k = [k_ref[rows, qk_lanes[hh]] for hh in heads]
        v_aug = [jnp.concatenate([v_ref[rows, v_lanes[hh]], ones_aug], axis=1) for hh in heads]
        qk = [_dot(q[hh], k[hh], _NT) for hh in heads]
        c_old = [c_st[hh] for hh in heads]
        inter = [_dot(q[hh], c_old[hh].astype(BF16)) for hh in heads]
        intra = [_dot((qk[hh] * p[hh]).astype(BF16), v_aug[hh]) for hh in heads]
        for hh in heads:
            kw = k[hh] * jnp.concatenate([ws[:, lanes_of(hh)]] * (A_QK_PAD // LANES), axis=1)
            c_st[hh] = (jnp.concatenate([decay[:, lanes_of(hh)]] * rep, axis=1) * c_old[hh]
                        + _dot(kw, v_aug[hh], _TN))
        for hh in heads:
            tot = jnp.concatenate([w_inter[:, lanes_of(hh)]] * rep, axis=1) * inter[hh] + intra[hh]
            den = tot[:, A_V_DIM:]
            inv = 1.0 / jnp.maximum(jnp.abs(den), e_neg[:, lanes_of(hh)])
            h_out = tot[:, :A_V_DIM] * jnp.concatenate([inv] * (A_V_DIM // LANES), axis=1)
            ms = jnp.mean(h_out * h_out, axis=-1, keepdims=True)
            hn = h_out * lax.rsqrt(ms + EPS) * hg_ref[:, v_lanes[hh]]
            out_ref[rows, v_lanes[hh]] = (o_ref[rows, v_lanes[hh]] * hn).astype(out_ref.dtype)
        return carry

    lax.fori_loop(0, tb // A_CHUNK, chunk, 0)


def _mlstm(zq, zk, zv, zo, zg, gate_b, head_g, batch, seq, tb, hb):
    t_total = batch * seq
    nt = seq // tb
    tril = jnp.asarray(np.tril(np.ones((A_CHUNK, A_CHUNK), np.float32)), BF16)
    row_map = lambda b, h, t: (b * nt + t, h)
    head_map = lambda b, h, t: (0, h)
    fixed = lambda b, h, t: (0, 0)
    return pl.pallas_call(
        functools.partial(_mlstm_kernel, tb=tb, hb=hb),
        grid=(batch, A_HEADS // hb, nt),
        in_specs=[pl.BlockSpec((tb, hb * A_QK_PAD), row_map),
                  pl.BlockSpec((tb, hb * A_QK_PAD), row_map),
                  pl.BlockSpec((tb, hb * A_V_DIM), row_map),
                  pl.BlockSpec((tb, hb * A_V_DIM), row_map),
                  pl.BlockSpec((tb, LANES), lambda b, h, t: (b * nt + t, 0)),
                  pl.BlockSpec((1, LANES), fixed),
                  pl.BlockSpec((1, hb * A_V_DIM), head_map),
                  pl.BlockSpec((A_CHUNK, A_CHUNK), fixed)],
        out_specs=pl.BlockSpec((tb, hb * A_V_DIM), row_map),
        out_shape=jax.ShapeDtypeStruct((t_total, MIX_WIDTH), BF16),
        scratch_shapes=[pltpu.VMEM((hb, A_QK_PAD, A_V_DIM + A_AUG), F32),
                        pltpu.VMEM((SUBLANES, hb * LANES), F32)],
        compiler_params=_params(3),
        name="mlstm",
    )(zq, zk, zv, zo, zg, gate_b, head_g, tril)


def _hgrn2_pair_masks():
    n = CHUNK
    masks = [np.eye(n, dtype=np.float32)]
    for j in range(1, B_LEVELS + 1):
        c = n >> j
        pm = np.zeros((n, n), np.float32)
        for r in range(n):
            if r % (2 * c) >= c:
                mid = (r // (2 * c)) * 2 * c + c
                pm[r, mid - c:mid] = 1.0
        masks.append(pm)
    return np.stack(masks, axis=0)


def _hgrn2_kernel(q_ref, lf_ref, kk_ref, v_ref, sg_ref, hg_ref, tril_ref, mask_ref, out_ref, s_st, *, tb, hb):
    t = pl.program_id(2)

    @pl.when(t == 0)
    def _():
        s_st[...] = jnp.zeros_like(s_st)

    tril = tril_ref[...]
    wide = hb * B_K_DIM
    heads = range(hb)
    rowi = lax.broadcasted_iota(jnp.int32, (CHUNK, wide), 0)
    sub = lax.broadcasted_iota(jnp.int32, (CHUNK // SUBLANES, SUBLANES, wide), 1)

    def level_operand(q, kk, gc, lf, half):
        if half >= SUBLANES:
            pieces = []
            for b0 in range(0, CHUNK, 2 * half):
                mid = b0 + half
                ref = gc[mid - 1:mid, :]
                pieces.append(kk[b0:mid, :] * jnp.exp2(ref - gc[b0:mid, :]))
                pieces.append(q[mid:mid + half, :] * jnp.exp2(gc[mid:mid + half, :] - ref))
            return jnp.concatenate(pieces, axis=0)
        if half == 1:
            odd = (rowi & 1) != 0
            return jnp.where(odd, q * jnp.exp2(lf), kk)
        g3 = gc.reshape(CHUNK // SUBLANES, SUBLANES, wide)
        ref = g3[:, half - 1:half, :]
        for b0 in range(2 * half, SUBLANES, 2 * half):
            ref = jnp.where(sub >= b0, g3[:, b0 + half - 1:b0 + half, :], ref)
        d = g3 - ref
        upper = (sub & half) != 0
        ex = jnp.exp2(jnp.where(upper, d, -d)).reshape(CHUNK, wide)
        return jnp.where((rowi & half) != 0, q, kk) * ex

    def hl(x, hh):
        return x[:, hh * B_K_DIM:(hh + 1) * B_K_DIM]

    def chunk_group(c, carry):
        group = range(B_CHUNKS_PER_ITER)
        rows = [pl.ds(pl.multiple_of((c * B_CHUNKS_PER_ITER + i) * CHUNK, CHUNK), CHUNK) for i in group]
        q = [q_ref[r, :] for r in rows]
        lf = [lf_ref[r, :] for r in rows]
        kk = [kk_ref[r, :] for r in rows]
        v = [v_ref[r, :] for r in rows]
        gc = [_dot_exact_lhs(tril, lf[i]) for i in group]
        g_end = [gc[i][CHUNK - 1:CHUNK, :] for i in group]

        xs = [[level_operand(q[i], kk[i], gc[i], lf[i], CHUNK >> j).astype(BF16) for j in range(1, B_LEVELS + 1)]
              for i in group]
        q_bf = [q[i].astype(BF16) for i in group]
        k_bf = [kk[i].astype(BF16) for i in group]
        qg = [(q[i] * jnp.exp2(gc[i])).astype(BF16) for i in group]
        kd = [(kk[i] * jnp.exp2(g_end[i] - gc[i])).astype(BF16) for i in group]
        dec = [jnp.exp2(g_end[i]) for i in group]

        prod = [[[_dot(hl(q_bf[i], hh), hl(k_bf[i], hh), _NT)] + [_dot(hl(x, hh), hl(x, hh), _NT) for x in xs[i]]
                 for hh in heads] for i in group]
        upd = [[_dot(hl(v[i], hh), hl(kd[i], hh), _TN) for hh in heads] for i in group]
        state = [s_st[hh] for hh in heads]
        o_inter = []
        for i in group:
            o_inter.append([_dot(hl(qg[i], hh), state[hh].astype(BF16), _NT) for hh in heads])
            state = [state[hh] * hl(dec[i], hh) + upd[i][hh] for hh in heads]
        for hh in heads:
            s_st[hh] = state[hh]
        for i in group:
            outs = []
            for hh in heads:
                acc = mask_ref[0] * prod[i][hh][0]
                for j in range(1, B_LEVELS + 1):
                    acc = acc + mask_ref[j] * prod[i][hh][j]
                o = _dot(acc.astype(BF16), hl(v[i], hh)) + o_inter[i][hh]
                ms = jnp.mean(o * o, axis=-1, keepdims=True)
                outs.append(o * lax.rsqrt(ms + EPS))
            on = jnp.concatenate(outs, axis=1) * hg_ref[...]
            out_ref[rows[i], :] = (on * sg_ref[rows[i], :]).astype(out_ref.dtype)
        return carry

    lax.fori_loop(0, tb // (CHUNK * B_CHUNKS_PER_ITER), chunk_group, 0)


def _hgrn2(zq, zlf, zkk, zv, zsg, head_g, batch, seq, tb, hb):
    t_total = batch * seq
    nt = seq // tb
    mask_np = _hgrn2_pair_masks()
    tril = jnp.asarray(np.tril(np.ones((CHUNK, CHUNK), np.float32)), BF16)
    row_map = lambda b, h, t: (b * nt + t, h)
    head_map = lambda b, h, t: (0, h)
    return pl.pallas_call(
        functools.partial(_hgrn2_kernel, tb=tb, hb=hb),
        grid=(batch, B_HEADS // hb, nt),
        in_specs=[pl.BlockSpec((tb, hb * B_K_DIM), row_map),
                  pl.BlockSpec((tb, hb * B_K_DIM), row_map),
                  pl.BlockSpec((tb, hb * B_K_DIM), row_map),
                  pl.BlockSpec((tb, hb * B_V_DIM), row_map),
                  pl.BlockSpec((tb, hb * B_V_DIM), row_map),
                  pl.BlockSpec((1, hb * B_V_DIM), head_map),
                  pl.BlockSpec((CHUNK, CHUNK), lambda b, h, t: (0, 0)),
                  pl.BlockSpec(mask_np.shape, lambda b, h, t: (0, 0, 0))],
        out_specs=pl.BlockSpec((tb, hb * B_V_DIM), row_map),
        out_shape=jax.ShapeDtypeStruct((t_total, MIX_WIDTH), BF16),
        scratch_shapes=[pltpu.VMEM((hb, B_V_DIM, B_K_DIM), F32)],
        compiler_params=_params(3),
        name="hgrn2",
    )(zq, zlf, zkk, zv, zsg, head_g, tril, jnp.asarray(mask_np, F32))


def _outproj_kernel(y_ref, xq_ref, kv_ref, wo_ref, h_ref, out_ref, hn_ref, rs_ref):
    kv = kv_ref[...]
    parts = []
    for hh in range(XA_HEADS):
        lo = hh * XA_HEAD_DIM
        qh = xq_ref[:, lo:lo + XA_HEAD_DIM]
        kh = kv[:, lo:lo + XA_HEAD_DIM].astype(BF16)
        vh = kv[:, XA_WIDTH + lo:XA_WIDTH + lo + XA_HEAD_DIM].astype(BF16)
        s = _dot(qh, kh, _NT) * (XA_HEAD_DIM ** -0.5)
        e = jnp.exp(s - jnp.max(s, axis=-1, keepdims=True))
        den = jnp.sum(e, axis=-1, keepdims=True)
        parts.append((_dot(e.astype(BF16), vh) / den).astype(BF16))
    y_mem = jnp.concatenate(parts, axis=1)
    h_new = (h_ref[...] + _dot(y_ref[...], wo_ref[0:MIX_WIDTH, :])
             + _dot(y_mem, wo_ref[MIX_WIDTH:D_MODEL, :]))
    out_ref[...] = h_new
    hn_ref[...] = h_new.astype(BF16)
    rs = lax.rsqrt(jnp.mean(h_new * h_new, axis=-1, keepdims=True) + EPS)
    rs_ref[...] = jnp.broadcast_to(rs, rs_ref.shape)


def _outproj(y_mix, zxq, kv_all, layer, w_out_bf, h, seq, mem_len, tm):
    t = h.shape[0]
    per_seq = seq // tm
    return pl.pallas_call(
        _outproj_kernel,
        grid=(t // tm,),
        in_specs=[pl.BlockSpec((tm, MIX_WIDTH), lambda i: (i, 0)),
                  pl.BlockSpec((tm, XA_WIDTH), lambda i: (i, 0)),
                  pl.BlockSpec((mem_len, 2 * XA_WIDTH), lambda i: (i // per_seq, layer)),
                  _resident((D_MODEL, D_MODEL)),
                  pl.BlockSpec((tm, D_MODEL), lambda i: (i, 0))],
        out_specs=[pl.BlockSpec((tm, D_MODEL), lambda i: (i, 0)),
                   pl.BlockSpec((tm, D_MODEL), lambda i: (i, 0)),
                   pl.BlockSpec((tm, LANES), lambda i: (i, 0))],
        out_shape=[jax.ShapeDtypeStruct((t, D_MODEL), F32),
                   jax.ShapeDtypeStruct((t, D_MODEL), BF16),
                   jax.ShapeDtypeStruct((t, LANES), F32)],
        compiler_params=_params(1),
        name="outproj",
    )(y_mix, zxq, kv_all, w_out_bf, h)


def _ffn_kernel(hn_ref, rs_ref, h_ref, wu_ref, wg_ref, cw_ref, cb_ref, wd_ref, out_ref,
                gbuf, gcarry, act_s, *, tm, nf, per_seq):
    i = pl.program_id(0)
    j = pl.program_id(1)
    tf = act_s.shape[2]

    @pl.when(j < nf)
    def _():
        hn = hn_ref[...]
        rs = jnp.concatenate([rs_ref[...]] * (tf // LANES), axis=1)
        g = _dot(hn, wg_ref[...]) * rs
        prev = jnp.where(i % per_seq == 0, 0.0, gcarry[j])
        gbuf[0:SUBLANES] = prev
        gbuf[SUBLANES:SUBLANES + tm] = g
        gcarry[j] = g[tm - SUBLANES:tm, :]
        gc = cb_ref[...]
        for k in range(FFN_CONV):
            lo = SUBLANES - (FFN_CONV - 1) + k
            gc = gc + cw_ref[k:k + 1, :] * gbuf[lo:lo + tm, :]
        sg = _silu(gc) * rs
        u = _dot(hn, wu_ref[...])
        act_s[j] = (sg * u).astype(BF16)

    @pl.when(j >= nf)
    def _():
        act = jnp.concatenate([act_s[f] for f in range(nf)], axis=1)
        out_ref[...] = h_ref[...] + _dot(act, wd_ref[...])


def _ffn(hn, rs, h, w_up_bf, conv_w, conv_b, w_down_bf, seq, tm, tf, tn):
    t, d = h.shape
    nf = D_FF // tf
    nd = d // tn
    per_seq = seq // tm
    up = lambda j: jnp.minimum(j, nf - 1)
    down = lambda j: jnp.maximum(j - nf, 0)
    return pl.pallas_call(
        functools.partial(_ffn_kernel, tm=tm, nf=nf, per_seq=per_seq),
        grid=(t // tm, nf + nd),
        in_specs=[pl.BlockSpec((tm, d), lambda i, j: (i, 0)),
                  pl.BlockSpec((tm, LANES), lambda i, j: (i, 0)),
                  pl.BlockSpec((tm, tn), lambda i, j: (i, down(j))),
                  pl.BlockSpec((d, tf), lambda i, j: (0, up(j))),
                  pl.BlockSpec((d, tf), lambda i, j: (0, nf + up(j))),
                  pl.BlockSpec((FFN_CONV, tf), lambda i, j: (0, up(j))),
                  pl.BlockSpec((1, tf), lambda i, j: (0, up(j))),
                  pl.BlockSpec((D_FF, tn), lambda i, j: (0, down(j)))],
        out_specs=pl.BlockSpec((tm, tn), lambda i, j: (i, down(j))),
        out_shape=jax.ShapeDtypeStruct((t, d), F32),
        scratch_shapes=[pltpu.VMEM((tm + SUBLANES, tf), F32),
                        pltpu.VMEM((nf, SUBLANES, tf), F32),
                        pltpu.VMEM((nf, tm, tf), BF16)],
        compiler_params=_params(2),
        name="ffn",
    )(hn, rs, h, w_up_bf, w_up_bf, conv_w, conv_b.reshape(1, D_FF), w_down_bf)


def _final_norm_kernel(x_ref, g_ref, out_ref):
    x = x_ref[...]
    out_ref[...] = x * lax.rsqrt(jnp.mean(x * x, axis=-1, keepdims=True) + EPS) * g_ref[...]


def _final_norm(x, g, tm):
    t, d = x.shape
    return pl.pallas_call(
        _final_norm_kernel,
        grid=(t // tm,),
        in_specs=[pl.BlockSpec((tm, d), lambda i: (i, 0)),
                  pl.BlockSpec((1, d), lambda i: (0, 0))],
        out_specs=pl.BlockSpec((tm, d), lambda i: (i, 0)),
        out_shape=jax.ShapeDtypeStruct((t, d), F32),
        compiler_params=_params(1),
        name="final_norm",
    )(x, g.reshape(1, d))


def _pick(n, cap):
    b = min(n, cap)
    while n % b:
        b //= 2
    return b


def kernel(x, mem, norm_mix_g, norm_mem_g, norm_ffn_g, norm_out_g, w_mem_kv, a_w_in, a_gate_b, a_conv_w, a_conv_b, a_head_g, a_w_out, b_w_in, b_lb_logits, b_head_g, b_w_out, ffn_w_up, ffn_conv_w, ffn_conv_b, ffn_w_down):
    batch, seq, d = x.shape
    mem_len = mem.shape[1]
    depth = norm_mix_g.shape[0]
    t = batch * seq
    tm_proj = _pick(seq, 256)
    tm_out = _pick(seq, 512)
    tm_ffn = _pick(seq, 1024)
    tb = _pick(seq, 512)

    h = x.reshape(t, d)
    w_kv_all = jnp.concatenate([_gain_folded(w_mem_kv[layer], norm_mem_g[layer]) for layer in range(depth)], axis=1)
    kv_all = _norm_matmul(mem.reshape(batch * mem_len, d), w_kv_all, _pick(mem_len, 256))
    for layer in range(depth):
        j = layer // N_MIXERS
        if layer % N_MIXERS == 0:
            zq, zk, zv, zo, zxq, zg = _proj_mlstm(h, _gain_folded(a_w_in[j], norm_mix_g[layer]), a_conv_w[j],
                                                  a_conv_b[j][None, :], seq, tm_proj)
            gb = jnp.pad(a_gate_b[j][None, :], ((0, 0), (0, LANES - 2 * A_HEADS)))
            y_mix = _mlstm(zq, zk, zv, zo, zg, gb, a_head_g[j].reshape(1, MIX_WIDTH), batch, seq, tb,
                           A_HEADS_PER_STEP)
            w_out = a_w_out[j]
        else:
            zq, zlf, zkk, zv, zsg, zxq = _proj_hgrn2(h, _gain_folded(b_w_in[j], norm_mix_g[layer]), b_lb_logits,
                                                     tm_proj, layer)
            y_mix = _hgrn2(zq, zlf, zkk, zv, zsg, b_head_g[j].reshape(1, MIX_WIDTH), batch, seq, tb,
                           B_HEADS_PER_STEP)
            w_out = b_w_out[j]
        h, hn, rs = _outproj(y_mix, zxq, kv_all, layer, w_out.astype(BF16), h, seq, mem_len, tm_out)
        h = _ffn(hn, rs, h, _gain_folded(ffn_w_up[layer], norm_ffn_g[layer]), ffn_conv_w[layer], ffn_conv_b[layer],
                 ffn_w_down[layer].astype(BF16), seq, tm_ffn, 512, 512)
    return _final_norm(h, norm_out_g, tm_out).reshape(batch, seq, d)
```

```python
import functools

import numpy as np
import jax
import jax.numpy as jnp
from jax import lax
from jax.experimental import pallas as pl
from jax.experimental.pallas import tpu as pltpu

F32 = jnp.float32
BF16 = jnp.bfloat16

D_MODEL = 2048
N_MIXERS = 2
CHUNK = 64
EPS = 1e-6

XA_HEADS = 4
XA_WIDTH = D_MODEL // 4
XA_HEAD_DIM = XA_WIDTH // XA_HEADS
MIX_WIDTH = D_MODEL - XA_WIDTH

A_HEADS = 4
A_V_DIM = MIX_WIDTH // A_HEADS
A_QK_DIM = A_V_DIM // 2
A_QK_PAD = 256
A_CONV = 4
A_AUG = 128
A_HEADS_PER_STEP = 4
A_CHUNK = 128
A_CHUNKS_PER_ITER = 2
B_HEADS = 12
B_K_DIM = 128
B_V_DIM = 128
B_LEVELS = 6
B_HEADS_PER_STEP = 12
B_CHUNKS_PER_ITER = 2

D_FF = 5632
FFN_CONV = 3

LANES = 128
SUBLANES = 8
VMEM_LIMIT_BYTES = 56 * 1024 * 1024

_NN = (((1,), (0,)), ((), ()))
_NT = (((1,), (1,)), ((), ()))
_TN = (((0,), (0,)), ((), ()))


def _dot(a, b, dims=_NN):
    return lax.dot_general(a, b, dims, preferred_element_type=F32)


def _split2(x):
    hi = x.astype(BF16)
    return hi, (x - hi.astype(F32)).astype(BF16)


def _dot_exact_lhs(a_bf, x, dims=_NN):
    return sum(_dot(a_bf, p, dims) for p in _split2(x))


def _dot_exact_rhs(x, b_bf, dims=_NN):
    return sum(_dot(p, b_bf, dims) for p in _split2(x))


def _sigmoid(x):
    return 0.5 + 0.5 * jnp.tanh(0.5 * x)


def _silu(x):
    hx = 0.5 * x
    return hx + hx * jnp.tanh(hx)


def _log_sigmoid(x):
    return jnp.minimum(x, 0.0) - jnp.log(1.0 + jnp.exp(-jnp.abs(x)))


def _params(n_grid):
    return pltpu.CompilerParams(dimension_semantics=("arbitrary",) * n_grid,
                                vmem_limit_bytes=VMEM_LIMIT_BYTES)


def _resident(shape):
    nd = len(shape)
    return pl.BlockSpec(shape, lambda *_: (0,) * nd, pipeline_mode=pl.Buffered(1))


def _gain_folded(w, g):
    return (g[:, None] * w).astype(BF16)


def _rms_split(x_ref):
    x = x_ref[...]
    return x.astype(BF16), lax.rsqrt(jnp.mean(x * x, axis=-1, keepdims=True) + EPS)


def _norm_matmul_kernel(x_ref, w_ref, out_ref):
    x_bf, rs = _rms_split(x_ref)
    out_ref[...] = _dot(x_bf, w_ref[...]) * rs


def _norm_matmul(x, wg_bf, tm):
    t, d = x.shape
    n = wg_bf.shape[1]
    return pl.pallas_call(
        _norm_matmul_kernel,
        grid=(t // tm,),
        in_specs=[pl.BlockSpec((tm, d), lambda i: (i, 0)),
                  _resident((d, n))],
        out_specs=pl.BlockSpec((tm, n), lambda i: (i, 0)),
        out_shape=jax.ShapeDtypeStruct((t, n), F32),
        compiler_params=_params(1),
        name="norm_matmul",
    )(x, wg_bf)


def _proj_mlstm_kernel(x_ref, w_ref, wx_ref, wg_ref, cw_ref, cb_ref, q_out, k_out, v_out, o_out, xq_out, gt_out,
                       cbuf, *, tm, per_seq):
    i = pl.program_id(0)
    qk_w = 2 * A_HEADS * A_QK_DIM

    @pl.when(i % per_seq == 0)
    def _():
        cbuf[0:SUBLANES] = jnp.zeros((SUBLANES, qk_w), F32)

    x_bf, rs = _rms_split(x_ref)

    def proj(off, width):
        return _dot(x_bf, w_ref[:, off:off + width]) * rs

    cbuf[SUBLANES:SUBLANES + tm] = proj(0, qk_w)
    acc = cb_ref[...]
    for j in range(A_CONV):
        lo = SUBLANES - (A_CONV - 1) + j
        acc = acc + cw_ref[j:j + 1, :] * cbuf[lo:lo + tm, :]
    y = _silu(acc)
    cbuf[0:SUBLANES] = cbuf[tm:tm + SUBLANES]
    pad = jnp.zeros((tm, A_QK_PAD - A_QK_DIM), F32)
    for hh in range(A_HEADS):
        qh = y[:, hh * A_QK_DIM:(hh + 1) * A_QK_DIM]
        kh = y[:, (A_HEADS + hh) * A_QK_DIM:(A_HEADS + hh + 1) * A_QK_DIM] * (A_QK_DIM ** -0.5)
        q_out[:, hh * A_QK_PAD:(hh + 1) * A_QK_PAD] = jnp.concatenate([qh, pad], axis=1).astype(BF16)
        k_out[:, hh * A_QK_PAD:(hh + 1) * A_QK_PAD] = jnp.concatenate([kh, pad], axis=1).astype(BF16)
    off = qk_w
    v_out[...] = proj(off, MIX_WIDTH).astype(BF16)
    off += MIX_WIDTH
    o_out[...] = _sigmoid(proj(off, MIX_WIDTH))
    xq_out[...] = (_dot(x_bf, wx_ref[...]) * rs).astype(BF16)
    gt_out[...] = _dot(x_bf, wg_ref[...]) * rs


def _proj_mlstm(x, w_bf, conv_w, conv_b, seq, tm):
    t, d = x.shape
    qk_w = 2 * A_HEADS * A_QK_DIM
    g0 = qk_w + 2 * MIX_WIDTH
    n_gates = 2 * A_HEADS
    assert w_bf.shape[1] == g0 + n_gates + XA_WIDTH
    w_g = jnp.pad(w_bf[:, g0:g0 + n_gates], ((0, 0), (0, LANES - n_gates)))
    w_x = w_bf[:, g0 + n_gates:]
    widths = (A_HEADS * A_QK_PAD, A_HEADS * A_QK_PAD, MIX_WIDTH, MIX_WIDTH, XA_WIDTH, LANES)
    dtypes = (BF16, BF16, BF16, F32, BF16, F32)
    return pl.pallas_call(
        functools.partial(_proj_mlstm_kernel, tm=tm, per_seq=seq // tm),
        grid=(t // tm,),
        in_specs=[pl.BlockSpec((tm, d), lambda i: (i, 0)),
                  _resident((d, g0)),
                  _resident((d, XA_WIDTH)),
                  _resident((d, LANES)),
                  _resident((A_CONV, qk_w)),
                  _resident((1, qk_w))],
        out_specs=[pl.BlockSpec((tm, s), lambda i: (i, 0)) for s in widths],
        out_shape=[jax.ShapeDtypeStruct((t, s), dt) for s, dt in zip(widths, dtypes)],
        scratch_shapes=[pltpu.VMEM((tm + SUBLANES, qk_w), F32)],
        compiler_params=_params(1),
        name="proj_mlstm",
    )(x, w_bf, w_x, w_g, conv_w, conv_b)


def _proj_hgrn2_kernel(x_ref, w_ref, lbl_ref, q_out, lf_out, kk_out, v_out, sg_out, xq_out, *, layer):
    lg = lbl_ref[...]
    lg = lg - jnp.max(lg, axis=0, keepdims=True)
    pe = jnp.exp(lg)
    pr = pe / jnp.sum(pe, axis=0, keepdims=True)
    c0 = pr[0:1, :]
    cl = c0
    for r in range(1, layer + 1):
        cl = cl + pr[r:r + 1, :]
    lb = cl - c0
    om = 1.0 - lb

    x_bf, rs = _rms_split(x_ref)
    kw = B_HEADS * B_K_DIM

    def proj(off, width):
        return _dot(x_bf, w_ref[:, off:off + width]) * rs

    q_out[...] = _silu(proj(0, kw))
    th = 0.5 * jnp.tanh(0.5 * proj(kw, kw))
    lf_out[...] = jnp.log2(lb + om * (0.5 + th))
    kk_out[...] = om * (0.5 - th)
    off = 2 * kw
    v_out[...] = proj(off, MIX_WIDTH).astype(BF16)
    off += MIX_WIDTH
    sg_out[...] = _silu(proj(off, MIX_WIDTH))
    off += MIX_WIDTH
    xq_out[...] = proj(off, XA_WIDTH).astype(BF16)


def _proj_hgrn2(x, w_bf, lb_logits, tm, layer):
    t, d = x.shape
    n = w_bf.shape[1]
    kw = B_HEADS * B_K_DIM
    widths = (kw, kw, kw, MIX_WIDTH, MIX_WIDTH, XA_WIDTH)
    dtypes = (F32, F32, F32, BF16, F32, BF16)
    assert 2 * kw + 2 * MIX_WIDTH + XA_WIDTH == n
    return pl.pallas_call(
        functools.partial(_proj_hgrn2_kernel, layer=layer),
        grid=(t // tm,),
        in_specs=[pl.BlockSpec((tm, d), lambda i: (i, 0)),
                  _resident((d, n)),
                  _resident(lb_logits.shape)],
        out_specs=[pl.BlockSpec((tm, s), lambda i: (i, 0)) for s in widths],
        out_shape=[jax.ShapeDtypeStruct((t, s), dt) for s, dt in zip(widths, dtypes)],
        compiler_params=_params(1),
        name="proj_hgrn2",
    )(x, w_bf, lb_logits)


def _mlstm_kernel(q_ref, k_ref, v_ref, o_ref, gt_ref, gb_ref, hg_ref, tril_ref, out_ref, c_st, m_st, *, tb, hb):
    hgrp = pl.program_id(1)
    t = pl.program_id(2)
    aug_w = A_V_DIM + A_AUG
    rep = aug_w // LANES

    @pl.when(t == 0)
    def _():
        c_st[...] = jnp.zeros_like(c_st)
        m_st[...] = jnp.zeros_like(m_st)

    tril = tril_ref[...]
    row = lax.broadcasted_iota(jnp.int32, (A_CHUNK, A_CHUNK), 0)
    col = lax.broadcasted_iota(jnp.int32, (A_CHUNK, A_CHUNK), 1)
    causal = row >= col
    wide = hb * LANES
    srow = lax.broadcasted_iota(jnp.int32, (LANES, 2 * wide), 0)
    scol = lax.broadcasted_iota(jnp.int32, (LANES, 2 * wide), 1)
    blk = scol // LANES
    want = jnp.where(blk < hb, hgrp * hb + blk, A_HEADS + hgrp * hb + (blk - hb))
    sel = jnp.where(srow == want, 1.0, 0.0).astype(BF16)
    frow = lax.broadcasted_iota(jnp.int32, (A_CHUNK, LANES), 1)
    first = jnp.where(frow == 0, 1.0, 0.0).astype(BF16)
    ones_aug = jnp.ones((A_CHUNK, A_AUG), BF16)
    heads = range(hb)

    def lanes_of(hh):
        return slice(hh * LANES, (hh + 1) * LANES)

    qk_lanes = [slice(hh * A_QK_PAD, (hh + 1) * A_QK_PAD) for hh in heads]
    v_lanes = [slice(hh * A_V_DIM, (hh + 1) * A_V_DIM) for hh in heads]

    def chunk_group(c, carry):
        group = range(A_CHUNKS_PER_ITER)
        rows = [pl.ds(pl.multiple_of((c * A_CHUNKS_PER_ITER + i) * A_CHUNK, A_CHUNK), A_CHUNK) for i in group]
        gates = [gt_ref[r, :] + gb_ref[...] for r in rows]
        g2 = [_dot_exact_rhs(gates[i], sel) for i in group]
        ic = [g2[i][:, :wide] for i in group]
        lf = [_log_sigmoid(g2[i][:, wide:]) for i in group]
        gc = [_dot_exact_lhs(tril, lf[i]) for i in group]
        r_mat = [[_dot_exact_lhs(first, (ic[i] - gc[i])[:, lanes_of(hh)], _NT) for hh in heads]
                 for i in group]
        dmat = [[jnp.where(causal, gc[i][:, hh * LANES:hh * LANES + A_CHUNK] + r_mat[i][hh], -jnp.inf)
                 for hh in heads] for i in group]
        mx = [jnp.concatenate([jnp.broadcast_to(jnp.max(dmat[i][hh], axis=1, keepdims=True), (A_CHUNK, LANES))
                               for hh in heads], axis=1) for i in group]

        m_prev = m_st[0:1, :]
        w_inter, e_neg, p, decay, ws = [], [], [], [], []
        for i in group:
            a = gc[i] + m_prev
            m_row = jnp.maximum(a, mx[i])
            w_inter.append(jnp.exp(a - m_row))
            e_neg.append(jnp.exp(-m_row))
            p.append([jnp.exp(dmat[i][hh] - m_row[:, hh * LANES:hh * LANES + A_CHUNK]) for hh in heads])
            g_end = gc[i][A_CHUNK - 1:A_CHUNK, :]
            a_end = g_end + m_prev
            w_end = g_end - gc[i] + ic[i]
            m_prev = jnp.maximum(a_end, jnp.max(w_end, axis=0, keepdims=True))
            decay.append(jnp.exp(a_end - m_prev))
            ws.append(jnp.exp(w_end - m_prev).astype(BF16))
        m_st[...] = jnp.broadcast_to(m_prev, (SUBLANES, wide))

        q = [[q_ref[r, qk_lanes[hh]] for hh in heads] for r in rows]
        k = [[k_ref[r, qk_lanes[hh]] for hh in heads] for r in rows]
        v_aug = [[jnp.concatenate([v_ref[r, v_lanes[hh]], ones_aug], axis=1) for hh in heads] for r in rows]
        qk = [[_dot(q[i][hh], k[i][hh], _NT) for hh in heads] for i in group]
        intra = [[_dot((qk[i][hh] * p[i][hh]).astype(BF16), v_aug[i][hh]) for hh in heads] for i in group]
        upd = [[_dot(k[i][hh] * jnp.concatenate([ws[i][:, lanes_of(hh)]] * (A_QK_PAD // LANES), axis=1),
                     v_aug[i][hh], _TN) for hh in heads] for i in group]
        state = [c_st[hh] for hh in heads]
        inter = []
        for i in group:
            inter.append([_dot(q[i][hh], state[hh].astype(BF16)) for hh in heads])
            state = [jnp.concatenate([decay[i][:, lanes_of(hh)]] * rep, axis=1) * state[hh] + upd[i][hh]
                     for hh in heads]
        for hh in heads:
            c_st[hh] = state[hh]
        for i in group:
            for hh in heads:
                tot = jnp.concatenate([w_inter[i][:, lanes_of(hh)]] * rep, axis=1) * inter[i][hh] + intra[i][hh]
                den = tot[:, A_V_DIM:]
                inv = 1.0 / jnp.maximum(jnp.abs(den), e_neg[i][:, lanes_of(hh)])
                h_out = tot[:, :A_V_DIM] * jnp.concatenate([inv] * (A_V_DIM // LANES), axis=1)
                ms = jnp.mean(h_out * h_out, axis=-1, keepdims=True)
                hn = h_out * lax.rsqrt(ms + EPS) * hg_ref[:, v_lanes[hh]]
                out_ref[rows[i], v_lanes[hh]] = (o_ref[rows[i], v_lanes[hh]] * hn).astype(out_ref.dtype)
        return carry

    lax.fori_loop(0, tb // (A_CHUNK * A_CHUNKS_PER_ITER), chunk_group, 0)


def _mlstm(zq, zk, zv, zo, zg, gate_b, head_g, batch, seq, tb, hb):
    t_total = batch * seq
    nt = seq // tb
    assert tb % (A_CHUNK * A_CHUNKS_PER_ITER) == 0
    tril = jnp.asarray(np.tril(np.ones((A_CHUNK, A_CHUNK), np.float32)), BF16)
    row_map = lambda b, h, t: (b * nt + t, h)
    head_map = lambda b, h, t: (0, h)
    fixed = lambda b, h, t: (0, 0)
    return pl.pallas_call(
        functools.partial(_mlstm_kernel, tb=tb, hb=hb),
        grid=(batch, A_HEADS // hb, nt),
        in_specs=[pl.BlockSpec((tb, hb * A_QK_PAD), row_map),
                  pl.BlockSpec((tb, hb * A_QK_PAD), row_map),
                  pl.BlockSpec((tb, hb * A_V_DIM), row_map),
                  pl.BlockSpec((tb, hb * A_V_DIM), row_map),
                  pl.BlockSpec((tb, LANES), lambda b, h, t: (b * nt + t, 0)),
                  pl.BlockSpec((1, LANES), fixed),
                  pl.BlockSpec((1, hb * A_V_DIM), head_map),
                  pl.BlockSpec((A_CHUNK, A_CHUNK), fixed)],
        out_specs=pl.BlockSpec((tb, hb * A_V_DIM), row_map),
        out_shape=jax.ShapeDtypeStruct((t_total, MIX_WIDTH), BF16),
        scratch_shapes=[pltpu.VMEM((hb, A_QK_PAD, A_V_DIM + A_AUG), F32),
                        pltpu.VMEM((SUBLANES, hb * LANES), F32)],
        compiler_params=_params(3),
        name="mlstm",
    )(zq, zk, zv, zo, zg, gate_b, head_g, tril)


def _hgrn2_pair_masks():
    n = CHUNK
    masks = [np.eye(n, dtype=np.float32)]
    for j in range(1, B_LEVELS + 1):
        c = n >> j
        pm = np.zeros((n, n), np.float32)
        for r in range(n):
            if r % (2 * c) >= c:
                mid = (r // (2 * c)) * 2 * c + c
                pm[r, mid - c:mid] = 1.0
        masks.append(pm)
    return np.stack(masks, axis=0)


def _hgrn2_kernel(q_ref, lf_ref, kk_ref, v_ref, sg_ref, hg_ref, tril_ref, mask_ref, out_ref, s_st, *, tb, hb):
    t = pl.program_id(2)

    @pl.when(t == 0)
    def _():
        s_st[...] = jnp.zeros_like(s_st)

    tril = tril_ref[...]
    wide = hb * B_K_DIM
    heads = range(hb)
    rowi = lax.broadcasted_iota(jnp.int32, (CHUNK, wide), 0)
    sub = lax.broadcasted_iota(jnp.int32, (CHUNK // SUBLANES, SUBLANES, wide), 1)

    def level_operand(q, kk, gc, lf, half):
        if half >= SUBLANES:
            pieces = []
            for b0 in range(0, CHUNK, 2 * half):
                mid = b0 + half
                ref = gc[mid - 1:mid, :]
                pieces.append(kk[b0:mid, :] * jnp.exp2(ref - gc[b0:mid, :]))
                pieces.append(q[mid:mid + half, :] * jnp.exp2(gc[mid:mid + half, :] - ref))
            return jnp.concatenate(pieces, axis=0)
        if half == 1:
            odd = (rowi & 1) != 0
            return jnp.where(odd, q * jnp.exp2(lf), kk)
        g3 = gc.reshape(CHUNK // SUBLANES, SUBLANES, wide)
        ref = g3[:, half - 1:half, :]
        for b0 in range(2 * half, SUBLANES, 2 * half):
            ref = jnp.where(sub >= b0, g3[:, b0 + half - 1:b0 + half, :], ref)
        d = g3 - ref
        upper = (sub & half) != 0
        ex = jnp.exp2(jnp.where(upper, d, -d)).reshape(CHUNK, wide)
        return jnp.where((rowi & half) != 0, q, kk) * ex

    def hl(x, hh):
        return x[:, hh * B_K_DIM:(hh + 1) * B_K_DIM]

    def chunk_group(c, carry):
        group = range(B_CHUNKS_PER_ITER)
        rows = [pl.ds(pl.multiple_of((c * B_CHUNKS_PER_ITER + i) * CHUNK, CHUNK), CHUNK) for i in group]
        q = [q_ref[r, :] for r in rows]
        lf = [lf_ref[r, :] for r in rows]
        kk = [kk_ref[r, :] for r in rows]
        v = [v_ref[r, :] for r in rows]
        gc = [_dot_exact_lhs(tril, lf[i]) for i in group]
        g_end = [gc[i][CHUNK - 1:CHUNK, :] for i in group]

        xs = [[level_operand(q[i], kk[i], gc[i], lf[i], CHUNK >> j).astype(BF16) for j in range(1, B_LEVELS + 1)]
              for i in group]
        q_bf = [q[i].astype(BF16) for i in group]
        k_bf = [kk[i].astype(BF16) for i in group]
        qg = [(q[i] * jnp.exp2(gc[i])).astype(BF16) for i in group]
        kd = [(kk[i] * jnp.exp2(g_end[i] - gc[i])).astype(BF16) for i in group]
        dec = [jnp.exp2(g_end[i]) for i in group]

        prod = [[[_dot(hl(q_bf[i], hh), hl(k_bf[i], hh), _NT)] + [_dot(hl(x, hh), hl(x, hh), _NT) for x in xs[i]]
                 for hh in heads] for i in group]
        upd = [[_dot(hl(v[i], hh), hl(kd[i], hh), _TN) for hh in heads] for i in group]
        state = [s_st[hh] for hh in heads]
        o_inter = []
        for i in group:
            o_inter.append([_dot(hl(qg[i], hh), state[hh].astype(BF16), _NT) for hh in heads])
            state = [state[hh] * hl(dec[i], hh) + upd[i][hh] for hh in heads]
        for hh in heads:
            s_st[hh] = state[hh]
        for i in group:
            outs = []
            for hh in heads:
                acc = mask_ref[0] * prod[i][hh][0]
                for j in range(1, B_LEVELS + 1):
                    acc = acc + mask_ref[j] * prod[i][hh][j]
                o = _dot(acc.astype(BF16), hl(v[i], hh)) + o_inter[i][hh]
                ms = jnp.mean(o * o, axis=-1, keepdims=True)
                outs.append(o * lax.rsqrt(ms + EPS))
            on = jnp.concatenate(outs, axis=1) * hg_ref[...]
            out_ref[rows[i], :] = (on * sg_ref[rows[i], :]).astype(out_ref.dtype)
        return carry

    lax.fori_loop(0, tb // (CHUNK * B_CHUNKS_PER_ITER), chunk_group, 0)


def _hgrn2(zq, zlf, zkk, zv, zsg, head_g, batch, seq, tb, hb):
    t_total = batch * seq
    nt = seq // tb
    assert tb % (CHUNK * B_CHUNKS_PER_ITER) == 0
    mask_np = _hgrn2_pair_masks()
    tril = jnp.asarray(np.tril(np.ones((CHUNK, CHUNK), np.float32)), BF16)
    row_map = lambda b, h, t: (b * nt + t, h)
    head_map = lambda b, h, t: (0, h)
    return pl.pallas_call(
        functools.partial(_hgrn2_kernel, tb=tb, hb=hb),
        grid=(batch, B_HEADS // hb, nt),
        in_specs=[pl.BlockSpec((tb, hb * B_K_DIM), row_map),
                  pl.BlockSpec((tb, hb * B_K_DIM), row_map),
                  pl.BlockSpec((tb, hb * B_K_DIM), row_map),
                  pl.BlockSpec((tb, hb * B_V_DIM), row_map),
                  pl.BlockSpec((tb, hb * B_V_DIM), row_map),
                  pl.BlockSpec((1, hb * B_V_DIM), head_map),
                  pl.BlockSpec((CHUNK, CHUNK), lambda b, h, t: (0, 0)),
                  pl.BlockSpec(mask_np.shape, lambda b, h, t: (0, 0, 0))],
        out_specs=pl.BlockSpec((tb, hb * B_V_DIM), row_map),
        out_shape=jax.ShapeDtypeStruct((t_total, MIX_WIDTH), BF16),
        scratch_shapes=[pltpu.VMEM((hb, B_V_DIM, B_K_DIM), F32)],
        compiler_params=_params(3),
        name="hgrn2",
    )(zq, zlf, zkk, zv, zsg, head_g, tril, jnp.asarray(mask_np, F32))


def _outproj_kernel(y_ref, xq_ref, kv_ref, wo_ref, h_ref, out_ref, hn_ref, rs_ref):
    kv = kv_ref[...]
    parts = []
    for hh in range(XA_HEADS):
        lo = hh * XA_HEAD_DIM
        qh = xq_ref[:, lo:lo + XA_HEAD_DIM]
        kh = kv[:, lo:lo + XA_HEAD_DIM].astype(BF16)
        vh = kv[:, XA_WIDTH + lo:XA_WIDTH + lo + XA_HEAD_DIM].astype(BF16)
        s = _dot(qh, kh, _NT) * (XA_HEAD_DIM ** -0.5)
        e = jnp.exp(s - jnp.max(s, axis=-1, keepdims=True))
        den = jnp.sum(e, axis=-1, keepdims=True)
        parts.append((_dot(e.astype(BF16), vh) / den).astype(BF16))
    y_mem = jnp.concatenate(parts, axis=1)
    h_new = (h_ref[...] + _dot(y_ref[...], wo_ref[0:MIX_WIDTH, :])
             + _dot(y_mem, wo_ref[MIX_WIDTH:D_MODEL, :]))
    out_ref[...] = h_new
    hn_ref[...] = h_new.astype(BF16)
    rs = lax.rsqrt(jnp.mean(h_new * h_new, axis=-1, keepdims=True) + EPS)
    rs_ref[...] = jnp.broadcast_to(rs, rs_ref.shape)


def _outproj(y_mix, zxq, kv_all, layer, w_out_bf, h, seq, mem_len, tm):
    t = h.shape[0]
    per_seq = seq // tm
    return pl.pallas_call(
        _outproj_kernel,
        grid=(t // tm,),
        in_specs=[pl.BlockSpec((tm, MIX_WIDTH), lambda i: (i, 0)),
                  pl.BlockSpec((tm, XA_WIDTH), lambda i: (i, 0)),
                  pl.BlockSpec((mem_len, 2 * XA_WIDTH), lambda i: (i // per_seq, layer)),
                  _resident((D_MODEL, D_MODEL)),
                  pl.BlockSpec((tm, D_MODEL), lambda i: (i, 0))],
        out_specs=[pl.BlockSpec((tm, D_MODEL), lambda i: (i, 0)),
                   pl.BlockSpec((tm, D_MODEL), lambda i: (i, 0)),
                   pl.BlockSpec((tm, LANES), lambda i: (i, 0))],
        out_shape=[jax.ShapeDtypeStruct((t, D_MODEL), F32),
                   jax.ShapeDtypeStruct((t, D_MODEL), BF16),
                   jax.ShapeDtypeStruct((t, LANES), F32)],
        compiler_params=_params(1),
        name="outproj",
    )(y_mix, zxq, kv_all, w_out_bf, h)


def _ffn_kernel(hn_ref, rs_ref, h_ref, wu_ref, wg_ref, cw_ref, cb_ref, wd_ref, out_ref,
                gbuf, gcarry, act_s, *, tm, nf, per_seq):
    i = pl.program_id(0)
    j = pl.program_id(1)
    tf = act_s.shape[2]

    @pl.when(j < nf)
    def _():
        hn = hn_ref[...]
        rs = jnp.concatenate([rs_ref[...]] * (tf // LANES), axis=1)
        g = _dot(hn, wg_ref[...]) * rs
        prev = jnp.where(i % per_seq == 0, 0.0, gcarry[j])
        gbuf[0:SUBLANES] = prev
        gbuf[SUBLANES:SUBLANES + tm] = g
        gcarry[j] = g[tm - SUBLANES:tm, :]
        gc = cb_ref[...]
        for k in range(FFN_CONV):
            lo = SUBLANES - (FFN_CONV - 1) + k
            gc = gc + cw_ref[k:k + 1, :] * gbuf[lo:lo + tm, :]
        sg = _silu(gc) * rs
        u = _dot(hn, wu_ref[...])
        act_s[j] = (sg * u).astype(BF16)

    @pl.when(j >= nf)
    def _():
        act = jnp.concatenate([act_s[f] for f in range(nf)], axis=1)
        out_ref[...] = h_ref[...] + _dot(act, wd_ref[...])


def _ffn(hn, rs, h, w_up_bf, conv_w, conv_b, w_down_all_bf, layer, seq, tm, tf, tn):
    t, d = h.shape
    nf = D_FF // tf
    nd = d // tn
    per_seq = seq // tm
    up = lambda j: jnp.minimum(j, nf - 1)
    down = lambda j: jnp.maximum(j - nf, 0)
    return pl.pallas_call(
        functools.partial(_ffn_kernel, tm=tm, nf=nf, per_seq=per_seq),
        grid=(t // tm, nf + nd),
        in_specs=[pl.BlockSpec((tm, d), lambda i, j: (i, 0)),
                  pl.BlockSpec((tm, LANES), lambda i, j: (i, 0)),
                  pl.BlockSpec((tm, tn), lambda i, j: (i, down(j))),
                  pl.BlockSpec((d, tf), lambda i, j: (0, up(j))),
                  pl.BlockSpec((d, tf), lambda i, j: (0, nf + up(j))),
                  pl.BlockSpec((FFN_CONV, tf), lambda i, j: (0, up(j))),
                  pl.BlockSpec((1, tf), lambda i, j: (0, up(j))),
                  pl.BlockSpec((None, D_FF, tn), lambda i, j: (layer, 0, down(j)))],
        out_specs=pl.BlockSpec((tm, tn), lambda i, j: (i, down(j))),
        out_shape=jax.ShapeDtypeStruct((t, d), F32),
        scratch_shapes=[pltpu.VMEM((tm + SUBLANES, tf), F32),
                        pltpu.VMEM((nf, SUBLANES, tf), F32),
                        pltpu.VMEM((nf, tm, tf), BF16)],
        compiler_params=_params(2),
        name="ffn",
    )(hn, rs, h, w_up_bf, w_up_bf, conv_w, conv_b.reshape(1, D_FF), w_down_all_bf)


def _final_norm_kernel(x_ref, g_ref, out_ref):
    x = x_ref[...]
    out_ref[...] = x * lax.rsqrt(jnp.mean(x * x, axis=-1, keepdims=True) + EPS) * g_ref[...]


def _final_norm(x, g, tm):
    t, d = x.shape
    return pl.pallas_call(
        _final_norm_kernel,
        grid=(t // tm,),
        in_specs=[pl.BlockSpec((tm, d), lambda i: (i, 0)),
                  pl.BlockSpec((1, d), lambda i: (0, 0))],
        out_specs=pl.BlockSpec((tm, d), lambda i: (i, 0)),
        out_shape=jax.ShapeDtypeStruct((t, d), F32),
        compiler_params=_params(1),
        name="final_norm",
    )(x, g.reshape(1, d))


def _pick(n, cap):
    b = min(n, cap)
    while n % b:
        b //= 2
    return b


def kernel(x, mem, norm_mix_g, norm_mem_g, norm_ffn_g, norm_out_g, w_mem_kv, a_w_in, a_gate_b, a_conv_w, a_conv_b, a_head_g, a_w_out, b_w_in, b_lb_logits, b_head_g, b_w_out, ffn_w_up, ffn_conv_w, ffn_conv_b, ffn_w_down):
    batch, seq, d = x.shape
    mem_len = mem.shape[1]
    depth = norm_mix_g.shape[0]
    t = batch * seq
    tm_proj = _pick(seq, 256)
    tm_out = _pick(seq, 512)
    tm_ffn = _pick(seq, 1024)
    tb = _pick(seq, 512)

    h = x.reshape(t, d)
    w_kv_all = jnp.concatenate([_gain_folded(w_mem_kv[layer], norm_mem_g[layer]) for layer in range(depth)], axis=1)
    kv_all = _norm_matmul(mem.reshape(batch * mem_len, d), w_kv_all, _pick(mem_len, 256))
    w_down_all = ffn_w_down.astype(BF16)
    for layer in range(depth):
        j = layer // N_MIXERS
        if layer % N_MIXERS == 0:
            zq, zk, zv, zo, zxq, zg = _proj_mlstm(h, _gain_folded(a_w_in[j], norm_mix_g[layer]), a_conv_w[j],
                                                  a_conv_b[j][None, :], seq, tm_proj)
            gb = jnp.pad(a_gate_b[j][None, :], ((0, 0), (0, LANES - 2 * A_HEADS)))
            y_mix = _mlstm(zq, zk, zv, zo, zg, gb, a_head_g[j].reshape(1, MIX_WIDTH), batch, seq, tb,
                           A_HEADS_PER_STEP)
            w_out = a_w_out[j]
        else:
            zq, zlf, zkk, zv, zsg, zxq = _proj_hgrn2(h, _gain_folded(b_w_in[j], norm_mix_g[layer]), b_lb_logits,
                                                     tm_proj, layer)
            y_mix = _hgrn2(zq, zlf, zkk, zv, zsg, b_head_g[j].reshape(1, MIX_WIDTH), batch, seq, tb,
                           B_HEADS_PER_STEP)
            w_out = b_w_out[j]
        h, hn, rs = _outproj(y_mix, zxq, kv_all, layer, w_out.astype(BF16), h, seq, mem_len, tm_out)
        h = _ffn(hn, rs, h, _gain_folded(ffn_w_up[layer], norm_ffn_g[layer]), ffn_conv_w[layer], ffn_conv_b[layer],
                 w_down_all, layer, seq, tm_ffn, 512, 512)
    return _final_norm(h, norm_out_g, tm_out).reshape(batch, seq, d)
```

```python
import functools

import numpy as np
import jax
import jax.numpy as jnp
from jax import lax
from jax.experimental import pallas as pl
from jax.experimental.pallas import tpu as pltpu

F32 = jnp.float32
BF16 = jnp.bfloat16

D_MODEL = 2048
N_MIXERS = 2
CHUNK = 64
EPS = 1e-6
LOG2_E = 1.4426950408889634
F32_TINY = 1e-37

XA_HEADS = 4
XA_WIDTH = D_MODEL // 4
XA_HEAD_DIM = XA_WIDTH // XA_HEADS
MIX_WIDTH = D_MODEL - XA_WIDTH

A_HEADS = 4
A_V_DIM = MIX_WIDTH // A_HEADS
A_QK_DIM = A_V_DIM // 2
A_QK_PAD = 256
A_CONV = 4
A_AUG = 128
A_HEADS_PER_STEP = 4
A_CHUNK = 128
A_CHUNKS_PER_ITER = 2
B_HEADS = 12
B_K_DIM = 128
B_V_DIM = 128
B_LEVELS = 6
B_HEADS_PER_STEP = 12
B_CHUNKS_PER_ITER = 2

D_FF = 5632
FFN_CONV = 3

LANES = 128
SUBLANES = 8
VMEM_LIMIT_BYTES = 56 * 1024 * 1024

_NN = (((1,), (0,)), ((), ()))
_NT = (((1,), (1,)), ((), ()))
_TN = (((0,), (0,)), ((), ()))


def _dot(a, b, dims=_NN):
    return lax.dot_general(a, b, dims, preferred_element_type=F32)


def _split2(x):
    hi = x.astype(BF16)
    return hi, (x - hi.astype(F32)).astype(BF16)


def _dot_exact_lhs(a_bf, x, dims=_NN):
    return sum(_dot(a_bf, p, dims) for p in _split2(x))


def _dot_exact_rhs(x, b_bf, dims=_NN):
    return sum(_dot(p, b_bf, dims) for p in _split2(x))


def _sigmoid(x):
    return 0.5 + 0.5 * jnp.tanh(0.5 * x)


def _silu(x):
    hx = 0.5 * x
    return hx + hx * jnp.tanh(hx)


def _log_sigmoid(x):
    return jnp.minimum(x, 0.0) - jnp.log(1.0 + jnp.exp(-jnp.abs(x)))


def _params(n_grid):
    return pltpu.CompilerParams(dimension_semantics=("arbitrary",) * n_grid,
                                vmem_limit_bytes=VMEM_LIMIT_BYTES)


def _resident(shape):
    nd = len(shape)
    return pl.BlockSpec(shape, lambda *_: (0,) * nd, pipeline_mode=pl.Buffered(1))


def _gain_folded(w, g):
    return (g[:, None] * w).astype(BF16)


def _rms_split(x_ref):
    x = x_ref[...]
    return x.astype(BF16), lax.rsqrt(jnp.mean(x * x, axis=-1, keepdims=True) + EPS)


def _norm_matmul_kernel(x_ref, w_ref, out_ref):
    x_bf, rs = _rms_split(x_ref)
    out_ref[...] = _dot(x_bf, w_ref[...]) * rs


def _norm_matmul(x, wg_bf, tm):
    t, d = x.shape
    n = wg_bf.shape[1]
    return pl.pallas_call(
        _norm_matmul_kernel,
        grid=(t // tm,),
        in_specs=[pl.BlockSpec((tm, d), lambda i: (i, 0)),
                  _resident((d, n))],
        out_specs=pl.BlockSpec((tm, n), lambda i: (i, 0)),
        out_shape=jax.ShapeDtypeStruct((t, n), F32),
        compiler_params=_params(1),
        name="norm_matmul",
    )(x, wg_bf)


def _proj_mlstm_kernel(x_ref, w_ref, wx_ref, wg_ref, cw_ref, cb_ref, q_out, k_out, v_out, o_out, xq_out, gt_out,
                       cbuf, *, tm, per_seq):
    i = pl.program_id(0)
    qk_w = 2 * A_HEADS * A_QK_DIM

    @pl.when(i % per_seq == 0)
    def _():
        cbuf[0:SUBLANES] = jnp.zeros((SUBLANES, qk_w), F32)

    x_bf, rs = _rms_split(x_ref)

    def proj(off, width):
        return _dot(x_bf, w_ref[:, off:off + width]) * rs

    cbuf[SUBLANES:SUBLANES + tm] = proj(0, qk_w)
    acc = cb_ref[...]
    for j in range(A_CONV):
        lo = SUBLANES - (A_CONV - 1) + j
        acc = acc + cw_ref[j:j + 1, :] * cbuf[lo:lo + tm, :]
    y = _silu(acc)
    cbuf[0:SUBLANES] = cbuf[tm:tm + SUBLANES]
    pad = jnp.zeros((tm, A_QK_PAD - A_QK_DIM), F32)
    for hh in range(A_HEADS):
        qh = y[:, hh * A_QK_DIM:(hh + 1) * A_QK_DIM]
        kh = y[:, (A_HEADS + hh) * A_QK_DIM:(A_HEADS + hh + 1) * A_QK_DIM] * (A_QK_DIM ** -0.5)
        q_out[:, hh * A_QK_PAD:(hh + 1) * A_QK_PAD] = jnp.concatenate([qh, pad], axis=1).astype(BF16)
        k_out[:, hh * A_QK_PAD:(hh + 1) * A_QK_PAD] = jnp.concatenate([kh, pad], axis=1).astype(BF16)
    off = qk_w
    v_out[...] = proj(off, MIX_WIDTH).astype(BF16)
    off += MIX_WIDTH
    o_out[...] = _sigmoid(proj(off, MIX_WIDTH))
    xq_out[...] = (_dot(x_bf, wx_ref[...]) * rs).astype(BF16)
    gt_out[...] = _dot(x_bf, wg_ref[...]) * rs


def _proj_mlstm(x, w_bf, conv_w, conv_b, seq, tm):
    t, d = x.shape
    qk_w = 2 * A_HEADS * A_QK_DIM
    g0 = qk_w + 2 * MIX_WIDTH
    n_gates = 2 * A_HEADS
    assert w_bf.shape[1] == g0 + n_gates + XA_WIDTH
    w_g = jnp.pad(w_bf[:, g0:g0 + n_gates], ((0, 0), (0, LANES - n_gates)))
    w_x = w_bf[:, g0 + n_gates:]
    widths = (A_HEADS * A_QK_PAD, A_HEADS * A_QK_PAD, MIX_WIDTH, MIX_WIDTH, XA_WIDTH, LANES)
    dtypes = (BF16, BF16, BF16, F32, BF16, F32)
    return pl.pallas_call(
        functools.partial(_proj_mlstm_kernel, tm=tm, per_seq=seq // tm),
        grid=(t // tm,),
        in_specs=[pl.BlockSpec((tm, d), lambda i: (i, 0)),
                  _resident((d, g0)),
                  _resident((d, XA_WIDTH)),
                  _resident((d, LANES)),
                  _resident((A_CONV, qk_w)),
                  _resident((1, qk_w))],
        out_specs=[pl.BlockSpec((tm, s), lambda i: (i, 0)) for s in widths],
        out_shape=[jax.ShapeDtypeStruct((t, s), dt) for s, dt in zip(widths, dtypes)],
        scratch_shapes=[pltpu.VMEM((tm + SUBLANES, qk_w), F32)],
        compiler_params=_params(1),
        name="proj_mlstm",
    )(x, w_bf, w_x, w_g, conv_w, conv_b)


def _proj_hgrn2_kernel(x_ref, w_ref, lbl_ref, q_out, lf_out, kk_out, v_out, sg_out, xq_out, *, layer):
    lg = lbl_ref[...]
    lg = lg - jnp.max(lg, axis=0, keepdims=True)
    pe = jnp.exp(lg)
    pr = pe / jnp.sum(pe, axis=0, keepdims=True)
    c0 = pr[0:1, :]
    cl = c0
    for r in range(1, layer + 1):
        cl = cl + pr[r:r + 1, :]
    lb = cl - c0
    om = 1.0 - lb

    x_bf, rs = _rms_split(x_ref)
    kw = B_HEADS * B_K_DIM

    def proj(off, width):
        return _dot(x_bf, w_ref[:, off:off + width]) * rs

    q_out[...] = _silu(proj(0, kw))
    fz = proj(kw, kw)
    e = jnp.exp(-jnp.abs(fz))
    r = 1.0 / (1.0 + e)
    gate = lb + om * jnp.where(fz >= 0, r, e * r)
    lf_out[...] = jnp.where(gate > F32_TINY, jnp.log2(gate), jnp.log2(om) + fz * LOG2_E)
    kk_out[...] = om * jnp.where(fz >= 0, e * r, r)
    off = 2 * kw
    v_out[...] = proj(off, MIX_WIDTH).astype(BF16)
    off += MIX_WIDTH
    sg_out[...] = _silu(proj(off, MIX_WIDTH))
    off += MIX_WIDTH
    xq_out[...] = proj(off, XA_WIDTH).astype(BF16)


def _proj_hgrn2(x, w_bf, lb_logits, tm, layer):
    t, d = x.shape
    n = w_bf.shape[1]
    kw = B_HEADS * B_K_DIM
    widths = (kw, kw, kw, MIX_WIDTH, MIX_WIDTH, XA_WIDTH)
    dtypes = (F32, F32, F32, BF16, F32, BF16)
    assert 2 * kw + 2 * MIX_WIDTH + XA_WIDTH == n
    return pl.pallas_call(
        functools.partial(_proj_hgrn2_kernel, layer=layer),
        grid=(t // tm,),
        in_specs=[pl.BlockSpec((tm, d), lambda i: (i, 0)),
                  _resident((d, n)),
                  _resident(lb_logits.shape)],
        out_specs=[pl.BlockSpec((tm, s), lambda i: (i, 0)) for s in widths],
        out_shape=[jax.ShapeDtypeStruct((t, s), dt) for s, dt in zip(widths, dtypes)],
        compiler_params=_params(1),
        name="proj_hgrn2",
    )(x, w_bf, lb_logits)


def _mlstm_kernel(q_ref, k_ref, v_ref, o_ref, gt_ref, gb_ref, hg_ref, tril_ref, out_ref, c_st, m_st, *, tb, hb):
    hgrp = pl.program_id(1)
    t = pl.program_id(2)
    aug_w = A_V_DIM + A_AUG
    rep = aug_w // LANES

    @pl.when(t == 0)
    def _():
        c_st[...] = jnp.zeros_like(c_st)
        m_st[...] = jnp.zeros_like(m_st)

    tril = tril_ref[...]
    row = lax.broadcasted_iota(jnp.int32, (A_CHUNK, A_CHUNK), 0)
    col = lax.broadcasted_iota(jnp.int32, (A_CHUNK, A_CHUNK), 1)
    causal = row >= col
    wide = hb * LANES
    srow = lax.broadcasted_iota(jnp.int32, (LANES, 2 * wide), 0)
    scol = lax.broadcasted_iota(jnp.int32, (LANES, 2 * wide), 1)
    blk = scol // LANES
    want = jnp.where(blk < hb, hgrp * hb + blk, A_HEADS + hgrp * hb + (blk - hb))
    sel = jnp.where(srow == want, 1.0, 0.0).astype(BF16)
    frow = lax.broadcasted_iota(jnp.int32, (A_CHUNK, LANES), 1)
    first = jnp.where(frow == 0, 1.0, 0.0).astype(BF16)
    ones_aug = jnp.ones((A_CHUNK, A_AUG), BF16)
    heads = range(hb)

    def lanes_of(hh):
        return slice(hh * LANES, (hh + 1) * LANES)

    qk_lanes = [slice(hh * A_QK_PAD, (hh + 1) * A_QK_PAD) for hh in heads]
    v_lanes = [slice(hh * A_V_DIM, (hh + 1) * A_V_DIM) for hh in heads]

    def chunk_group(c, carry):
        group = range(A_CHUNKS_PER_ITER)
        rows = [pl.ds(pl.multiple_of((c * A_CHUNKS_PER_ITER + i) * A_CHUNK, A_CHUNK), A_CHUNK) for i in group]
        gates = [gt_ref[r, :] + gb_ref[...] for r in rows]
        g2 = [_dot_exact_rhs(gates[i], sel) for i in group]
        ic = [g2[i][:, :wide] for i in group]
        lf = [_log_sigmoid(g2[i][:, wide:]) for i in group]
        gc = [_dot_exact_lhs(tril, lf[i]) for i in group]
        r_mat = [[_dot_exact_lhs(first, (ic[i] - gc[i])[:, lanes_of(hh)], _NT) for hh in heads]
                 for i in group]
        dmat = [[jnp.where(causal, gc[i][:, hh * LANES:hh * LANES + A_CHUNK] + r_mat[i][hh], -jnp.inf)
                 for hh in heads] for i in group]
        mx = [jnp.concatenate([jnp.broadcast_to(jnp.max(dmat[i][hh], axis=1, keepdims=True), (A_CHUNK, LANES))
                               for hh in heads], axis=1) for i in group]

        m_prev = m_st[0:1, :]
        w_inter, e_neg, p, decay, ws = [], [], [], [], []
        for i in group:
            a = gc[i] + m_prev
            m_row = jnp.maximum(a, mx[i])
            w_inter.append(jnp.exp(a - m_row))
            e_neg.append(jnp.exp(-m_row))
            p.append([jnp.exp(dmat[i][hh] - m_row[:, hh * LANES:hh * LANES + A_CHUNK]) for hh in heads])
            g_end = gc[i][A_CHUNK - 1:A_CHUNK, :]
            a_end = g_end + m_prev
            w_end = g_end - gc[i] + ic[i]
            m_prev = jnp.maximum(a_end, jnp.max(w_end, axis=0, keepdims=True))
            decay.append(jnp.exp(a_end - m_prev))
            ws.append(jnp.exp(w_end - m_prev).astype(BF16))
        m_st[...] = jnp.broadcast_to(m_prev, (SUBLANES, wide))

        q = [[q_ref[r, qk_lanes[hh]] for hh in heads] for r in rows]
        k = [[k_ref[r, qk_lanes[hh]] for hh in heads] for r in rows]
        v_aug = [[jnp.concatenate([v_ref[r, v_lanes[hh]], ones_aug], axis=1) for hh in heads] for r in rows]
        qk = [[_dot(q[i][hh], k[i][hh], _NT) for hh in heads] for i in group]
        intra = [[_dot((qk[i][hh] * p[i][hh]).astype(BF16), v_aug[i][hh]) for hh in heads] for i in group]
        upd = [[_dot(k[i][hh] * jnp.concatenate([ws[i][:, lanes_of(hh)]] * (A_QK_PAD // LANES), axis=1),
                     v_aug[i][hh], _TN) for hh in heads] for i in group]
        state = [c_st[hh] for hh in heads]
        inter = []
        for i in group:
            inter.append([_dot(q[i][hh], state[hh].astype(BF16)) for hh in heads])
            state = [jnp.concatenate([decay[i][:, lanes_of(hh)]] * rep, axis=1) * state[hh] + upd[i][hh]
                     for hh in heads]
        for hh in heads:
            c_st[hh] = state[hh]
        for i in group:
            for hh in heads:
                tot = jnp.concatenate([w_inter[i][:, lanes_of(hh)]] * rep, axis=1) * inter[i][hh] + intra[i][hh]
                den = tot[:, A_V_DIM:]
                inv = 1.0 / jnp.maximum(jnp.abs(den), e_neg[i][:, lanes_of(hh)])
                h_out = tot[:, :A_V_DIM] * jnp.concatenate([inv] * (A_V_DIM // LANES), axis=1)
                ms = jnp.mean(h_out * h_out, axis=-1, keepdims=True)
                hn = h_out * lax.rsqrt(ms + EPS) * hg_ref[:, v_lanes[hh]]
                out_ref[rows[i], v_lanes[hh]] = (o_ref[rows[i], v_lanes[hh]] * hn).astype(out_ref.dtype)
        return carry

    lax.fori_loop(0, tb // (A_CHUNK * A_CHUNKS_PER_ITER), chunk_group, 0)


def _mlstm(zq, zk, zv, zo, zg, gate_b, head_g, batch, seq, tb, hb):
    t_total = batch * seq
    nt = seq // tb
    assert tb % (A_CHUNK * A_CHUNKS_PER_ITER) == 0
    tril = jnp.asarray(np.tril(np.ones((A_CHUNK, A_CHUNK), np.float32)), BF16)
    row_map = lambda b, h, t: (b * nt + t, h)
    head_map = lambda b, h, t: (0, h)
    fixed = lambda b, h, t: (0, 0)
    return pl.pallas_call(
        functools.partial(_mlstm_kernel, tb=tb, hb=hb),
        grid=(batch, A_HEADS // hb, nt),
        in_specs=[pl.BlockSpec((tb, hb * A_QK_PAD), row_map),
                  pl.BlockSpec((tb, hb * A_QK_PAD), row_map),
                  pl.BlockSpec((tb, hb * A_V_DIM), row_map),
                  pl.BlockSpec((tb, hb * A_V_DIM), row_map),
                  pl.BlockSpec((tb, LANES), lambda b, h, t: (b * nt + t, 0)),
                  pl.BlockSpec((1, LANES), fixed),
                  pl.BlockSpec((1, hb * A_V_DIM), head_map),
                  pl.BlockSpec((A_CHUNK, A_CHUNK), fixed)],
        out_specs=pl.BlockSpec((tb, hb * A_V_DIM), row_map),
        out_shape=jax.ShapeDtypeStruct((t_total, MIX_WIDTH), BF16),
        scratch_shapes=[pltpu.VMEM((hb, A_QK_PAD, A_V_DIM + A_AUG), F32),
                        pltpu.VMEM((SUBLANES, hb * LANES), F32)],
        compiler_params=_params(3),
        name="mlstm",
    )(zq, zk, zv, zo, zg, gate_b, head_g, tril)


def _hgrn2_pair_masks():
    n = CHUNK
    masks = [np.eye(n, dtype=np.float32)]
    for j in range(1, B_LEVELS + 1):
        c = n >> j
        pm = np.zeros((n, n), np.float32)
        for r in range(n):
            if r % (2 * c) >= c:
                mid = (r // (2 * c)) * 2 * c + c
                pm[r, mid - c:mid] = 1.0
        masks.append(pm)
    return np.stack(masks, axis=0)


def _hgrn2_kernel(q_ref, lf_ref, kk_ref, v_ref, sg_ref, hg_ref, tril_ref, mask_ref, out_ref, s_st, *, tb, hb):
    t = pl.program_id(2)

    @pl.when(t == 0)
    def _():
        s_st[...] = jnp.zeros_like(s_st)

    tril = tril_ref[...]
    wide = hb * B_K_DIM
    heads = range(hb)
    rowi = lax.broadcasted_iota(jnp.int32, (CHUNK, wide), 0)
    sub = lax.broadcasted_iota(jnp.int32, (CHUNK // SUBLANES, SUBLANES, wide), 1)

    def level_operand(q, kk, gc, lf, half):
        if half >= SUBLANES:
            pieces = []
            for b0 in range(0, CHUNK, 2 * half):
                mid = b0 + half
                ref = gc[mid - 1:mid, :]
                pieces.append(kk[b0:mid, :] * jnp.exp2(ref - gc[b0:mid, :]))
                pieces.append(q[mid:mid + half, :] * jnp.exp2(gc[mid:mid + half, :] - ref))
            return jnp.concatenate(pieces, axis=0)
        if half == 1:
            odd = (rowi & 1) != 0
            return jnp.where(odd, q * jnp.exp2(lf), kk)
        g3 = gc.reshape(CHUNK // SUBLANES, SUBLANES, wide)
        ref = g3[:, half - 1:half, :]
        for b0 in range(2 * half, SUBLANES, 2 * half):
            ref = jnp.where(sub >= b0, g3[:, b0 + half - 1:b0 + half, :], ref)
        d = g3 - ref
        upper = (sub & half) != 0
        ex = jnp.exp2(jnp.where(upper, d, -d)).reshape(CHUNK, wide)
        return jnp.where((rowi & half) != 0, q, kk) * ex

    def hl(x, hh):
        return x[:, hh * B_K_DIM:(hh + 1) * B_K_DIM]

    def chunk_group(c, carry):
        group = range(B_CHUNKS_PER_ITER)
        rows = [pl.ds(pl.multiple_of((c * B_CHUNKS_PER_ITER + i) * CHUNK, CHUNK), CHUNK) for i in group]
        q = [q_ref[r, :] for r in rows]
        lf = [lf_ref[r, :] for r in rows]
        kk = [kk_ref[r, :] for r in rows]
        v = [v_ref[r, :] for r in rows]
        gc = [_dot_exact_lhs(tril, lf[i]) for i in group]
        g_end = [gc[i][CHUNK - 1:CHUNK, :] for i in group]

        xs = [[level_operand(q[i], kk[i], gc[i], lf[i], CHUNK >> j).astype(BF16) for j in range(1, B_LEVELS + 1)]
              for i in group]
        q_bf = [q[i].astype(BF16) for i in group]
        k_bf = [kk[i].astype(BF16) for i in group]
        qg = [(q[i] * jnp.exp2(gc[i])).astype(BF16) for i in group]
        kd = [(kk[i] * jnp.exp2(g_end[i] - gc[i])).astype(BF16) for i in group]
        dec = [jnp.exp2(g_end[i]) for i in group]

        prod = [[[_dot(hl(q_bf[i], hh), hl(k_bf[i], hh), _NT)] + [_dot(hl(x, hh), hl(x, hh), _NT) for x in xs[i]]
                 for hh in heads] for i in group]
        upd = [[_dot(hl(v[i], hh), hl(kd[i], hh), _TN) for hh in heads] for i in group]
        state = [s_st[hh] for hh in heads]
        o_inter = []
        for i in group:
            o_inter.append([_dot(hl(qg[i], hh), state[hh].astype(BF16), _NT) for hh in heads])
            state = [state[hh] * hl(dec[i], hh) + upd[i][hh] for hh in heads]
        for hh in heads:
            s_st[hh] = state[hh]
        for i in group:
            outs = []
            for hh in heads:
                acc = mask_ref[0] * prod[i][hh][0]
                for j in range(1, B_LEVELS + 1):
                    acc = acc + mask_ref[j] * prod[i][hh][j]
                o = _dot(acc.astype(BF16), hl(v[i], hh)) + o_inter[i][hh]
                ms = jnp.mean(o * o, axis=-1, keepdims=True)
                outs.append(o * lax.rsqrt(ms + EPS))
            on = jnp.concatenate(outs, axis=1) * hg_ref[...]
            out_ref[rows[i], :] = (on * sg_ref[rows[i], :]).astype(out_ref.dtype)
        return carry

    lax.fori_loop(0, tb // (CHUNK * B_CHUNKS_PER_ITER), chunk_group, 0)


def _hgrn2(zq, zlf, zkk, zv, zsg, head_g, batch, seq, tb, hb):
    t_total = batch * seq
    nt = seq // tb
    assert tb % (CHUNK * B_CHUNKS_PER_ITER) == 0
    mask_np = _hgrn2_pair_masks()
    tril = jnp.asarray(np.tril(np.ones((CHUNK, CHUNK), np.float32)), BF16)
    row_map = lambda b, h, t: (b * nt + t, h)
    head_map = lambda b, h, t: (0, h)
    return pl.pallas_call(
        functools.partial(_hgrn2_kernel, tb=tb, hb=hb),
        grid=(batch, B_HEADS // hb, nt),
        in_specs=[pl.BlockSpec((tb, hb * B_K_DIM), row_map),
                  pl.BlockSpec((tb, hb * B_K_DIM), row_map),
                  pl.BlockSpec((tb, hb * B_K_DIM), row_map),
                  pl.BlockSpec((tb, hb * B_V_DIM), row_map),
                  pl.BlockSpec((tb, hb * B_V_DIM), row_map),
                  pl.BlockSpec((1, hb * B_V_DIM), head_map),
                  pl.BlockSpec((CHUNK, CHUNK), lambda b, h, t: (0, 0)),
                  pl.BlockSpec(mask_np.shape, lambda b, h, t: (0, 0, 0))],
        out_specs=pl.BlockSpec((tb, hb * B_V_DIM), row_map),
        out_shape=jax.ShapeDtypeStruct((t_total, MIX_WIDTH), BF16),
        scratch_shapes=[pltpu.VMEM((hb, B_V_DIM, B_K_DIM), F32)],
        compiler_params=_params(3),
        name="hgrn2",
    )(zq, zlf, zkk, zv, zsg, head_g, tril, jnp.asarray(mask_np, F32))


def _outproj_kernel(y_ref, xq_ref, kv_ref, wo_ref, h_ref, out_ref, hn_ref, rs_ref):
    kv = kv_ref[...]
    parts = []
    for hh in range(XA_HEADS):
        lo = hh * XA_HEAD_DIM
        qh = xq_ref[:, lo:lo + XA_HEAD_DIM]
        kh = kv[:, lo:lo + XA_HEAD_DIM].astype(BF16)
        vh = kv[:, XA_WIDTH + lo:XA_WIDTH + lo + XA_HEAD_DIM].astype(BF16)
        s = _dot(qh, kh, _NT) * (XA_HEAD_DIM ** -0.5)
        e = jnp.exp(s - jnp.max(s, axis=-1, keepdims=True))
        den = jnp.sum(e, axis=-1, keepdims=True)
        parts.append((_dot(e.astype(BF16), vh) / den).astype(BF16))
    y_mem = jnp.concatenate(parts, axis=1)
    h_new = (h_ref[...] + _dot(y_ref[...], wo_ref[0:MIX_WIDTH, :])
             + _dot(y_mem, wo_ref[MIX_WIDTH:D_MODEL, :]))
    out_ref[...] = h_new
    hn_ref[...] = h_new.astype(BF16)
    rs = lax.rsqrt(jnp.mean(h_new * h_new, axis=-1, keepdims=True) + EPS)
    rs_ref[...] = jnp.broadcast_to(rs, rs_ref.shape)


def _outproj(y_mix, zxq, kv_all, layer, w_out_bf, h, seq, mem_len, tm):
    t = h.shape[0]
    per_seq = seq // tm
    return pl.pallas_call(
        _outproj_kernel,
        grid=(t // tm,),
        in_specs=[pl.BlockSpec((tm, MIX_WIDTH), lambda i: (i, 0)),
                  pl.BlockSpec((tm, XA_WIDTH), lambda i: (i, 0)),
                  pl.BlockSpec((mem_len, 2 * XA_WIDTH), lambda i: (i // per_seq, layer)),
                  _resident((D_MODEL, D_MODEL)),
                  pl.BlockSpec((tm, D_MODEL), lambda i: (i, 0))],
        out_specs=[pl.BlockSpec((tm, D_MODEL), lambda i: (i, 0)),
                   pl.BlockSpec((tm, D_MODEL), lambda i: (i, 0)),
                   pl.BlockSpec((tm, LANES), lambda i: (i, 0))],
        out_shape=[jax.ShapeDtypeStruct((t, D_MODEL), F32),
                   jax.ShapeDtypeStruct((t, D_MODEL), BF16),
                   jax.ShapeDtypeStruct((t, LANES), F32)],
        compiler_params=_params(1),
        name="outproj",
    )(y_mix, zxq, kv_all, w_out_bf, h)


def _ffn_kernel(hn_ref, rs_ref, h_ref, wu_ref, wg_ref, cw_ref, cb_ref, wd_ref, out_ref,
                gbuf, gcarry, act_s, *, tm, nf, per_seq):
    i = pl.program_id(0)
    j = pl.program_id(1)
    tf = act_s.shape[2]

    @pl.when(j < nf)
    def _():
        hn = hn_ref[...]
        rs = jnp.concatenate([rs_ref[...]] * (tf // LANES), axis=1)
        g = _dot(hn, wg_ref[...]) * rs
        prev = jnp.where(i % per_seq == 0, 0.0, gcarry[j])
        gbuf[0:SUBLANES] = prev
        gbuf[SUBLANES:SUBLANES + tm] = g
        gcarry[j] = g[tm - SUBLANES:tm, :]
        gc = cb_ref[...]
        for k in range(FFN_CONV):
            lo = SUBLANES - (FFN_CONV - 1) + k
            gc = gc + cw_ref[k:k + 1, :] * gbuf[lo:lo + tm, :]
        sg = _silu(gc) * rs
        u = _dot(hn, wu_ref[...])
        act_s[j] = (sg * u).astype(BF16)

    @pl.when(j >= nf)
    def _():
        act = jnp.concatenate([act_s[f] for f in range(nf)], axis=1)
        out_ref[...] = h_ref[...] + _dot(act, wd_ref[...])


def _ffn(hn, rs, h, w_up_bf, conv_w, conv_b, w_down_all_bf, layer, seq, tm, tf, tn):
    t, d = h.shape
    nf = D_FF // tf
    nd = d // tn
    per_seq = seq // tm
    up = lambda j: jnp.minimum(j, nf - 1)
    down = lambda j: jnp.maximum(j - nf, 0)
    return pl.pallas_call(
        functools.partial(_ffn_kernel, tm=tm, nf=nf, per_seq=per_seq),
        grid=(t // tm, nf + nd),
        in_specs=[pl.BlockSpec((tm, d), lambda i, j: (i, 0)),
                  pl.BlockSpec((tm, LANES), lambda i, j: (i, 0)),
                  pl.BlockSpec((tm, tn), lambda i, j: (i, down(j))),
                  pl.BlockSpec((d, tf), lambda i, j: (0, up(j))),
                  pl.BlockSpec((d, tf), lambda i, j: (0, nf + up(j))),
                  pl.BlockSpec((FFN_CONV, tf), lambda i, j: (0, up(j))),
                  pl.BlockSpec((1, tf), lambda i, j: (0, up(j))),
                  pl.BlockSpec((None, D_FF, tn), lambda i, j: (layer, 0, down(j)))],
        out_specs=pl.BlockSpec((tm, tn), lambda i, j: (i, down(j))),
        out_shape=jax.ShapeDtypeStruct((t, d), F32),
        scratch_shapes=[pltpu.VMEM((tm + SUBLANES, tf), F32),
                        pltpu.VMEM((nf, SUBLANES, tf), F32),
                        pltpu.VMEM((nf, tm, tf), BF16)],
        compiler_params=_params(2),
        name="ffn",
    )(hn, rs, h, w_up_bf, w_up_bf, conv_w, conv_b.reshape(1, D_FF), w_down_all_bf)


def _final_norm_kernel(x_ref, g_ref, out_ref):
    x = x_ref[...]
    out_ref[...] = x * lax.rsqrt(jnp.mean(x * x, axis=-1, keepdims=True) + EPS) * g_ref[...]


def _final_norm(x, g, tm):
    t, d = x.shape
    return pl.pallas_call(
        _final_norm_kernel,
        grid=(t // tm,),
        in_specs=[pl.BlockSpec((tm, d), lambda i: (i, 0)),
                  pl.BlockSpec((1, d), lambda i: (0, 0))],
        out_specs=pl.BlockSpec((tm, d), lambda i: (i, 0)),
        out_shape=jax.ShapeDtypeStruct((t, d), F32),
        compiler_params=_params(1),
        name="final_norm",
    )(x, g.reshape(1, d))


def _pick(n, cap):
    b = min(n, cap)
    while n % b:
        b //= 2
    return b


def kernel(x, mem, norm_mix_g, norm_mem_g, norm_ffn_g, norm_out_g, w_mem_kv, a_w_in, a_gate_b, a_conv_w, a_conv_b, a_head_g, a_w_out, b_w_in, b_lb_logits, b_head_g, b_w_out, ffn_w_up, ffn_conv_w, ffn_conv_b, ffn_w_down):
    batch, seq, d = x.shape
    mem_len = mem.shape[1]
    depth = norm_mix_g.shape[0]
    t = batch * seq
    tm_proj = _pick(seq, 256)
    tm_out = _pick(seq, 512)
    tm_ffn = _pick(seq, 1024)
    tb = _pick(seq, 512)

    h = x.reshape(t, d)
    w_kv_all = jnp.concatenate([_gain_folded(w_mem_kv[layer], norm_mem_g[layer]) for layer in range(depth)], axis=1)
    kv_all = _norm_matmul(mem.reshape(batch * mem_len, d), w_kv_all, _pick(mem_len, 256))
    w_down_all = ffn_w_down.astype(BF16)
    for layer in range(depth):
        j = layer // N_MIXERS
        if layer % N_MIXERS == 0:
            zq, zk, zv, zo, zxq, zg = _proj_mlstm(h, _gain_folded(a_w_in[j], norm_mix_g[layer]), a_conv_w[j],
                                                  a_conv_b[j][None, :], seq, tm_proj)
            gb = jnp.pad(a_gate_b[j][None, :], ((0, 0), (0, LANES - 2 * A_HEADS)))
            y_mix = _mlstm(zq, zk, zv, zo, zg, gb, a_head_g[j].reshape(1, MIX_WIDTH), batch, seq, tb,
                           A_HEADS_PER_STEP)
            w_out = a_w_out[j]
        else:
            zq, zlf, zkk, zv, zsg, zxq = _proj_hgrn2(h, _gain_folded(b_w_in[j], norm_mix_g[layer]), b_lb_logits,
                                                     tm_proj, layer)
            y_mix = _hgrn2(zq, zlf, zkk, zv, zsg, b_head_g[j].reshape(1, MIX_WIDTH), batch, seq, tb,
                           B_HEADS_PER_STEP)
            w_out = b_w_out[j]
        h, hn, rs = _outproj(y_mix, zxq, kv_all, layer, w_out.astype(BF16), h, seq, mem_len, tm_out)
        h = _ffn(hn, rs, h, _gain_folded(ffn_w_up[layer], norm_ffn_g[layer]), ffn_conv_w[layer], ffn_conv_b[layer],
                 w_down_all, layer, seq, tm_ffn, 512, 512)
    return _final_norm(h, norm_out_g, tm_out).reshape(batch, seq, d)
```

```python
import functools

import numpy as np
import jax
import jax.numpy as jnp
from jax import lax
from jax.experimental import pallas as pl
from jax.experimental.pallas import tpu as pltpu

F32 = jnp.float32
BF16 = jnp.bfloat16

D_MODEL = 2048
N_MIXERS = 2
CHUNK = 64
EPS = 1e-6
LOG2_E = 1.4426950408889634
F32_TINY = 1e-37

XA_HEADS = 4
XA_WIDTH = D_MODEL // 4
XA_HEAD_DIM = XA_WIDTH // XA_HEADS
MIX_WIDTH = D_MODEL - XA_WIDTH

A_HEADS = 4
A_V_DIM = MIX_WIDTH // A_HEADS
A_QK_DIM = A_V_DIM // 2
A_QK_PAD = 256
A_CONV = 4
A_AUG = 128
A_HEADS_PER_STEP = 4
A_CHUNK = 128
A_CHUNKS_PER_ITER = 2
B_HEADS = 12
B_K_DIM = 128
B_V_DIM = 128
B_LEVELS = 6
B_HEADS_PER_STEP = 12
B_CHUNKS_PER_ITER = 2

D_FF = 5632
FFN_CONV = 3

LANES = 128
SUBLANES = 8
VMEM_LIMIT_BYTES = 56 * 1024 * 1024

_NN = (((1,), (0,)), ((), ()))
_NT = (((1,), (1,)), ((), ()))
_TN = (((0,), (0,)), ((), ()))


def _dot(a, b, dims=_NN):
    return lax.dot_general(a, b, dims, preferred_element_type=F32)


def _split2(x):
    hi = x.astype(BF16)
    return hi, (x - hi.astype(F32)).astype(BF16)


def _dot_exact_lhs(a_bf, x, dims=_NN):
    return sum(_dot(a_bf, p, dims) for p in _split2(x))


def _dot_exact_rhs(x, b_bf, dims=_NN):
    return sum(_dot(p, b_bf, dims) for p in _split2(x))


def _sigmoid(x):
    return 0.5 + 0.5 * jnp.tanh(0.5 * x)


def _silu(x):
    hx = 0.5 * x
    return hx + hx * jnp.tanh(hx)


def _log_sigmoid(x):
    return jnp.minimum(x, 0.0) - jnp.log(1.0 + jnp.exp(-jnp.abs(x)))


def _params(n_grid):
    return pltpu.CompilerParams(dimension_semantics=("arbitrary",) * n_grid,
                                vmem_limit_bytes=VMEM_LIMIT_BYTES)


def _resident(shape):
    nd = len(shape)
    return pl.BlockSpec(shape, lambda *_: (0,) * nd, pipeline_mode=pl.Buffered(1))


def _gain_folded(w, g):
    return (g[:, None] * w).astype(BF16)


def _rms_split(x_ref):
    x = x_ref[...]
    return x.astype(BF16), lax.rsqrt(jnp.mean(x * x, axis=-1, keepdims=True) + EPS)


def _norm_matmul_kernel(x_ref, w_ref, out_ref):
    x_bf, rs = _rms_split(x_ref)
    out_ref[...] = _dot(x_bf, w_ref[...]) * rs


def _norm_matmul(x, wg_bf, tm):
    t, d = x.shape
    n = wg_bf.shape[1]
    return pl.pallas_call(
        _norm_matmul_kernel,
        grid=(t // tm,),
        in_specs=[pl.BlockSpec((tm, d), lambda i: (i, 0)),
                  _resident((d, n))],
        out_specs=pl.BlockSpec((tm, n), lambda i: (i, 0)),
        out_shape=jax.ShapeDtypeStruct((t, n), F32),
        compiler_params=_params(1),
        name="norm_matmul",
    )(x, wg_bf)


def _proj_mlstm_kernel(x_ref, w_ref, wx_ref, wg_ref, cw_ref, cb_ref, q_out, k_out, v_out, o_out, xq_out, gt_out,
                       cbuf, *, tm, per_seq):
    i = pl.program_id(0)
    qk_w = 2 * A_HEADS * A_QK_DIM

    @pl.when(i % per_seq == 0)
    def _():
        cbuf[0:SUBLANES] = jnp.zeros((SUBLANES, qk_w), F32)

    x_bf, rs = _rms_split(x_ref)

    def proj(off, width):
        return _dot(x_bf, w_ref[:, off:off + width]) * rs

    cbuf[SUBLANES:SUBLANES + tm] = proj(0, qk_w)
    acc = cb_ref[...]
    for j in range(A_CONV):
        lo = SUBLANES - (A_CONV - 1) + j
        acc = acc + cw_ref[j:j + 1, :] * cbuf[lo:lo + tm, :]
    y = _silu(acc)
    cbuf[0:SUBLANES] = cbuf[tm:tm + SUBLANES]
    pad = jnp.zeros((tm, A_QK_PAD - A_QK_DIM), F32)
    for hh in range(A_HEADS):
        qh = y[:, hh * A_QK_DIM:(hh + 1) * A_QK_DIM]
        kh = y[:, (A_HEADS + hh) * A_QK_DIM:(A_HEADS + hh + 1) * A_QK_DIM] * (A_QK_DIM ** -0.5)
        q_out[:, hh * A_QK_PAD:(hh + 1) * A_QK_PAD] = jnp.concatenate([qh, pad], axis=1).astype(BF16)
        k_out[:, hh * A_QK_PAD:(hh + 1) * A_QK_PAD] = jnp.concatenate([kh, pad], axis=1).astype(BF16)
    off = qk_w
    v_out[...] = proj(off, MIX_WIDTH).astype(BF16)
    off += MIX_WIDTH
    o_out[...] = _sigmoid(proj(off, MIX_WIDTH))
    xq_out[...] = (_dot(x_bf, wx_ref[...]) * rs).astype(BF16)
    gt_out[...] = _dot(x_bf, wg_ref[...]) * rs


def _proj_mlstm(x, w_bf, conv_w, conv_b, seq, tm):
    t, d = x.shape
    qk_w = 2 * A_HEADS * A_QK_DIM
    g0 = qk_w + 2 * MIX_WIDTH
    n_gates = 2 * A_HEADS
    assert w_bf.shape[1] == g0 + n_gates + XA_WIDTH
    w_g = jnp.pad(w_bf[:, g0:g0 + n_gates], ((0, 0), (0, LANES - n_gates)))
    w_x = w_bf[:, g0 + n_gates:]
    widths = (A_HEADS * A_QK_PAD, A_HEADS * A_QK_PAD, MIX_WIDTH, MIX_WIDTH, XA_WIDTH, LANES)
    dtypes = (BF16, BF16, BF16, F32, BF16, F32)
    return pl.pallas_call(
        functools.partial(_proj_mlstm_kernel, tm=tm, per_seq=seq // tm),
        grid=(t // tm,),
        in_specs=[pl.BlockSpec((tm, d), lambda i: (i, 0)),
                  _resident((d, g0)),
                  _resident((d, XA_WIDTH)),
                  _resident((d, LANES)),
                  _resident((A_CONV, qk_w)),
                  _resident((1, qk_w))],
        out_specs=[pl.BlockSpec((tm, s), lambda i: (i, 0)) for s in widths],
        out_shape=[jax.ShapeDtypeStruct((t, s), dt) for s, dt in zip(widths, dtypes)],
        scratch_shapes=[pltpu.VMEM((tm + SUBLANES, qk_w), F32)],
        compiler_params=_params(1),
        name="proj_mlstm",
    )(x, w_bf, w_x, w_g, conv_w, conv_b)


def _proj_hgrn2_kernel(x_ref, w_ref, lbl_ref, q_out, lf_out, kk_out, v_out, sg_out, xq_out, *, layer):
    lg = lbl_ref[...]
    lg = lg - jnp.max(lg, axis=0, keepdims=True)
    pe = jnp.exp(lg)
    pr = pe / jnp.sum(pe, axis=0, keepdims=True)
    c0 = pr[0:1, :]
    cl = c0
    for r in range(1, layer + 1):
        cl = cl + pr[r:r + 1, :]
    lb = cl - c0
    om = 1.0 - lb

    x_bf, rs = _rms_split(x_ref)
    kw = B_HEADS * B_K_DIM

    def proj(off, width):
        return _dot(x_bf, w_ref[:, off:off + width]) * rs

    q_out[...] = _silu(proj(0, kw))
    fz = proj(kw, kw)
    sig = 1.0 / (1.0 + jnp.exp(-fz))
    gate = lb + om * sig
    lf_out[...] = jnp.where(gate > F32_TINY, jnp.log2(gate), jnp.log2(om) + fz * LOG2_E)
    kk_out[...] = om * (1.0 - sig)
    off = 2 * kw
    v_out[...] = proj(off, MIX_WIDTH).astype(BF16)
    off += MIX_WIDTH
    sg_out[...] = _silu(proj(off, MIX_WIDTH))
    off += MIX_WIDTH
    xq_out[...] = proj(off, XA_WIDTH).astype(BF16)


def _proj_hgrn2(x, w_bf, lb_logits, tm, layer):
    t, d = x.shape
    n = w_bf.shape[1]
    kw = B_HEADS * B_K_DIM
    widths = (kw, kw, kw, MIX_WIDTH, MIX_WIDTH, XA_WIDTH)
    dtypes = (F32, F32, F32, BF16, F32, BF16)
    assert 2 * kw + 2 * MIX_WIDTH + XA_WIDTH == n
    return pl.pallas_call(
        functools.partial(_proj_hgrn2_kernel, layer=layer),
        grid=(t // tm,),
        in_specs=[pl.BlockSpec((tm, d), lambda i: (i, 0)),
                  _resident((d, n)),
                  _resident(lb_logits.shape)],
        out_specs=[pl.BlockSpec((tm, s), lambda i: (i, 0)) for s in widths],
        out_shape=[jax.ShapeDtypeStruct((t, s), dt) for s, dt in zip(widths, dtypes)],
        compiler_params=_params(1),
        name="proj_hgrn2",
    )(x, w_bf, lb_logits)


def _mlstm_kernel(q_ref, k_ref, v_ref, o_ref, gt_ref, gb_ref, hg_ref, tril_ref, out_ref, c_st, m_st, *, tb, hb):
    hgrp = pl.program_id(1)
    t = pl.program_id(2)
    aug_w = A_V_DIM + A_AUG
    rep = aug_w // LANES

    @pl.when(t == 0)
    def _():
        c_st[...] = jnp.zeros_like(c_st)
        m_st[...] = jnp.zeros_like(m_st)

    tril = tril_ref[...]
    row = lax.broadcasted_iota(jnp.int32, (A_CHUNK, A_CHUNK), 0)
    col = lax.broadcasted_iota(jnp.int32, (A_CHUNK, A_CHUNK), 1)
    causal = row >= col
    wide = hb * LANES
    srow = lax.broadcasted_iota(jnp.int32, (LANES, 2 * wide), 0)
    scol = lax.broadcasted_iota(jnp.int32, (LANES, 2 * wide), 1)
    blk = scol // LANES
    want = jnp.where(blk < hb, hgrp * hb + blk, A_HEADS + hgrp * hb + (blk - hb))
    sel = jnp.where(srow == want, 1.0, 0.0).astype(BF16)
    frow = lax.broadcasted_iota(jnp.int32, (A_CHUNK, LANES), 1)
    first = jnp.where(frow == 0, 1.0, 0.0).astype(BF16)
    ones_aug = jnp.ones((A_CHUNK, A_AUG), BF16)
    heads = range(hb)

    def lanes_of(hh):
        return slice(hh * LANES, (hh + 1) * LANES)

    qk_lanes = [slice(hh * A_QK_PAD, (hh + 1) * A_QK_PAD) for hh in heads]
    v_lanes = [slice(hh * A_V_DIM, (hh + 1) * A_V_DIM) for hh in heads]

    def chunk_group(c, carry):
        group = range(A_CHUNKS_PER_ITER)
        rows = [pl.ds(pl.multiple_of((c * A_CHUNKS_PER_ITER + i) * A_CHUNK, A_CHUNK), A_CHUNK) for i in group]
        gates = [gt_ref[r, :] + gb_ref[...] for r in rows]
        g2 = [_dot_exact_rhs(gates[i], sel) for i in group]
        ic = [g2[i][:, :wide] for i in group]
        lf = [_log_sigmoid(g2[i][:, wide:]) for i in group]
        gc = [_dot_exact_lhs(tril, lf[i]) for i in group]
        r_mat = [[_dot_exact_lhs(first, (ic[i] - gc[i])[:, lanes_of(hh)], _NT) for hh in heads]
                 for i in group]
        dmat = [[jnp.where(causal, gc[i][:, hh * LANES:hh * LANES + A_CHUNK] + r_mat[i][hh], -jnp.inf)
                 for hh in heads] for i in group]
        mx = [jnp.concatenate([jnp.broadcast_to(jnp.max(dmat[i][hh], axis=1, keepdims=True), (A_CHUNK, LANES))
                               for hh in heads], axis=1) for i in group]

        m_prev = m_st[0:1, :]
        w_inter, e_neg, p, decay, ws = [], [], [], [], []
        for i in group:
            a = gc[i] + m_prev
            m_row = jnp.maximum(a, mx[i])
            w_inter.append(jnp.exp(a - m_row))
            e_neg.append(jnp.exp(-m_row))
            p.append([jnp.exp(dmat[i][hh] - m_row[:, hh * LANES:hh * LANES + A_CHUNK]) for hh in heads])
            g_end = gc[i][A_CHUNK - 1:A_CHUNK, :]
            a_end = g_end + m_prev
            w_end = g_end - gc[i] + ic[i]
            m_prev = jnp.maximum(a_end, jnp.max(w_end, axis=0, keepdims=True))
            decay.append(jnp.exp(a_end - m_prev))
            ws.append(jnp.exp(w_end - m_prev).astype(BF16))
        m_st[...] = jnp.broadcast_to(m_prev, (SUBLANES, wide))

        q = [[q_ref[r, qk_lanes[hh]] for hh in heads] for r in rows]
        k = [[k_ref[r, qk_lanes[hh]] for hh in heads] for r in rows]
        v_aug = [[jnp.concatenate([v_ref[r, v_lanes[hh]], ones_aug], axis=1) for hh in heads] for r in rows]
        qk = [[_dot(q[i][hh], k[i][hh], _NT) for hh in heads] for i in group]
        intra = [[_dot((qk[i][hh] * p[i][hh]).astype(BF16), v_aug[i][hh]) for hh in heads] for i in group]
        upd = [[_dot(k[i][hh] * jnp.concatenate([ws[i][:, lanes_of(hh)]] * (A_QK_PAD // LANES), axis=1),
                     v_aug[i][hh], _TN) for hh in heads] for i in group]
        state = [c_st[hh] for hh in heads]
        inter = []
        for i in group:
            inter.append([_dot(q[i][hh], state[hh].astype(BF16)) for hh in heads])
            state = [jnp.concatenate([decay[i][:, lanes_of(hh)]] * rep, axis=1) * state[hh] + upd[i][hh]
                     for hh in heads]
        for hh in heads:
            c_st[hh] = state[hh]
        for i in group:
            for hh in heads:
                tot = jnp.concatenate([w_inter[i][:, lanes_of(hh)]] * rep, axis=1) * inter[i][hh] + intra[i][hh]
                den = tot[:, A_V_DIM:]
                inv = 1.0 / jnp.maximum(jnp.abs(den), e_neg[i][:, lanes_of(hh)])
                h_out = tot[:, :A_V_DIM] * jnp.concatenate([inv] * (A_V_DIM // LANES), axis=1)
                ms = jnp.mean(h_out * h_out, axis=-1, keepdims=True)
                hn = h_out * lax.rsqrt(ms + EPS) * hg_ref[:, v_lanes[hh]]
                out_ref[rows[i], v_lanes[hh]] = (o_ref[rows[i], v_lanes[hh]] * hn).astype(out_ref.dtype)
        return carry

    lax.fori_loop(0, tb // (A_CHUNK * A_CHUNKS_PER_ITER), chunk_group, 0)


def _mlstm(zq, zk, zv, zo, zg, gate_b, head_g, batch, seq, tb, hb):
    t_total = batch * seq
    nt = seq // tb
    assert tb % (A_CHUNK * A_CHUNKS_PER_ITER) == 0
    tril = jnp.asarray(np.tril(np.ones((A_CHUNK, A_CHUNK), np.float32)), BF16)
    row_map = lambda b, h, t: (b * nt + t, h)
    head_map = lambda b, h, t: (0, h)
    fixed = lambda b, h, t: (0, 0)
    return pl.pallas_call(
        functools.partial(_mlstm_kernel, tb=tb, hb=hb),
        grid=(batch, A_HEADS // hb, nt),
        in_specs=[pl.BlockSpec((tb, hb * A_QK_PAD), row_map),
                  pl.BlockSpec((tb, hb * A_QK_PAD), row_map),
                  pl.BlockSpec((tb, hb * A_V_DIM), row_map),
                  pl.BlockSpec((tb, hb * A_V_DIM), row_map),
                  pl.BlockSpec((tb, LANES), lambda b, h, t: (b * nt + t, 0)),
                  pl.BlockSpec((1, LANES), fixed),
                  pl.BlockSpec((1, hb * A_V_DIM), head_map),
                  pl.BlockSpec((A_CHUNK, A_CHUNK), fixed)],
        out_specs=pl.BlockSpec((tb, hb * A_V_DIM), row_map),
        out_shape=jax.ShapeDtypeStruct((t_total, MIX_WIDTH), BF16),
        scratch_shapes=[pltpu.VMEM((hb, A_QK_PAD, A_V_DIM + A_AUG), F32),
                        pltpu.VMEM((SUBLANES, hb * LANES), F32)],
        compiler_params=_params(3),
        name="mlstm",
    )(zq, zk, zv, zo, zg, gate_b, head_g, tril)


def _hgrn2_pair_masks():
    n = CHUNK
    masks = [np.eye(n, dtype=np.float32)]
    for j in range(1, B_LEVELS + 1):
        c = n >> j
        pm = np.zeros((n, n), np.float32)
        for r in range(n):
            if r % (2 * c) >= c:
                mid = (r // (2 * c)) * 2 * c + c
                pm[r, mid - c:mid] = 1.0
        masks.append(pm)
    return np.stack(masks, axis=0)


def _hgrn2_kernel(q_ref, lf_ref, kk_ref, v_ref, sg_ref, hg_ref, tril_ref, mask_ref, out_ref, s_st, *, tb, hb):
    t = pl.program_id(2)

    @pl.when(t == 0)
    def _():
        s_st[...] = jnp.zeros_like(s_st)

    tril = tril_ref[...]
    wide = hb * B_K_DIM
    heads = range(hb)
    rowi = lax.broadcasted_iota(jnp.int32, (CHUNK, wide), 0)
    sub = lax.broadcasted_iota(jnp.int32, (CHUNK // SUBLANES, SUBLANES, wide), 1)

    def level_operand(q, kk, gc, lf, half):
        if half >= SUBLANES:
            pieces = []
            for b0 in range(0, CHUNK, 2 * half):
                mid = b0 + half
                ref = gc[mid - 1:mid, :]
                pieces.append(kk[b0:mid, :] * jnp.exp2(ref - gc[b0:mid, :]))
                pieces.append(q[mid:mid + half, :] * jnp.exp2(gc[mid:mid + half, :] - ref))
            return jnp.concatenate(pieces, axis=0)
        if half == 1:
            odd = (rowi & 1) != 0
            return jnp.where(odd, q * jnp.exp2(lf), kk)
        g3 = gc.reshape(CHUNK // SUBLANES, SUBLANES, wide)
        ref = g3[:, half - 1:half, :]
        for b0 in range(2 * half, SUBLANES, 2 * half):
            ref = jnp.where(sub >= b0, g3[:, b0 + half - 1:b0 + half, :], ref)
        d = g3 - ref
        upper = (sub & half) != 0
        ex = jnp.exp2(jnp.where(upper, d, -d)).reshape(CHUNK, wide)
        return jnp.where((rowi & half) != 0, q, kk) * ex

    def hl(x, hh):
        return x[:, hh * B_K_DIM:(hh + 1) * B_K_DIM]

    def chunk_group(c, carry):
        group = range(B_CHUNKS_PER_ITER)
        rows = [pl.ds(pl.multiple_of((c * B_CHUNKS_PER_ITER + i) * CHUNK, CHUNK), CHUNK) for i in group]
        q = [q_ref[r, :] for r in rows]
        lf = [lf_ref[r, :] for r in rows]
        kk = [kk_ref[r, :] for r in rows]
        v = [v_ref[r, :] for r in rows]
        gc = [_dot_exact_lhs(tril, lf[i]) for i in group]
        g_end = [gc[i][CHUNK - 1:CHUNK, :] for i in group]

        xs = [[level_operand(q[i], kk[i], gc[i], lf[i], CHUNK >> j).astype(BF16) for j in range(1, B_LEVELS + 1)]
              for i in group]
        q_bf = [q[i].astype(BF16) for i in group]
        k_bf = [kk[i].astype(BF16) for i in group]
        qg = [(q[i] * jnp.exp2(gc[i])).astype(BF16) for i in group]
        kd = [(kk[i] * jnp.exp2(g_end[i] - gc[i])).astype(BF16) for i in group]
        dec = [jnp.exp2(g_end[i]) for i in group]

        prod = [[[_dot(hl(q_bf[i], hh), hl(k_bf[i], hh), _NT)] + [_dot(hl(x, hh), hl(x, hh), _NT) for x in xs[i]]
                 for hh in heads] for i in group]
        upd = [[_dot(hl(v[i], hh), hl(kd[i], hh), _TN) for hh in heads] for i in group]
        state = [s_st[hh] for hh in heads]
        o_inter = []
        for i in group:
            o_inter.append([_dot(hl(qg[i], hh), state[hh].astype(BF16), _NT) for hh in heads])
            state = [state[hh] * hl(dec[i], hh) + upd[i][hh] for hh in heads]
        for hh in heads:
            s_st[hh] = state[hh]
        for i in group:
            outs = []
            for hh in heads:
                acc = mask_ref[0] * prod[i][hh][0]
                for j in range(1, B_LEVELS + 1):
                    acc = acc + mask_ref[j] * prod[i][hh][j]
                o = _dot(acc.astype(BF16), hl(v[i], hh)) + o_inter[i][hh]
                ms = jnp.mean(o * o, axis=-1, keepdims=True)
                outs.append(o * lax.rsqrt(ms + EPS))
            on = jnp.concatenate(outs, axis=1) * hg_ref[...]
            out_ref[rows[i], :] = (on * sg_ref[rows[i], :]).astype(out_ref.dtype)
        return carry

    lax.fori_loop(0, tb // (CHUNK * B_CHUNKS_PER_ITER), chunk_group, 0)


def _hgrn2(zq, zlf, zkk, zv, zsg, head_g, batch, seq, tb, hb):
    t_total = batch * seq
    nt = seq // tb
    assert tb % (CHUNK * B_CHUNKS_PER_ITER) == 0
    mask_np = _hgrn2_pair_masks()
    tril = jnp.asarray(np.tril(np.ones((CHUNK, CHUNK), np.float32)), BF16)
    row_map = lambda b, h, t: (b * nt + t, h)
    head_map = lambda b, h, t: (0, h)
    return pl.pallas_call(
        functools.partial(_hgrn2_kernel, tb=tb, hb=hb),
        grid=(batch, B_HEADS // hb, nt),
        in_specs=[pl.BlockSpec((tb, hb * B_K_DIM), row_map),
                  pl.BlockSpec((tb, hb * B_K_DIM), row_map),
                  pl.BlockSpec((tb, hb * B_K_DIM), row_map),
                  pl.BlockSpec((tb, hb * B_V_DIM), row_map),
                  pl.BlockSpec((tb, hb * B_V_DIM), row_map),
                  pl.BlockSpec((1, hb * B_V_DIM), head_map),
                  pl.BlockSpec((CHUNK, CHUNK), lambda b, h, t: (0, 0)),
                  pl.BlockSpec(mask_np.shape, lambda b, h, t: (0, 0, 0))],
        out_specs=pl.BlockSpec((tb, hb * B_V_DIM), row_map),
        out_shape=jax.ShapeDtypeStruct((t_total, MIX_WIDTH), BF16),
        scratch_shapes=[pltpu.VMEM((hb, B_V_DIM, B_K_DIM), F32)],
        compiler_params=_params(3),
        name="hgrn2",
    )(zq, zlf, zkk, zv, zsg, head_g, tril, jnp.asarray(mask_np, F32))


def _outproj_kernel(y_ref, xq_ref, kv_ref, wo_ref, h_ref, out_ref, hn_ref, rs_ref):
    kv = kv_ref[...]
    parts = []
    for hh in range(XA_HEADS):
        lo = hh * XA_HEAD_DIM
        qh = xq_ref[:, lo:lo + XA_HEAD_DIM]
        kh = kv[:, lo:lo + XA_HEAD_DIM].astype(BF16)
        vh = kv[:, XA_WIDTH + lo:XA_WIDTH + lo + XA_HEAD_DIM].astype(BF16)
        s = _dot(qh, kh, _NT) * (XA_HEAD_DIM ** -0.5)
        e = jnp.exp(s - jnp.max(s, axis=-1, keepdims=True))
        den = jnp.sum(e, axis=-1, keepdims=True)
        parts.append((_dot(e.astype(BF16), vh) / den).astype(BF16))
    y_mem = jnp.concatenate(parts, axis=1)
    h_new = (h_ref[...] + _dot(y_ref[...], wo_ref[0:MIX_WIDTH, :])
             + _dot(y_mem, wo_ref[MIX_WIDTH:D_MODEL, :]))
    out_ref[...] = h_new
    hn_ref[...] = h_new.astype(BF16)
    rs = lax.rsqrt(jnp.mean(h_new * h_new, axis=-1, keepdims=True) + EPS)
    rs_ref[...] = jnp.broadcast_to(rs, rs_ref.shape)


def _outproj(y_mix, zxq, kv_all, layer, w_out_bf, h, seq, mem_len, tm):
    t = h.shape[0]
    per_seq = seq // tm
    return pl.pallas_call(
        _outproj_kernel,
        grid=(t // tm,),
        in_specs=[pl.BlockSpec((tm, MIX_WIDTH), lambda i: (i, 0)),
                  pl.BlockSpec((tm, XA_WIDTH), lambda i: (i, 0)),
                  pl.BlockSpec((mem_len, 2 * XA_WIDTH), lambda i: (i // per_seq, layer)),
                  _resident((D_MODEL, D_MODEL)),
                  pl.BlockSpec((tm, D_MODEL), lambda i: (i, 0))],
        out_specs=[pl.BlockSpec((tm, D_MODEL), lambda i: (i, 0)),
                   pl.BlockSpec((tm, D_MODEL), lambda i: (i, 0)),
                   pl.BlockSpec((tm, LANES), lambda i: (i, 0))],
        out_shape=[jax.ShapeDtypeStruct((t, D_MODEL), F32),
                   jax.ShapeDtypeStruct((t, D_MODEL), BF16),
                   jax.ShapeDtypeStruct((t, LANES), F32)],
        compiler_params=_params(1),
        name="outproj",
    )(y_mix, zxq, kv_all, w_out_bf, h)


def _ffn_kernel(hn_ref, rs_ref, h_ref, wu_ref, wg_ref, cw_ref, cb_ref, wd_ref, out_ref,
                gbuf, gcarry, act_s, *, tm, nf, per_seq):
    i = pl.program_id(0)
    j = pl.program_id(1)
    tf = act_s.shape[2]

    @pl.when(j < nf)
    def _():
        hn = hn_ref[...]
        rs = jnp.concatenate([rs_ref[...]] * (tf // LANES), axis=1)
        g = _dot(hn, wg_ref[...]) * rs
        prev = jnp.where(i % per_seq == 0, 0.0, gcarry[j])
        gbuf[0:SUBLANES] = prev
        gbuf[SUBLANES:SUBLANES + tm] = g
        gcarry[j] = g[tm - SUBLANES:tm, :]
        gc = cb_ref[...]
        for k in range(FFN_CONV):
            lo = SUBLANES - (FFN_CONV - 1) + k
            gc = gc + cw_ref[k:k + 1, :] * gbuf[lo:lo + tm, :]
        sg = _silu(gc) * rs
        u = _dot(hn, wu_ref[...])
        act_s[j] = (sg * u).astype(BF16)

    @pl.when(j >= nf)
    def _():
        act = jnp.concatenate([act_s[f] for f in range(nf)], axis=1)
        out_ref[...] = h_ref[...] + _dot(act, wd_ref[...])


def _ffn(hn, rs, h, w_up_bf, conv_w, conv_b, w_down_all_bf, layer, seq, tm, tf, tn):
    t, d = h.shape
    nf = D_FF // tf
    nd = d // tn
    per_seq = seq // tm
    up = lambda j: jnp.minimum(j, nf - 1)
    down = lambda j: jnp.maximum(j - nf, 0)
    return pl.pallas_call(
        functools.partial(_ffn_kernel, tm=tm, nf=nf, per_seq=per_seq),
        grid=(t // tm, nf + nd),
        in_specs=[pl.BlockSpec((tm, d), lambda i, j: (i, 0)),
                  pl.BlockSpec((tm, LANES), lambda i, j: (i, 0)),
                  pl.BlockSpec((tm, tn), lambda i, j: (i, down(j))),
                  pl.BlockSpec((d, tf), lambda i, j: (0, up(j))),
                  pl.BlockSpec((d, tf), lambda i, j: (0, nf + up(j))),
                  pl.BlockSpec((FFN_CONV, tf), lambda i, j: (0, up(j))),
                  pl.BlockSpec((1, tf), lambda i, j: (0, up(j))),
                  pl.BlockSpec((None, D_FF, tn), lambda i, j: (layer, 0, down(j)))],
        out_specs=pl.BlockSpec((tm, tn), lambda i, j: (i, down(j))),
        out_shape=jax.ShapeDtypeStruct((t, d), F32),
        scratch_shapes=[pltpu.VMEM((tm + SUBLANES, tf), F32),
                        pltpu.VMEM((nf, SUBLANES, tf), F32),
                        pltpu.VMEM((nf, tm, tf), BF16)],
        compiler_params=_params(2),
        name="ffn",
    )(hn, rs, h, w_up_bf, w_up_bf, conv_w, conv_b.reshape(1, D_FF), w_down_all_bf)


def _final_norm_kernel(x_ref, g_ref, out_ref):
    x = x_ref[...]
    out_ref[...] = x * lax.rsqrt(jnp.mean(x * x, axis=-1, keepdims=True) + EPS) * g_ref[...]


def _final_norm(x, g, tm):
    t, d = x.shape
    return pl.pallas_call(
        _final_norm_kernel,
        grid=(t // tm,),
        in_specs=[pl.BlockSpec((tm, d), lambda i: (i, 0)),
                  pl.BlockSpec((1, d), lambda i: (0, 0))],
        out_specs=pl.BlockSpec((tm, d), lambda i: (i, 0)),
        out_shape=jax.ShapeDtypeStruct((t, d), F32),
        compiler_params=_params(1),
        name="final_norm",
    )(x, g.reshape(1, d))


def _pick(n, cap):
    b = min(n, cap)
    while n % b:
        b //= 2
    return b


def kernel(x, mem, norm_mix_g, norm_mem_g, norm_ffn_g, norm_out_g, w_mem_kv, a_w_in, a_gate_b, a_conv_w, a_conv_b, a_head_g, a_w_out, b_w_in, b_lb_logits, b_head_g, b_w_out, ffn_w_up, ffn_conv_w, ffn_conv_b, ffn_w_down):
    batch, seq, d = x.shape
    mem_len = mem.shape[1]
    depth = norm_mix_g.shape[0]
    t = batch * seq
    tm_proj = _pick(seq, 256)
    tm_out = _pick(seq, 512)
    tm_ffn = _pick(seq, 1024)
    tb = _pick(seq, 512)

    h = x.reshape(t, d)
    w_kv_all = jnp.concatenate([_gain_folded(w_mem_kv[layer], norm_mem_g[layer]) for layer in range(depth)], axis=1)
    kv_all = _norm_matmul(mem.reshape(batch * mem_len, d), w_kv_all, _pick(mem_len, 256))
    w_down_all = ffn_w_down.astype(BF16)
    for layer in range(depth):
        j = layer // N_MIXERS
        if layer % N_MIXERS == 0:
            zq, zk, zv, zo, zxq, zg = _proj_mlstm(h, _gain_folded(a_w_in[j], norm_mix_g[layer]), a_conv_w[j],
                                                  a_conv_b[j][None, :], seq, tm_proj)
            gb = jnp.pad(a_gate_b[j][None, :], ((0, 0), (0, LANES - 2 * A_HEADS)))
            y_mix = _mlstm(zq, zk, zv, zo, zg, gb, a_head_g[j].reshape(1, MIX_WIDTH), batch, seq, tb,
                           A_HEADS_PER_STEP)
            w_out = a_w_out[j]
        else:
            zq, zlf, zkk, zv, zsg, zxq = _proj_hgrn2(h, _gain_folded(b_w_in[j], norm_mix_g[layer]), b_lb_logits,
                                                     tm_proj, layer)
            y_mix = _hgrn2(zq, zlf, zkk, zv, zsg, b_head_g[j].reshape(1, MIX_WIDTH), batch, seq, tb,
                           B_HEADS_PER_STEP)
            w_out = b_w_out[j]
        h, hn, rs = _outproj(y_mix, zxq, kv_all, layer, w_out.astype(BF16), h, seq, mem_len, tm_out)
        h = _ffn(hn, rs, h, _gain_folded(ffn_w_up[layer], norm_ffn_g[layer]), ffn_conv_w[layer], ffn_conv_b[layer],
                 w_down_all, layer, seq, tm_ffn, 512, 512)
    return _final_norm(h, norm_out_g, tm_out).reshape(batch, seq, d)
```

```python
import functools

import numpy as np
import jax
import jax.numpy as jnp
from jax import lax
from jax.experimental import pallas as pl
from jax.experimental.pallas import tpu as pltpu

F32 = jnp.float32
BF16 = jnp.bfloat16

D_MODEL = 2048
N_MIXERS = 2
CHUNK = 64
EPS = 1e-6
LOG2_E = 1.4426950408889634
F32_TINY = 1e-37

XA_HEADS = 4
XA_WIDTH = D_MODEL // 4
XA_HEAD_DIM = XA_WIDTH // XA_HEADS
MIX_WIDTH = D_MODEL - XA_WIDTH

A_HEADS = 4
A_V_DIM = MIX_WIDTH // A_HEADS
A_QK_DIM = A_V_DIM // 2
A_QK_PAD = 256
A_CONV = 4
A_AUG = 128
A_HEADS_PER_STEP = 4
A_CHUNK = 128
A_CHUNKS_PER_ITER = 2
B_HEADS = 12
B_K_DIM = 128
B_V_DIM = 128
B_LEVELS = 6
B_HEADS_PER_STEP = 12
B_CHUNKS_PER_ITER = 2

D_FF = 5632
FFN_CONV = 3

LANES = 128
SUBLANES = 8
VMEM_LIMIT_BYTES = 56 * 1024 * 1024

TM_PROJ = 256
TM_OUT = 512
TM_FFN = 1024
TB_REC = 512
FFN_TF = 512
FFN_TN = 512

_NN = (((1,), (0,)), ((), ()))
_NT = (((1,), (1,)), ((), ()))
_TN = (((0,), (0,)), ((), ()))


def _dot(a, b, dims=_NN):
    return lax.dot_general(a, b, dims, preferred_element_type=F32)


def _split2(x):
    hi = x.astype(BF16)
    return hi, (x - hi.astype(F32)).astype(BF16)


def _dot_exact_lhs(a_bf, x, dims=_NN):
    return sum(_dot(a_bf, p, dims) for p in _split2(x))


def _dot_exact_rhs(x, b_bf, dims=_NN):
    return sum(_dot(p, b_bf, dims) for p in _split2(x))


def _sigmoid(x):
    return 0.5 + 0.5 * jnp.tanh(0.5 * x)


def _silu(x):
    hx = 0.5 * x
    return hx + hx * jnp.tanh(hx)


def _log_sigmoid(x):
    return jnp.minimum(x, 0.0) - jnp.log(1.0 + jnp.exp(-jnp.abs(x)))


def _params(n_grid):
    return pltpu.CompilerParams(dimension_semantics=("arbitrary",) * n_grid,
                                vmem_limit_bytes=VMEM_LIMIT_BYTES)


def _resident(shape):
    nd = len(shape)
    return pl.BlockSpec(shape, lambda *_: (0,) * nd, pipeline_mode=pl.Buffered(1))


def _gain_folded(w, g):
    return (g[:, None] * w).astype(BF16)


def _rms_split(x_ref):
    x = x_ref[...]
    return x.astype(BF16), lax.rsqrt(jnp.mean(x * x, axis=-1, keepdims=True) + EPS)


def _norm_matmul_kernel(x_ref, w_ref, out_ref):
    x_bf, rs = _rms_split(x_ref)
    out_ref[...] = _dot(x_bf, w_ref[...]) * rs


def _norm_matmul(x, wg_bf, tm):
    t, d = x.shape
    n = wg_bf.shape[1]
    return pl.pallas_call(
        _norm_matmul_kernel,
        grid=(t // tm,),
        in_specs=[pl.BlockSpec((tm, d), lambda i: (i, 0)),
                  _resident((d, n))],
        out_specs=pl.BlockSpec((tm, n), lambda i: (i, 0)),
        out_shape=jax.ShapeDtypeStruct((t, n), F32),
        compiler_params=_params(1),
        name="norm_matmul",
    )(x, wg_bf)


def _proj_mlstm_kernel(x_ref, w_ref, wx_ref, wg_ref, cw_ref, cb_ref, q_out, k_out, v_out, o_out, xq_out, gt_out,
                       cbuf, *, tm, per_seq):
    i = pl.program_id(0)
    qk_w = 2 * A_HEADS * A_QK_DIM

    @pl.when(i % per_seq == 0)
    def _():
        cbuf[0:SUBLANES] = jnp.zeros((SUBLANES, qk_w), F32)

    x_bf, rs = _rms_split(x_ref)

    def proj(off, width):
        return _dot(x_bf, w_ref[:, off:off + width]) * rs

    cbuf[SUBLANES:SUBLANES + tm] = proj(0, qk_w)
    acc = cb_ref[...]
    for j in range(A_CONV):
        lo = SUBLANES - (A_CONV - 1) + j
        acc = acc + cw_ref[j:j + 1, :] * cbuf[lo:lo + tm, :]
    y = _silu(acc)
    cbuf[0:SUBLANES] = cbuf[tm:tm + SUBLANES]
    pad = jnp.zeros((tm, A_QK_PAD - A_QK_DIM), F32)
    for hh in range(A_HEADS):
        qh = y[:, hh * A_QK_DIM:(hh + 1) * A_QK_DIM]
        kh = y[:, (A_HEADS + hh) * A_QK_DIM:(A_HEADS + hh + 1) * A_QK_DIM] * (A_QK_DIM ** -0.5)
        q_out[:, hh * A_QK_PAD:(hh + 1) * A_QK_PAD] = jnp.concatenate([qh, pad], axis=1).astype(BF16)
        k_out[:, hh * A_QK_PAD:(hh + 1) * A_QK_PAD] = jnp.concatenate([kh, pad], axis=1).astype(BF16)
    off = qk_w
    v_out[...] = proj(off, MIX_WIDTH).astype(BF16)
    off += MIX_WIDTH
    o_out[...] = _sigmoid(proj(off, MIX_WIDTH))
    xq_out[...] = (_dot(x_bf, wx_ref[...]) * rs).astype(BF16)
    gt_out[...] = _dot(x_bf, wg_ref[...]) * rs


def _proj_mlstm(x, w_bf, conv_w, conv_b, seq, tm):
    t, d = x.shape
    qk_w = 2 * A_HEADS * A_QK_DIM
    g0 = qk_w + 2 * MIX_WIDTH
    n_gates = 2 * A_HEADS
    assert w_bf.shape[1] == g0 + n_gates + XA_WIDTH
    w_g = jnp.pad(w_bf[:, g0:g0 + n_gates], ((0, 0), (0, LANES - n_gates)))
    w_x = w_bf[:, g0 + n_gates:]
    widths = (A_HEADS * A_QK_PAD, A_HEADS * A_QK_PAD, MIX_WIDTH, MIX_WIDTH, XA_WIDTH, LANES)
    dtypes = (BF16, BF16, BF16, F32, BF16, F32)
    return pl.pallas_call(
        functools.partial(_proj_mlstm_kernel, tm=tm, per_seq=seq // tm),
        grid=(t // tm,),
        in_specs=[pl.BlockSpec((tm, d), lambda i: (i, 0)),
                  _resident((d, g0)),
                  _resident((d, XA_WIDTH)),
                  _resident((d, LANES)),
                  _resident((A_CONV, qk_w)),
                  _resident((1, qk_w))],
        out_specs=[pl.BlockSpec((tm, s), lambda i: (i, 0)) for s in widths],
        out_shape=[jax.ShapeDtypeStruct((t, s), dt) for s, dt in zip(widths, dtypes)],
        scratch_shapes=[pltpu.VMEM((tm + SUBLANES, qk_w), F32)],
        compiler_params=_params(1),
        name="proj_mlstm",
    )(x, w_bf, w_x, w_g, conv_w, conv_b)


def _proj_hgrn2_kernel(x_ref, w_ref, lbl_ref, q_out, lf_out, kk_out, v_out, sg_out, xq_out, *, layer):
    lg = lbl_ref[...]
    lg = lg - jnp.max(lg, axis=0, keepdims=True)
    pe = jnp.exp(lg)
    pr = pe / jnp.sum(pe, axis=0, keepdims=True)
    c0 = pr[0:1, :]
    cl = c0
    for r in range(1, layer + 1):
        cl = cl + pr[r:r + 1, :]
    lb = cl - c0
    om = 1.0 - lb

    x_bf, rs = _rms_split(x_ref)
    kw = B_HEADS * B_K_DIM

    def proj(off, width):
        return _dot(x_bf, w_ref[:, off:off + width]) * rs

    q_out[...] = _silu(proj(0, kw))
    fz = proj(kw, kw)
    sig = 1.0 / (1.0 + jnp.exp(-fz))
    gate = lb + om * sig
    lf_out[...] = jnp.where(gate > F32_TINY, jnp.log2(gate), jnp.log2(om) + fz * LOG2_E)
    kk_out[...] = om * (1.0 - sig)
    off = 2 * kw
    v_out[...] = proj(off, MIX_WIDTH).astype(BF16)
    off += MIX_WIDTH
    sg_out[...] = _silu(proj(off, MIX_WIDTH))
    off += MIX_WIDTH
    xq_out[...] = proj(off, XA_WIDTH).astype(BF16)


def _proj_hgrn2(x, w_bf, lb_logits, tm, layer):
    t, d = x.shape
    n = w_bf.shape[1]
    kw = B_HEADS * B_K_DIM
    widths = (kw, kw, kw, MIX_WIDTH, MIX_WIDTH, XA_WIDTH)
    dtypes = (F32, F32, F32, BF16, F32, BF16)
    assert 2 * kw + 2 * MIX_WIDTH + XA_WIDTH == n
    return pl.pallas_call(
        functools.partial(_proj_hgrn2_kernel, layer=layer),
        grid=(t // tm,),
        in_specs=[pl.BlockSpec((tm, d), lambda i: (i, 0)),
                  _resident((d, n)),
                  _resident(lb_logits.shape)],
        out_specs=[pl.BlockSpec((tm, s), lambda i: (i, 0)) for s in widths],
        out_shape=[jax.ShapeDtypeStruct((t, s), dt) for s, dt in zip(widths, dtypes)],
        compiler_params=_params(1),
        name="proj_hgrn2",
    )(x, w_bf, lb_logits)


def _mlstm_kernel(q_ref, k_ref, v_ref, o_ref, gt_ref, gb_ref, hg_ref, tril_ref, out_ref, c_st, m_st, *, tb, hb):
    hgrp = pl.program_id(1)
    t = pl.program_id(2)
    aug_w = A_V_DIM + A_AUG
    rep = aug_w // LANES

    @pl.when(t == 0)
    def _():
        c_st[...] = jnp.zeros_like(c_st)
        m_st[...] = jnp.zeros_like(m_st)

    tril = tril_ref[...]
    row = lax.broadcasted_iota(jnp.int32, (A_CHUNK, A_CHUNK), 0)
    col = lax.broadcasted_iota(jnp.int32, (A_CHUNK, A_CHUNK), 1)
    causal = row >= col
    wide = hb * LANES
    srow = lax.broadcasted_iota(jnp.int32, (LANES, 2 * wide), 0)
    scol = lax.broadcasted_iota(jnp.int32, (LANES, 2 * wide), 1)
    blk = scol // LANES
    want = jnp.where(blk < hb, hgrp * hb + blk, A_HEADS + hgrp * hb + (blk - hb))
    sel = jnp.where(srow == want, 1.0, 0.0).astype(BF16)
    frow = lax.broadcasted_iota(jnp.int32, (A_CHUNK, LANES), 1)
    first = jnp.where(frow == 0, 1.0, 0.0).astype(BF16)
    ones_aug = jnp.ones((A_CHUNK, A_AUG), BF16)
    heads = range(hb)

    def lanes_of(hh):
        return slice(hh * LANES, (hh + 1) * LANES)

    qk_lanes = [slice(hh * A_QK_PAD, (hh + 1) * A_QK_PAD) for hh in heads]
    v_lanes = [slice(hh * A_V_DIM, (hh + 1) * A_V_DIM) for hh in heads]

    def chunk_group(c, carry):
        group = range(A_CHUNKS_PER_ITER)
        rows = [pl.ds(pl.multiple_of((c * A_CHUNKS_PER_ITER + i) * A_CHUNK, A_CHUNK), A_CHUNK) for i in group]
        gates = [gt_ref[r, :] + gb_ref[...] for r in rows]
        g2 = [_dot_exact_rhs(gates[i], sel) for i in group]
        ic = [g2[i][:, :wide] for i in group]
        lf = [_log_sigmoid(g2[i][:, wide:]) for i in group]
        gc = [_dot_exact_lhs(tril, lf[i]) for i in group]
        r_mat = [[_dot_exact_lhs(first, (ic[i] - gc[i])[:, lanes_of(hh)], _NT) for hh in heads]
                 for i in group]
        dmat = [[jnp.where(causal, gc[i][:, hh * LANES:hh * LANES + A_CHUNK] + r_mat[i][hh], -jnp.inf)
                 for hh in heads] for i in group]
        mx = [jnp.concatenate([jnp.broadcast_to(jnp.max(dmat[i][hh], axis=1, keepdims=True), (A_CHUNK, LANES))
                               for hh in heads], axis=1) for i in group]

        m_prev = m_st[0:1, :]
        w_inter, e_neg, p, decay, ws = [], [], [], [], []
        for i in group:
            a = gc[i] + m_prev
            m_row = jnp.maximum(a, mx[i])
            w_inter.append(jnp.exp(a - m_row))
            e_neg.append(jnp.exp(-m_row))
            p.append([jnp.exp(dmat[i][hh] - m_row[:, hh * LANES:hh * LANES + A_CHUNK]) for hh in heads])
            g_end = gc[i][A_CHUNK - 1:A_CHUNK, :]
            a_end = g_end + m_prev
            w_end = g_end - gc[i] + ic[i]
            m_prev = jnp.maximum(a_end, jnp.max(w_end, axis=0, keepdims=True))
            decay.append(jnp.exp(a_end - m_prev))
            ws.append(jnp.exp(w_end - m_prev).astype(BF16))
        m_st[...] = jnp.broadcast_to(m_prev, (SUBLANES, wide))

        q = [[q_ref[r, qk_lanes[hh]] for hh in heads] for r in rows]
        k = [[k_ref[r, qk_lanes[hh]] for hh in heads] for r in rows]
        v_aug = [[jnp.concatenate([v_ref[r, v_lanes[hh]], ones_aug], axis=1) for hh in heads] for r in rows]
        qk = [[_dot(q[i][hh], k[i][hh], _NT) for hh in heads] for i in group]
        intra = [[_dot((qk[i][hh] * p[i][hh]).astype(BF16), v_aug[i][hh]) for hh in heads] for i in group]
        upd = [[_dot(k[i][hh] * jnp.concatenate([ws[i][:, lanes_of(hh)]] * (A_QK_PAD // LANES), axis=1),
                     v_aug[i][hh], _TN) for hh in heads] for i in group]
        state = [c_st[hh] for hh in heads]
        inter = []
        for i in group:
            inter.append([_dot(q[i][hh], state[hh].astype(BF16)) for hh in heads])
            state = [jnp.concatenate([decay[i][:, lanes_of(hh)]] * rep, axis=1) * state[hh] + upd[i][hh]
                     for hh in heads]
        for hh in heads:
            c_st[hh] = state[hh]
        for i in group:
            for hh in heads:
                tot = jnp.concatenate([w_inter[i][:, lanes_of(hh)]] * rep, axis=1) * inter[i][hh] + intra[i][hh]
                den = tot[:, A_V_DIM:]
                inv = 1.0 / jnp.maximum(jnp.abs(den), e_neg[i][:, lanes_of(hh)])
                h_out = tot[:, :A_V_DIM] * jnp.concatenate([inv] * (A_V_DIM // LANES), axis=1)
                ms = jnp.mean(h_out * h_out, axis=-1, keepdims=True)
                hn = h_out * lax.rsqrt(ms + EPS) * hg_ref[:, v_lanes[hh]]
                out_ref[rows[i], v_lanes[hh]] = (o_ref[rows[i], v_lanes[hh]] * hn).astype(out_ref.dtype)
        return carry

    lax.fori_loop(0, tb // (A_CHUNK * A_CHUNKS_PER_ITER), chunk_group, 0)


def _mlstm(zq, zk, zv, zo, zg, gate_b, head_g, batch, seq, tb, hb):
    t_total = batch * seq
    nt = seq // tb
    assert tb % (A_CHUNK * A_CHUNKS_PER_ITER) == 0
    tril = jnp.asarray(np.tril(np.ones((A_CHUNK, A_CHUNK), np.float32)), BF16)
    row_map = lambda b, h, t: (b * nt + t, h)
    head_map = lambda b, h, t: (0, h)
    fixed = lambda b, h, t: (0, 0)
    return pl.pallas_call(
        functools.partial(_mlstm_kernel, tb=tb, hb=hb),
        grid=(batch, A_HEADS // hb, nt),
        in_specs=[pl.BlockSpec((tb, hb * A_QK_PAD), row_map),
                  pl.BlockSpec((tb, hb * A_QK_PAD), row_map),
                  pl.BlockSpec((tb, hb * A_V_DIM), row_map),
                  pl.BlockSpec((tb, hb * A_V_DIM), row_map),
                  pl.BlockSpec((tb, LANES), lambda b, h, t: (b * nt + t, 0)),
                  pl.BlockSpec((1, LANES), fixed),
                  pl.BlockSpec((1, hb * A_V_DIM), head_map),
                  pl.BlockSpec((A_CHUNK, A_CHUNK), fixed)],
        out_specs=pl.BlockSpec((tb, hb * A_V_DIM), row_map),
        out_shape=jax.ShapeDtypeStruct((t_total, MIX_WIDTH), BF16),
        scratch_shapes=[pltpu.VMEM((hb, A_QK_PAD, A_V_DIM + A_AUG), F32),
                        pltpu.VMEM((SUBLANES, hb * LANES), F32)],
        compiler_params=_params(3),
        name="mlstm",
    )(zq, zk, zv, zo, zg, gate_b, head_g, tril)


def _hgrn2_pair_masks():
    n = CHUNK
    masks = [np.eye(n, dtype=np.float32)]
    for j in range(1, B_LEVELS + 1):
        c = n >> j
        pm = np.zeros((n, n), np.float32)
        for r in range(n):
            if r % (2 * c) >= c:
                mid = (r // (2 * c)) * 2 * c + c
                pm[r, mid - c:mid] = 1.0
        masks.append(pm)
    return np.stack(masks, axis=0)


def _hgrn2_kernel(q_ref, lf_ref, kk_ref, v_ref, sg_ref, hg_ref, tril_ref, mask_ref, out_ref, s_st, *, tb, hb):
    t = pl.program_id(2)

    @pl.when(t == 0)
    def _():
        s_st[...] = jnp.zeros_like(s_st)

    tril = tril_ref[...]
    wide = hb * B_K_DIM
    heads = range(hb)
    rowi = lax.broadcasted_iota(jnp.int32, (CHUNK, wide), 0)
    sub = lax.broadcasted_iota(jnp.int32, (CHUNK // SUBLANES, SUBLANES, wide), 1)

    def level_operand(q, kk, gc, lf, half):
        if half >= SUBLANES:
            pieces = []
            for b0 in range(0, CHUNK, 2 * half):
                mid = b0 + half
                ref = gc[mid - 1:mid, :]
                pieces.append(kk[b0:mid, :] * jnp.exp2(ref - gc[b0:mid, :]))
                pieces.append(q[mid:mid + half, :] * jnp.exp2(gc[mid:mid + half, :] - ref))
            return jnp.concatenate(pieces, axis=0)
        if half == 1:
            odd = (rowi & 1) != 0
            return jnp.where(odd, q * jnp.exp2(lf), kk)
        g3 = gc.reshape(CHUNK // SUBLANES, SUBLANES, wide)
        ref = g3[:, half - 1:half, :]
        for b0 in range(2 * half, SUBLANES, 2 * half):
            ref = jnp.where(sub >= b0, g3[:, b0 + half - 1:b0 + half, :], ref)
        d = g3 - ref
        upper = (sub & half) != 0
        ex = jnp.exp2(jnp.where(upper, d, -d)).reshape(CHUNK, wide)
        return jnp.where((rowi & half) != 0, q, kk) * ex

    def hl(x, hh):
        return x[:, hh * B_K_DIM:(hh + 1) * B_K_DIM]

    def chunk_group(c, carry):
        group = range(B_CHUNKS_PER_ITER)
        rows = [pl.ds(pl.multiple_of((c * B_CHUNKS_PER_ITER + i) * CHUNK, CHUNK), CHUNK) for i in group]
        q = [q_ref[r, :] for r in rows]
        lf = [lf_ref[r, :] for r in rows]
        kk = [kk_ref[r, :] for r in rows]
        v = [v_ref[r, :] for r in rows]
        gc = [_dot_exact_lhs(tril, lf[i]) for i in group]
        g_end = [gc[i][CHUNK - 1:CHUNK, :] for i in group]

        xs = [[level_operand(q[i], kk[i], gc[i], lf[i], CHUNK >> j).astype(BF16) for j in range(1, B_LEVELS + 1)]
              for i in group]
        q_bf = [q[i].astype(BF16) for i in group]
        k_bf = [kk[i].astype(BF16) for i in group]
        qg = [(q[i] * jnp.exp2(gc[i])).astype(BF16) for i in group]
        kd = [(kk[i] * jnp.exp2(g_end[i] - gc[i])).astype(BF16) for i in group]
        dec = [jnp.exp2(g_end[i]) for i in group]

        prod = [[[_dot(hl(q_bf[i], hh), hl(k_bf[i], hh), _NT)] + [_dot(hl(x, hh), hl(x, hh), _NT) for x in xs[i]]
                 for hh in heads] for i in group]
        upd = [[_dot(hl(v[i], hh), hl(kd[i], hh), _TN) for hh in heads] for i in group]
        state = [s_st[hh] for hh in heads]
        o_inter = []
        for i in group:
            o_inter.append([_dot(hl(qg[i], hh), state[hh].astype(BF16), _NT) for hh in heads])
            state = [state[hh] * hl(dec[i], hh) + upd[i][hh] for hh in heads]
        for hh in heads:
            s_st[hh] = state[hh]
        for i in group:
            outs = []
            for hh in heads:
                acc = mask_ref[0] * prod[i][hh][0]
                for j in range(1, B_LEVELS + 1):
                    acc = acc + mask_ref[j] * prod[i][hh][j]
                o = _dot(acc.astype(BF16), hl(v[i], hh)) + o_inter[i][hh]
                ms = jnp.mean(o * o, axis=-1, keepdims=True)
                outs.append(o * lax.rsqrt(ms + EPS))
            on = jnp.concatenate(outs, axis=1) * hg_ref[...]
            out_ref[rows[i], :] = (on * sg_ref[rows[i], :]).astype(out_ref.dtype)
        return carry

    lax.fori_loop(0, tb // (CHUNK * B_CHUNKS_PER_ITER), chunk_group, 0)


def _hgrn2(zq, zlf, zkk, zv, zsg, head_g, batch, seq, tb, hb):
    t_total = batch * seq
    nt = seq // tb
    assert tb % (CHUNK * B_CHUNKS_PER_ITER) == 0
    mask_np = _hgrn2_pair_masks()
    tril = jnp.asarray(np.tril(np.ones((CHUNK, CHUNK), np.float32)), BF16)
    row_map = lambda b, h, t: (b * nt + t, h)
    head_map = lambda b, h, t: (0, h)
    return pl.pallas_call(
        functools.partial(_hgrn2_kernel, tb=tb, hb=hb),
        grid=(batch, B_HEADS // hb, nt),
        in_specs=[pl.BlockSpec((tb, hb * B_K_DIM), row_map),
                  pl.BlockSpec((tb, hb * B_K_DIM), row_map),
                  pl.BlockSpec((tb, hb * B_K_DIM), row_map),
                  pl.BlockSpec((tb, hb * B_V_DIM), row_map),
                  pl.BlockSpec((tb, hb * B_V_DIM), row_map),
                  pl.BlockSpec((1, hb * B_V_DIM), head_map),
                  pl.BlockSpec((CHUNK, CHUNK), lambda b, h, t: (0, 0)),
                  pl.BlockSpec(mask_np.shape, lambda b, h, t: (0, 0, 0))],
        out_specs=pl.BlockSpec((tb, hb * B_V_DIM), row_map),
        out_shape=jax.ShapeDtypeStruct((t_total, MIX_WIDTH), BF16),
        scratch_shapes=[pltpu.VMEM((hb, B_V_DIM, B_K_DIM), F32)],
        compiler_params=_params(3),
        name="hgrn2",
    )(zq, zlf, zkk, zv, zsg, head_g, tril, jnp.asarray(mask_np, F32))


def _outproj_kernel(y_ref, xq_ref, kv_ref, wo_ref, h_ref, out_ref, hn_ref, rs_ref):
    kv = kv_ref[...]
    parts = []
    for hh in range(XA_HEADS):
        lo = hh * XA_HEAD_DIM
        qh = xq_ref[:, lo:lo + XA_HEAD_DIM]
        kh = kv[:, lo:lo + XA_HEAD_DIM].astype(BF16)
        vh = kv[:, XA_WIDTH + lo:XA_WIDTH + lo + XA_HEAD_DIM].astype(BF16)
        s = _dot(qh, kh, _NT) * (XA_HEAD_DIM ** -0.5)
        e = jnp.exp(s - jnp.max(s, axis=-1, keepdims=True))
        den = jnp.sum(e, axis=-1, keepdims=True)
        parts.append((_dot(e.astype(BF16), vh) / den).astype(BF16))
    y_mem = jnp.concatenate(parts, axis=1)
    h_new = (h_ref[...] + _dot(y_ref[...], wo_ref[0:MIX_WIDTH, :])
             + _dot(y_mem, wo_ref[MIX_WIDTH:D_MODEL, :]))
    out_ref[...] = h_new
    hn_ref[...] = h_new.astype(BF16)
    rs = lax.rsqrt(jnp.mean(h_new * h_new, axis=-1, keepdims=True) + EPS)
    rs_ref[...] = jnp.broadcast_to(rs, rs_ref.shape)


def _outproj(y_mix, zxq, kv_all, layer, w_out_bf, h, seq, mem_len, tm):
    t = h.shape[0]
    per_seq = seq // tm
    return pl.pallas_call(
        _outproj_kernel,
        grid=(t // tm,),
        in_specs=[pl.BlockSpec((tm, MIX_WIDTH), lambda i: (i, 0)),
                  pl.BlockSpec((tm, XA_WIDTH), lambda i: (i, 0)),
                  pl.BlockSpec((mem_len, 2 * XA_WIDTH), lambda i: (i // per_seq, layer)),
                  _resident((D_MODEL, D_MODEL)),
                  pl.BlockSpec((tm, D_MODEL), lambda i: (i, 0))],
        out_specs=[pl.BlockSpec((tm, D_MODEL), lambda i: (i, 0)),
                   pl.BlockSpec((tm, D_MODEL), lambda i: (i, 0)),
                   pl.BlockSpec((tm, LANES), lambda i: (i, 0))],
        out_shape=[jax.ShapeDtypeStruct((t, D_MODEL), F32),
                   jax.ShapeDtypeStruct((t, D_MODEL), BF16),
                   jax.ShapeDtypeStruct((t, LANES), F32)],
        compiler_params=_params(1),
        name="outproj",
    )(y_mix, zxq, kv_all, w_out_bf, h)


def _ffn_kernel(hn_ref, rs_ref, h_ref, wu_ref, wg_ref, cw_ref, cb_ref, wd_ref, out_ref,
                gbuf, gcarry, act_s, *, tm, nf, per_seq):
    i = pl.program_id(0)
    j = pl.program_id(1)
    tf = act_s.shape[2]

    @pl.when(j < nf)
    def _():
        hn = hn_ref[...]
        rs = jnp.concatenate([rs_ref[...]] * (tf // LANES), axis=1)
        g = _dot(hn, wg_ref[...]) * rs
        prev = jnp.where(i % per_seq == 0, 0.0, gcarry[j])
        gbuf[0:SUBLANES] = prev
        gbuf[SUBLANES:SUBLANES + tm] = g
        gcarry[j] = g[tm - SUBLANES:tm, :]
        gc = cb_ref[...]
        for k in range(FFN_CONV):
            lo = SUBLANES - (FFN_CONV - 1) + k
            gc = gc + cw_ref[k:k + 1, :] * gbuf[lo:lo + tm, :]
        sg = _silu(gc) * rs
        u = _dot(hn, wu_ref[...])
        act_s[j] = (sg * u).astype(BF16)

    @pl.when(j >= nf)
    def _():
        act = jnp.concatenate([act_s[f] for f in range(nf)], axis=1)
        out_ref[...] = h_ref[...] + _dot(act, wd_ref[...])


def _ffn(hn, rs, h, w_up_bf, conv_w, conv_b, w_down_all_bf, layer, seq, tm, tf, tn):
    t, d = h.shape
    nf = D_FF // tf
    nd = d // tn
    per_seq = seq // tm
    up = lambda j: jnp.minimum(j, nf - 1)
    down = lambda j: jnp.maximum(j - nf, 0)
    return pl.pallas_call(
        functools.partial(_ffn_kernel, tm=tm, nf=nf, per_seq=per_seq),
        grid=(t // tm, nf + nd),
        in_specs=[pl.BlockSpec((tm, d), lambda i, j: (i, 0)),
                  pl.BlockSpec((tm, LANES), lambda i, j: (i, 0)),
                  pl.BlockSpec((tm, tn), lambda i, j: (i, down(j))),
                  pl.BlockSpec((d, tf), lambda i, j: (0, up(j))),
                  pl.BlockSpec((d, tf), lambda i, j: (0, nf + up(j))),
                  pl.BlockSpec((FFN_CONV, tf), lambda i, j: (0, up(j))),
                  pl.BlockSpec((1, tf), lambda i, j: (0, up(j))),
                  pl.BlockSpec((None, D_FF, tn), lambda i, j: (layer, 0, down(j)))],
        out_specs=pl.BlockSpec((tm, tn), lambda i, j: (i, down(j))),
        out_shape=jax.ShapeDtypeStruct((t, d), F32),
        scratch_shapes=[pltpu.VMEM((tm + SUBLANES, tf), F32),
                        pltpu.VMEM((nf, SUBLANES, tf), F32),
                        pltpu.VMEM((nf, tm, tf), BF16)],
        compiler_params=_params(2),
        name="ffn",
    )(hn, rs, h, w_up_bf, w_up_bf, conv_w, conv_b.reshape(1, D_FF), w_down_all_bf)


def _final_norm_kernel(x_ref, g_ref, out_ref):
    x = x_ref[...]
    out_ref[...] = x * lax.rsqrt(jnp.mean(x * x, axis=-1, keepdims=True) + EPS) * g_ref[...]


def _final_norm(x, g, tm):
    t, d = x.shape
    return pl.pallas_call(
        _final_norm_kernel,
        grid=(t // tm,),
        in_specs=[pl.BlockSpec((tm, d), lambda i: (i, 0)),
                  pl.BlockSpec((1, d), lambda i: (0, 0))],
        out_specs=pl.BlockSpec((tm, d), lambda i: (i, 0)),
        out_shape=jax.ShapeDtypeStruct((t, d), F32),
        compiler_params=_params(1),
        name="final_norm",
    )(x, g.reshape(1, d))


def _pick(n, cap):
    b = min(n, cap)
    while n % b:
        b //= 2
    return b


def kernel(x, mem, norm_mix_g, norm_mem_g, norm_ffn_g, norm_out_g, w_mem_kv, a_w_in, a_gate_b, a_conv_w, a_conv_b, a_head_g, a_w_out, b_w_in, b_lb_logits, b_head_g, b_w_out, ffn_w_up, ffn_conv_w, ffn_conv_b, ffn_w_down):
    batch, seq, d = x.shape
    mem_len = mem.shape[1]
    depth = norm_mix_g.shape[0]
    t = batch * seq
    tm_proj = _pick(seq, TM_PROJ)
    tm_out = _pick(seq, TM_OUT)
    tm_ffn = _pick(seq, TM_FFN)
    tb = _pick(seq, TB_REC)

    h = x.reshape(t, d)
    w_kv_all = jnp.concatenate([_gain_folded(w_mem_kv[layer], norm_mem_g[layer]) for layer in range(depth)], axis=1)
    kv_all = _norm_matmul(mem.reshape(batch * mem_len, d), w_kv_all, _pick(mem_len, TM_PROJ))
    w_down_all = ffn_w_down.astype(BF16)
    for layer in range(depth):
        j = layer // N_MIXERS
        if layer % N_MIXERS == 0:
            zq, zk, zv, zo, zxq, zg = _proj_mlstm(h, _gain_folded(a_w_in[j], norm_mix_g[layer]), a_conv_w[j],
                                                  a_conv_b[j][None, :], seq, tm_proj)
            gb = jnp.pad(a_gate_b[j][None, :], ((0, 0), (0, LANES - 2 * A_HEADS)))
            y_mix = _mlstm(zq, zk, zv, zo, zg, gb, a_head_g[j].reshape(1, MIX_WIDTH), batch, seq, tb,
                           A_HEADS_PER_STEP)
            w_out = a_w_out[j]
        else:
            zq, zlf, zkk, zv, zsg, zxq = _proj_hgrn2(h, _gain_folded(b_w_in[j], norm_mix_g[layer]), b_lb_logits,
                                                     tm_proj, layer)
            y_mix = _hgrn2(zq, zlf, zkk, zv, zsg, b_head_g[j].reshape(1, MIX_WIDTH), batch, seq, tb,
                           B_HEADS_PER_STEP)
            w_out = b_w_out[j]
        h, hn, rs = _outproj(y_mix, zxq, kv_all, layer, w_out.astype(BF16), h, seq, mem_len, tm_out)
        h = _ffn(hn, rs, h, _gain_folded(ffn_w_up[layer], norm_ffn_g[layer]), ffn_conv_w[layer], ffn_conv_b[layer],
                 w_down_all, layer, seq, tm_ffn, FFN_TF, FFN_TN)
    return _final_norm(h, norm_out_g, tm_out).reshape(batch, seq, d)
```

```python
import functools

import numpy as np
import jax
import jax.numpy as jnp
from jax import lax
from jax.experimental import pallas as pl
from jax.experimental.pallas import tpu as pltpu

F32 = jnp.float32
BF16 = jnp.bfloat16

D_MODEL = 2048
N_MIXERS = 2
CHUNK = 64
EPS = 1e-6
LOG2_E = 1.4426950408889634
F32_TINY = 1e-37

XA_HEADS = 4
XA_WIDTH = D_MODEL // 4
XA_HEAD_DIM = XA_WIDTH // XA_HEADS
MIX_WIDTH = D_MODEL - XA_WIDTH

A_HEADS = 4
A_V_DIM = MIX_WIDTH // A_HEADS
A_QK_DIM = A_V_DIM // 2
A_QK_PAD = 256
A_CONV = 4
A_AUG = 128
A_HEADS_PER_STEP = 4
A_CHUNK = 128
A_CHUNKS_PER_ITER = 2
B_HEADS = 12
B_K_DIM = 128
B_V_DIM = 128
B_LEVELS = 6
B_HEADS_PER_STEP = 12
B_CHUNKS_PER_ITER = 2

D_FF = 5632
FFN_CONV = 3

LANES = 128
SUBLANES = 8
VMEM_LIMIT_BYTES = 56 * 1024 * 1024

TM_PROJ = 256
TM_OUT = 512
TM_FFN = 1024
TB_REC = 512
FFN_TF = 512
FFN_TN = 512

_NN = (((1,), (0,)), ((), ()))
_NT = (((1,), (1,)), ((), ()))
_TN = (((0,), (0,)), ((), ()))


def _dot(a, b, dims=_NN):
    return lax.dot_general(a, b, dims, preferred_element_type=F32)


def _split2(x):
    hi = x.astype(BF16)
    return hi, (x - hi.astype(F32)).astype(BF16)


def _dot_exact_lhs(a_bf, x, dims=_NN):
    return sum(_dot(a_bf, p, dims) for p in _split2(x))


def _dot_exact_rhs(x, b_bf, dims=_NN):
    return sum(_dot(p, b_bf, dims) for p in _split2(x))


def _sigmoid(x):
    return 0.5 + 0.5 * jnp.tanh(0.5 * x)


def _silu(x):
    hx = 0.5 * x
    return hx + hx * jnp.tanh(hx)


def _log_sigmoid(x):
    return jnp.minimum(x, 0.0) - jnp.log(1.0 + jnp.exp(-jnp.abs(x)))


def _params(n_grid):
    return pltpu.CompilerParams(dimension_semantics=("arbitrary",) * n_grid,
                                vmem_limit_bytes=VMEM_LIMIT_BYTES)


def _resident(shape):
    nd = len(shape)
    return pl.BlockSpec(shape, lambda *_: (0,) * nd, pipeline_mode=pl.Buffered(1))


def _gain_folded(w, g):
    return (g[:, None] * w).astype(BF16)


def _rms_split(x_ref):
    x = x_ref[...]
    return x.astype(BF16), lax.rsqrt(jnp.mean(x * x, axis=-1, keepdims=True) + EPS)


def _norm_matmul_kernel(x_ref, w_ref, out_ref):
    x_bf, rs = _rms_split(x_ref)
    out_ref[...] = _dot(x_bf, w_ref[...]) * rs


def _norm_matmul(x, wg_bf, tm):
    t, d = x.shape
    n = wg_bf.shape[1]
    return pl.pallas_call(
        _norm_matmul_kernel,
        grid=(t // tm,),
        in_specs=[pl.BlockSpec((tm, d), lambda i: (i, 0)),
                  _resident((d, n))],
        out_specs=pl.BlockSpec((tm, n), lambda i: (i, 0)),
        out_shape=jax.ShapeDtypeStruct((t, n), F32),
        compiler_params=_params(1),
        name="norm_matmul",
    )(x, wg_bf)


def _proj_mlstm_kernel(x_ref, w_ref, wx_ref, wg_ref, cw_ref, cb_ref, q_out, k_out, v_out, o_out, xq_out, gt_out,
                       cbuf, *, tm, per_seq):
    i = pl.program_id(0)
    qk_w = 2 * A_HEADS * A_QK_DIM

    @pl.when(i % per_seq == 0)
    def _():
        cbuf[0:SUBLANES] = jnp.zeros((SUBLANES, qk_w), F32)

    x_bf, rs = _rms_split(x_ref)

    def proj(off, width):
        return _dot(x_bf, w_ref[:, off:off + width]) * rs

    cbuf[SUBLANES:SUBLANES + tm] = proj(0, qk_w)
    acc = cb_ref[...]
    for j in range(A_CONV):
        lo = SUBLANES - (A_CONV - 1) + j
        acc = acc + cw_ref[j:j + 1, :] * cbuf[lo:lo + tm, :]
    y = _silu(acc)
    cbuf[0:SUBLANES] = cbuf[tm:tm + SUBLANES]
    pad = jnp.zeros((tm, A_QK_PAD - A_QK_DIM), F32)
    for hh in range(A_HEADS):
        qh = y[:, hh * A_QK_DIM:(hh + 1) * A_QK_DIM]
        kh = y[:, (A_HEADS + hh) * A_QK_DIM:(A_HEADS + hh + 1) * A_QK_DIM] * (A_QK_DIM ** -0.5)
        q_out[:, hh * A_QK_PAD:(hh + 1) * A_QK_PAD] = jnp.concatenate([qh, pad], axis=1).astype(BF16)
        k_out[:, hh * A_QK_PAD:(hh + 1) * A_QK_PAD] = jnp.concatenate([kh, pad], axis=1).astype(BF16)
    off = qk_w
    v_out[...] = proj(off, MIX_WIDTH).astype(BF16)
    off += MIX_WIDTH
    o_out[...] = _sigmoid(proj(off, MIX_WIDTH))
    xq_out[...] = (_dot(x_bf, wx_ref[...]) * rs).astype(BF16)
    gt_out[...] = _dot(x_bf, wg_ref[...]) * rs


def _proj_mlstm(x, w_bf, conv_w, conv_b, seq, tm):
    t, d = x.shape
    qk_w = 2 * A_HEADS * A_QK_DIM
    g0 = qk_w + 2 * MIX_WIDTH
    n_gates = 2 * A_HEADS
    assert w_bf.shape[1] == g0 + n_gates + XA_WIDTH
    w_g = jnp.pad(w_bf[:, g0:g0 + n_gates], ((0, 0), (0, LANES - n_gates)))
    w_x = w_bf[:, g0 + n_gates:]
    widths = (A_HEADS * A_QK_PAD, A_HEADS * A_QK_PAD, MIX_WIDTH, MIX_WIDTH, XA_WIDTH, LANES)
    dtypes = (BF16, BF16, BF16, F32, BF16, F32)
    return pl.pallas_call(
        functools.partial(_proj_mlstm_kernel, tm=tm, per_seq=seq // tm),
        grid=(t // tm,),
        in_specs=[pl.BlockSpec((tm, d), lambda i: (i, 0)),
                  _resident((d, g0)),
                  _resident((d, XA_WIDTH)),
                  _resident((d, LANES)),
                  _resident((A_CONV, qk_w)),
                  _resident((1, qk_w))],
        out_specs=[pl.BlockSpec((tm, s), lambda i: (i, 0)) for s in widths],
        out_shape=[jax.ShapeDtypeStruct((t, s), dt) for s, dt in zip(widths, dtypes)],
        scratch_shapes=[pltpu.VMEM((tm + SUBLANES, qk_w), F32)],
        compiler_params=_params(1),
        name="proj_mlstm",
    )(x, w_bf, w_x, w_g, conv_w, conv_b)


def _proj_hgrn2_kernel(x_ref, w_ref, lbl_ref, q_out, lf_out, kk_out, v_out, sg_out, xq_out, *, layer):
    lg = lbl_ref[...]
    lg = lg - jnp.max(lg, axis=0, keepdims=True)
    pe = jnp.exp(lg)
    pr = pe / jnp.sum(pe, axis=0, keepdims=True)
    c0 = pr[0:1, :]
    cl = c0
    for r in range(1, layer + 1):
        cl = cl + pr[r:r + 1, :]
    lb = cl - c0
    om = 1.0 - lb

    x_bf, rs = _rms_split(x_ref)
    kw = B_HEADS * B_K_DIM

    def proj(off, width):
        return _dot(x_bf, w_ref[:, off:off + width]) * rs

    q_out[...] = _silu(proj(0, kw))
    fz = proj(kw, kw)
    sig = 1.0 / (1.0 + jnp.exp(-fz))
    gate = lb + om * sig
    lf_out[...] = jnp.where(gate > F32_TINY, jnp.log2(gate), jnp.log2(om) + fz * LOG2_E)
    kk_out[...] = om * (1.0 - sig)
    off = 2 * kw
    v_out[...] = proj(off, MIX_WIDTH).astype(BF16)
    off += MIX_WIDTH
    sg_out[...] = _silu(proj(off, MIX_WIDTH))
    off += MIX_WIDTH
    xq_out[...] = proj(off, XA_WIDTH).astype(BF16)


def _proj_hgrn2(x, w_bf, lb_logits, tm, layer):
    t, d = x.shape
    n = w_bf.shape[1]
    kw = B_HEADS * B_K_DIM
    widths = (kw, kw, kw, MIX_WIDTH, MIX_WIDTH, XA_WIDTH)
    dtypes = (F32, F32, F32, BF16, F32, BF16)
    assert 2 * kw + 2 * MIX_WIDTH + XA_WIDTH == n
    return pl.pallas_call(
        functools.partial(_proj_hgrn2_kernel, layer=layer),
        grid=(t // tm,),
        in_specs=[pl.BlockSpec((tm, d), lambda i: (i, 0)),
                  _resident((d, n)),
                  _resident(lb_logits.shape)],
        out_specs=[pl.BlockSpec((tm, s), lambda i: (i, 0)) for s in widths],
        out_shape=[jax.ShapeDtypeStruct((t, s), dt) for s, dt in zip(widths, dtypes)],
        compiler_params=_params(1),
        name="proj_hgrn2",
    )(x, w_bf, lb_logits)


def _mlstm_kernel(q_ref, k_ref, v_ref, o_ref, gt_ref, gb_ref, hg_ref, tril_ref, out_ref, c_st, m_st, *, tb, hb):
    hgrp = pl.program_id(1)
    t = pl.program_id(2)
    aug_w = A_V_DIM + A_AUG
    rep = aug_w // LANES

    @pl.when(t == 0)
    def _():
        c_st[...] = jnp.zeros_like(c_st)
        m_st[...] = jnp.zeros_like(m_st)

    tril = tril_ref[...]
    row = lax.broadcasted_iota(jnp.int32, (A_CHUNK, A_CHUNK), 0)
    col = lax.broadcasted_iota(jnp.int32, (A_CHUNK, A_CHUNK), 1)
    causal = row >= col
    wide = hb * LANES
    srow = lax.broadcasted_iota(jnp.int32, (LANES, 2 * wide), 0)
    scol = lax.broadcasted_iota(jnp.int32, (LANES, 2 * wide), 1)
    blk = scol // LANES
    want = jnp.where(blk < hb, hgrp * hb + blk, A_HEADS + hgrp * hb + (blk - hb))
    sel = jnp.where(srow == want, 1.0, 0.0).astype(BF16)
    frow = lax.broadcasted_iota(jnp.int32, (A_CHUNK, LANES), 1)
    first = jnp.where(frow == 0, 1.0, 0.0).astype(BF16)
    ones_aug = jnp.ones((A_CHUNK, A_AUG), BF16)
    heads = range(hb)

    def lanes_of(hh):
        return slice(hh * LANES, (hh + 1) * LANES)

    qk_lanes = [slice(hh * A_QK_PAD, hh * A_QK_PAD + A_QK_DIM) for hh in heads]
    v_lanes = [slice(hh * A_V_DIM, (hh + 1) * A_V_DIM) for hh in heads]

    def chunk_group(c, carry):
        group = range(A_CHUNKS_PER_ITER)
        rows = [pl.ds(pl.multiple_of((c * A_CHUNKS_PER_ITER + i) * A_CHUNK, A_CHUNK), A_CHUNK) for i in group]
        gates = [gt_ref[r, :] + gb_ref[...] for r in rows]
        g2 = [_dot_exact_rhs(gates[i], sel) for i in group]
        ic = [g2[i][:, :wide] for i in group]
        lf = [_log_sigmoid(g2[i][:, wide:]) for i in group]
        gc = [_dot_exact_lhs(tril, lf[i]) for i in group]
        r_mat = [[_dot_exact_lhs(first, (ic[i] - gc[i])[:, lanes_of(hh)], _NT) for hh in heads]
                 for i in group]
        dmat = [[jnp.where(causal, gc[i][:, hh * LANES:hh * LANES + A_CHUNK] + r_mat[i][hh], -jnp.inf)
                 for hh in heads] for i in group]
        mx = [jnp.concatenate([jnp.broadcast_to(jnp.max(dmat[i][hh], axis=1, keepdims=True), (A_CHUNK, LANES))
                               for hh in heads], axis=1) for i in group]

        m_prev = m_st[0:1, :]
        w_inter, e_neg, p, decay, ws = [], [], [], [], []
        for i in group:
            a = gc[i] + m_prev
            m_row = jnp.maximum(a, mx[i])
            w_inter.append(jnp.exp(a - m_row))
            e_neg.append(jnp.exp(-m_row))
            p.append([jnp.exp(dmat[i][hh] - m_row[:, hh * LANES:hh * LANES + A_CHUNK]) for hh in heads])
            g_end = gc[i][A_CHUNK - 1:A_CHUNK, :]
            a_end = g_end + m_prev
            w_end = g_end - gc[i] + ic[i]
            m_prev = jnp.maximum(a_end, jnp.max(w_end, axis=0, keepdims=True))
            decay.append(jnp.exp(a_end - m_prev))
            ws.append(jnp.exp(w_end - m_prev).astype(BF16))
        m_st[...] = jnp.broadcast_to(m_prev, (SUBLANES, wide))

        q = [[q_ref[r, qk_lanes[hh]] for hh in heads] for r in rows]
        k = [[k_ref[r, qk_lanes[hh]] for hh in heads] for r in rows]
        v_aug = [[jnp.concatenate([v_ref[r, v_lanes[hh]], ones_aug], axis=1) for hh in heads] for r in rows]
        qk = [[_dot(q[i][hh], k[i][hh], _NT) for hh in heads] for i in group]
        intra = [[_dot((qk[i][hh] * p[i][hh]).astype(BF16), v_aug[i][hh]) for hh in heads] for i in group]
        upd = [[_dot(k[i][hh] * jnp.concatenate([ws[i][:, lanes_of(hh)]] * (A_QK_PAD // LANES), axis=1)[:, :A_QK_DIM],
                     v_aug[i][hh], _TN) for hh in heads] for i in group]
        state = [c_st[hh] for hh in heads]
        inter = []
        for i in group:
            inter.append([_dot(q[i][hh], state[hh].astype(BF16)) for hh in heads])
            state = [jnp.concatenate([decay[i][:, lanes_of(hh)]] * rep, axis=1) * state[hh] + upd[i][hh]
                     for hh in heads]
        for hh in heads:
            c_st[hh] = state[hh]
        for i in group:
            for hh in heads:
                tot = jnp.concatenate([w_inter[i][:, lanes_of(hh)]] * rep, axis=1) * inter[i][hh] + intra[i][hh]
                den = tot[:, A_V_DIM:]
                inv = 1.0 / jnp.maximum(jnp.abs(den), e_neg[i][:, lanes_of(hh)])
                h_out = tot[:, :A_V_DIM] * jnp.concatenate([inv] * (A_V_DIM // LANES), axis=1)
                ms = jnp.mean(h_out * h_out, axis=-1, keepdims=True)
                hn = h_out * lax.rsqrt(ms + EPS) * hg_ref[:, v_lanes[hh]]
                out_ref[rows[i], v_lanes[hh]] = (o_ref[rows[i], v_lanes[hh]] * hn).astype(out_ref.dtype)
        return carry

    lax.fori_loop(0, tb // (A_CHUNK * A_CHUNKS_PER_ITER), chunk_group, 0)


def _mlstm(zq, zk, zv, zo, zg, gate_b, head_g, batch, seq, tb, hb):
    t_total = batch * seq
    nt = seq // tb
    assert tb % (A_CHUNK * A_CHUNKS_PER_ITER) == 0
    tril = jnp.asarray(np.tril(np.ones((A_CHUNK, A_CHUNK), np.float32)), BF16)
    row_map = lambda b, h, t: (b * nt + t, h)
    head_map = lambda b, h, t: (0, h)
    fixed = lambda b, h, t: (0, 0)
    return pl.pallas_call(
        functools.partial(_mlstm_kernel, tb=tb, hb=hb),
        grid=(batch, A_HEADS // hb, nt),
        in_specs=[pl.BlockSpec((tb, hb * A_QK_PAD), row_map),
                  pl.BlockSpec((tb, hb * A_QK_PAD), row_map),
                  pl.BlockSpec((tb, hb * A_V_DIM), row_map),
                  pl.BlockSpec((tb, hb * A_V_DIM), row_map),
                  pl.BlockSpec((tb, LANES), lambda b, h, t: (b * nt + t, 0)),
                  pl.BlockSpec((1, LANES), fixed),
                  pl.BlockSpec((1, hb * A_V_DIM), head_map),
                  pl.BlockSpec((A_CHUNK, A_CHUNK), fixed)],
        out_specs=pl.BlockSpec((tb, hb * A_V_DIM), row_map),
        out_shape=jax.ShapeDtypeStruct((t_total, MIX_WIDTH), BF16),
        scratch_shapes=[pltpu.VMEM((hb, A_QK_DIM, A_V_DIM + A_AUG), F32),
                        pltpu.VMEM((SUBLANES, hb * LANES), F32)],
        compiler_params=_params(3),
        name="mlstm",
    )(zq, zk, zv, zo, zg, gate_b, head_g, tril)


def _hgrn2_pair_masks():
    n = CHUNK
    masks = [np.eye(n, dtype=np.float32)]
    for j in range(1, B_LEVELS + 1):
        c = n >> j
        pm = np.zeros((n, n), np.float32)
        for r in range(n):
            if r % (2 * c) >= c:
                mid = (r // (2 * c)) * 2 * c + c
                pm[r, mid - c:mid] = 1.0
        masks.append(pm)
    return np.stack(masks, axis=0)


def _hgrn2_kernel(q_ref, lf_ref, kk_ref, v_ref, sg_ref, hg_ref, tril_ref, mask_ref, out_ref, s_st, *, tb, hb):
    t = pl.program_id(2)

    @pl.when(t == 0)
    def _():
        s_st[...] = jnp.zeros_like(s_st)

    tril = tril_ref[...]
    wide = hb * B_K_DIM
    heads = range(hb)
    rowi = lax.broadcasted_iota(jnp.int32, (CHUNK, wide), 0)
    sub = lax.broadcasted_iota(jnp.int32, (CHUNK // SUBLANES, SUBLANES, wide), 1)

    def level_operand(q, kk, gc, lf, half):
        if half >= SUBLANES:
            pieces = []
            for b0 in range(0, CHUNK, 2 * half):
                mid = b0 + half
                ref = gc[mid - 1:mid, :]
                pieces.append(kk[b0:mid, :] * jnp.exp2(ref - gc[b0:mid, :]))
                pieces.append(q[mid:mid + half, :] * jnp.exp2(gc[mid:mid + half, :] - ref))
            return jnp.concatenate(pieces, axis=0)
        if half == 1:
            odd = (rowi & 1) != 0
            return jnp.where(odd, q * jnp.exp2(lf), kk)
        g3 = gc.reshape(CHUNK // SUBLANES, SUBLANES, wide)
        ref = g3[:, half - 1:half, :]
        for b0 in range(2 * half, SUBLANES, 2 * half):
            ref = jnp.where(sub >= b0, g3[:, b0 + half - 1:b0 + half, :], ref)
        d = g3 - ref
        upper = (sub & half) != 0
        ex = jnp.exp2(jnp.where(upper, d, -d)).reshape(CHUNK, wide)
        return jnp.where((rowi & half) != 0, q, kk) * ex

    def hl(x, hh):
        return x[:, hh * B_K_DIM:(hh + 1) * B_K_DIM]

    def chunk_group(c, carry):
        group = range(B_CHUNKS_PER_ITER)
        rows = [pl.ds(pl.multiple_of((c * B_CHUNKS_PER_ITER + i) * CHUNK, CHUNK), CHUNK) for i in group]
        q = [q_ref[r, :] for r in rows]
        lf = [lf_ref[r, :] for r in rows]
        kk = [kk_ref[r, :] for r in rows]
        v = [v_ref[r, :] for r in rows]
        gc = [_dot_exact_lhs(tril, lf[i]) for i in group]
        g_end = [gc[i][CHUNK - 1:CHUNK, :] for i in group]

        xs = [[level_operand(q[i], kk[i], gc[i], lf[i], CHUNK >> j).astype(BF16) for j in range(1, B_LEVELS + 1)]
              for i in group]
        q_bf = [q[i].astype(BF16) for i in group]
        k_bf = [kk[i].astype(BF16) for i in group]
        qg = [(q[i] * jnp.exp2(gc[i])).astype(BF16) for i in group]
        kd = [(kk[i] * jnp.exp2(g_end[i] - gc[i])).astype(BF16) for i in group]
        dec = [jnp.exp2(g_end[i]) for i in group]

        prod = [[[_dot(hl(q_bf[i], hh), hl(k_bf[i], hh), _NT)] + [_dot(hl(x, hh), hl(x, hh), _NT) for x in xs[i]]
                 for hh in heads] for i in group]
        upd = [[_dot(hl(v[i], hh), hl(kd[i], hh), _TN) for hh in heads] for i in group]
        state = [s_st[hh] for hh in heads]
        o_inter = []
        for i in group:
            o_inter.append([_dot(hl(qg[i], hh), state[hh].astype(BF16), _NT) for hh in heads])
            state = [state[hh] * hl(dec[i], hh) + upd[i][hh] for hh in heads]
        for hh in heads:
            s_st[hh] = state[hh]
        for i in group:
            outs = []
            for hh in heads:
                acc = mask_ref[0] * prod[i][hh][0]
                for j in range(1, B_LEVELS + 1):
                    acc = acc + mask_ref[j] * prod[i][hh][j]
                o = _dot(acc.astype(BF16), hl(v[i], hh)) + o_inter[i][hh]
                ms = jnp.mean(o * o, axis=-1, keepdims=True)
                outs.append(o * lax.rsqrt(ms + EPS))
            on = jnp.concatenate(outs, axis=1) * hg_ref[...]
            out_ref[rows[i], :] = (on * sg_ref[rows[i], :]).astype(out_ref.dtype)
        return carry

    lax.fori_loop(0, tb // (CHUNK * B_CHUNKS_PER_ITER), chunk_group, 0)


def _hgrn2(zq, zlf, zkk, zv, zsg, head_g, batch, seq, tb, hb):
    t_total = batch * seq
    nt = seq // tb
    assert tb % (CHUNK * B_CHUNKS_PER_ITER) == 0
    mask_np = _hgrn2_pair_masks()
    tril = jnp.asarray(np.tril(np.ones((CHUNK, CHUNK), np.float32)), BF16)
    row_map = lambda b, h, t: (b * nt + t, h)
    head_map = lambda b, h, t: (0, h)
    return pl.pallas_call(
        functools.partial(_hgrn2_kernel, tb=tb, hb=hb),
        grid=(batch, B_HEADS // hb, nt),
        in_specs=[pl.BlockSpec((tb, hb * B_K_DIM), row_map),
                  pl.BlockSpec((tb, hb * B_K_DIM), row_map),
                  pl.BlockSpec((tb, hb * B_K_DIM), row_map),
                  pl.BlockSpec((tb, hb * B_V_DIM), row_map),
                  pl.BlockSpec((tb, hb * B_V_DIM), row_map),
                  pl.BlockSpec((1, hb * B_V_DIM), head_map),
                  pl.BlockSpec((CHUNK, CHUNK), lambda b, h, t: (0, 0)),
                  pl.BlockSpec(mask_np.shape, lambda b, h, t: (0, 0, 0))],
        out_specs=pl.BlockSpec((tb, hb * B_V_DIM), row_map),
        out_shape=jax.ShapeDtypeStruct((t_total, MIX_WIDTH), BF16),
        scratch_shapes=[pltpu.VMEM((hb, B_V_DIM, B_K_DIM), F32)],
        compiler_params=_params(3),
        name="hgrn2",
    )(zq, zlf, zkk, zv, zsg, head_g, tril, jnp.asarray(mask_np, F32))


def _outproj_kernel(y_ref, xq_ref, kv_ref, wo_ref, h_ref, out_ref, hn_ref, rs_ref):
    kv = kv_ref[...]
    parts = []
    for hh in range(XA_HEADS):
        lo = hh * XA_HEAD_DIM
        qh = xq_ref[:, lo:lo + XA_HEAD_DIM]
        kh = kv[:, lo:lo + XA_HEAD_DIM].astype(BF16)
        vh = kv[:, XA_WIDTH + lo:XA_WIDTH + lo + XA_HEAD_DIM].astype(BF16)
        s = _dot(qh, kh, _NT) * (XA_HEAD_DIM ** -0.5)
        e = jnp.exp(s - jnp.max(s, axis=-1, keepdims=True))
        den = jnp.sum(e, axis=-1, keepdims=True)
        parts.append((_dot(e.astype(BF16), vh) / den).astype(BF16))
    y_mem = jnp.concatenate(parts, axis=1)
    h_new = (h_ref[...] + _dot(y_ref[...], wo_ref[0:MIX_WIDTH, :])
             + _dot(y_mem, wo_ref[MIX_WIDTH:D_MODEL, :]))
    out_ref[...] = h_new
    hn_ref[...] = h_new.astype(BF16)
    rs = lax.rsqrt(jnp.mean(h_new * h_new, axis=-1, keepdims=True) + EPS)
    rs_ref[...] = jnp.broadcast_to(rs, rs_ref.shape)


def _outproj(y_mix, zxq, kv_all, layer, w_out_bf, h, seq, mem_len, tm):
    t = h.shape[0]
    per_seq = seq // tm
    return pl.pallas_call(
        _outproj_kernel,
        grid=(t // tm,),
        in_specs=[pl.BlockSpec((tm, MIX_WIDTH), lambda i: (i, 0)),
                  pl.BlockSpec((tm, XA_WIDTH), lambda i: (i, 0)),
                  pl.BlockSpec((mem_len, 2 * XA_WIDTH), lambda i: (i // per_seq, layer)),
                  _resident((D_MODEL, D_MODEL)),
                  pl.BlockSpec((tm, D_MODEL), lambda i: (i, 0))],
        out_specs=[pl.BlockSpec((tm, D_MODEL), lambda i: (i, 0)),
                   pl.BlockSpec((tm, D_MODEL), lambda i: (i, 0)),
                   pl.BlockSpec((tm, LANES), lambda i: (i, 0))],
        out_shape=[jax.ShapeDtypeStruct((t, D_MODEL), F32),
                   jax.ShapeDtypeStruct((t, D_MODEL), BF16),
                   jax.ShapeDtypeStruct((t, LANES), F32)],
        compiler_params=_params(1),
        name="outproj",
    )(y_mix, zxq, kv_all, w_out_bf, h)


def _ffn_kernel(hn_ref, rs_ref, h_ref, wu_ref, wg_ref, cw_ref, cb_ref, wd_ref, out_ref,
                gbuf, gcarry, act_s, *, tm, nf, per_seq):
    i = pl.program_id(0)
    j = pl.program_id(1)
    tf = act_s.shape[2]

    @pl.when(j < nf)
    def _():
        hn = hn_ref[...]
        rs = jnp.concatenate([rs_ref[...]] * (tf // LANES), axis=1)
        g = _dot(hn, wg_ref[...]) * rs
        prev = jnp.where(i % per_seq == 0, 0.0, gcarry[j])
        gbuf[0:SUBLANES] = prev
        gbuf[SUBLANES:SUBLANES + tm] = g
        gcarry[j] = g[tm - SUBLANES:tm, :]
        gc = cb_ref[...]
        for k in range(FFN_CONV):
            lo = SUBLANES - (FFN_CONV - 1) + k
            gc = gc + cw_ref[k:k + 1, :] * gbuf[lo:lo + tm, :]
        sg = _silu(gc) * rs
        u = _dot(hn, wu_ref[...])
        act_s[j] = (sg * u).astype(BF16)

    @pl.when(j >= nf)
    def _():
        act = jnp.concatenate([act_s[f] for f in range(nf)], axis=1)
        out_ref[...] = h_ref[...] + _dot(act, wd_ref[...])


def _ffn(hn, rs, h, w_up_bf, conv_w, conv_b, w_down_all_bf, layer, seq, tm, tf, tn):
    t, d = h.shape
    nf = D_FF // tf
    nd = d // tn
    per_seq = seq // tm
    up = lambda j: jnp.minimum(j, nf - 1)
    down = lambda j: jnp.maximum(j - nf, 0)
    return pl.pallas_call(
        functools.partial(_ffn_kernel, tm=tm, nf=nf, per_seq=per_seq),
        grid=(t // tm, nf + nd),
        in_specs=[pl.BlockSpec((tm, d), lambda i, j: (i, 0)),
                  pl.BlockSpec((tm, LANES), lambda i, j: (i, 0)),
                  pl.BlockSpec((tm, tn), lambda i, j: (i, down(j))),
                  pl.BlockSpec((d, tf), lambda i, j: (0, up(j))),
                  pl.BlockSpec((d, tf), lambda i, j: (0, nf + up(j))),
                  pl.BlockSpec((FFN_CONV, tf), lambda i, j: (0, up(j))),
                  pl.BlockSpec((1, tf), lambda i, j: (0, up(j))),
                  pl.BlockSpec((None, D_FF, tn), lambda i, j: (layer, 0, down(j)))],
        out_specs=pl.BlockSpec((tm, tn), lambda i, j: (i, down(j))),
        out_shape=jax.ShapeDtypeStruct((t, d), F32),
        scratch_shapes=[pltpu.VMEM((tm + SUBLANES, tf), F32),
                        pltpu.VMEM((nf, SUBLANES, tf), F32),
                        pltpu.VMEM((nf, tm, tf), BF16)],
        compiler_params=_params(2),
        name="ffn",
    )(hn, rs, h, w_up_bf, w_up_bf, conv_w, conv_b.reshape(1, D_FF), w_down_all_bf)


def _final_norm_kernel(x_ref, g_ref, out_ref):
    x = x_ref[...]
    out_ref[...] = x * lax.rsqrt(jnp.mean(x * x, axis=-1, keepdims=True) + EPS) * g_ref[...]


def _final_norm(x, g, tm):
    t, d = x.shape
    return pl.pallas_call(
        _final_norm_kernel,
        grid=(t // tm,),
        in_specs=[pl.BlockSpec((tm, d), lambda i: (i, 0)),
                  pl.BlockSpec((1, d), lambda i: (0, 0))],
        out_specs=pl.BlockSpec((tm, d), lambda i: (i, 0)),
        out_shape=jax.ShapeDtypeStruct((t, d), F32),
        compiler_params=_params(1),
        name="final_norm",
    )(x, g.reshape(1, d))


def _pick(n, cap):
    b = min(n, cap)
    while n % b:
        b //= 2
    return b


def kernel(x, mem, norm_mix_g, norm_mem_g, norm_ffn_g, norm_out_g, w_mem_kv, a_w_in, a_gate_b, a_conv_w, a_conv_b, a_head_g, a_w_out, b_w_in, b_lb_logits, b_head_g, b_w_out, ffn_w_up, ffn_conv_w, ffn_conv_b, ffn_w_down):
    batch, seq, d = x.shape
    mem_len = mem.shape[1]
    depth = norm_mix_g.shape[0]
    t = batch * seq
    tm_proj = _pick(seq, TM_PROJ)
    tm_out = _pick(seq, TM_OUT)
    tm_ffn = _pick(seq, TM_FFN)
    tb = _pick(seq, TB_REC)

    h = x.reshape(t, d)
    w_kv_all = jnp.concatenate([_gain_folded(w_mem_kv[layer], norm_mem_g[layer]) for layer in range(depth)], axis=1)
    kv_all = _norm_matmul(mem.reshape(batch * mem_len, d), w_kv_all, _pick(mem_len, TM_PROJ))
    w_down_all = ffn_w_down.astype(BF16)
    for layer in range(depth):
        j = layer // N_MIXERS
        if layer % N_MIXERS == 0:
            zq, zk, zv, zo, zxq, zg = _proj_mlstm(h, _gain_folded(a_w_in[j], norm_mix_g[layer]), a_conv_w[j],
                                                  a_conv_b[j][None, :], seq, tm_proj)
            gb = jnp.pad(a_gate_b[j][None, :], ((0, 0), (0, LANES - 2 * A_HEADS)))
            y_mix = _mlstm(zq, zk, zv, zo, zg, gb, a_head_g[j].reshape(1, MIX_WIDTH), batch, seq, tb,
                           A_HEADS_PER_STEP)
            w_out = a_w_out[j]
        else:
            zq, zlf, zkk, zv, zsg, zxq = _proj_hgrn2(h, _gain_folded(b_w_in[j], norm_mix_g[layer]), b_lb_logits,
                                                     tm_proj, layer)
            y_mix = _hgrn2(zq, zlf, zkk, zv, zsg, b_head_g[j].reshape(1, MIX_WIDTH), batch, seq, tb,
                           B_HEADS_PER_STEP)
            w_out = b_w_out[j]
        h, hn, rs = _outproj(y_mix, zxq, kv_all, layer, w_out.astype(BF16), h, seq, mem_len, tm_out)
        h = _ffn(hn, rs, h, _gain_folded(ffn_w_up[layer], norm_ffn_g[layer]), ffn_conv_w[layer], ffn_conv_b[layer],
                 w_down_all, layer, seq, tm_ffn, FFN_TF, FFN_TN)
    return _final_norm(h, norm_out_g, tm_out).reshape(batch, seq, d)
```

```python
import functools

import numpy as np
import jax
import jax.numpy as jnp
from jax import lax
from jax.experimental import pallas as pl
from jax.experimental.pallas import tpu as pltpu

F32 = jnp.float32
BF16 = jnp.bfloat16

D_MODEL = 2048
N_MIXERS = 2
CHUNK = 64
EPS = 1e-6
LOG2_E = 1.4426950408889634
F32_TINY = 1e-37

XA_HEADS = 4
XA_WIDTH = D_MODEL // 4
XA_HEAD_DIM = XA_WIDTH // XA_HEADS
MIX_WIDTH = D_MODEL - XA_WIDTH

A_HEADS = 4
A_V_DIM = MIX_WIDTH // A_HEADS
A_QK_DIM = A_V_DIM // 2
A_QK_PAD = 256
A_CONV = 4
A_AUG = 128
A_HEADS_PER_STEP = 4
A_CHUNK = 128
A_CHUNKS_PER_ITER = 2
B_HEADS = 12
B_K_DIM = 128
B_V_DIM = 128
B_LEVELS = 6
B_HEADS_PER_STEP = 12
B_CHUNKS_PER_ITER = 2
B_MERGED_HALVES = (CHUNK // 2, CHUNK // 4)

D_FF = 5632
FFN_CONV = 3

LANES = 128
SUBLANES = 8
VMEM_LIMIT_BYTES = 56 * 1024 * 1024

TM_PROJ = 256
TM_OUT = 512
TM_FFN = 1024
TB_REC = 512
FFN_TF = 512
FFN_TN = 512

_NN = (((1,), (0,)), ((), ()))
_NT = (((1,), (1,)), ((), ()))
_TN = (((0,), (0,)), ((), ()))


def _dot(a, b, dims=_NN):
    return lax.dot_general(a, b, dims, preferred_element_type=F32)


def _split2(x):
    hi = x.astype(BF16)
    return hi, (x - hi.astype(F32)).astype(BF16)


def _dot_exact_lhs(a_bf, x, dims=_NN):
    return sum(_dot(a_bf, p, dims) for p in _split2(x))


def _dot_exact_rhs(x, b_bf, dims=_NN):
    return sum(_dot(p, b_bf, dims) for p in _split2(x))


def _sigmoid(x):
    return 0.5 + 0.5 * jnp.tanh(0.5 * x)


def _silu(x):
    hx = 0.5 * x
    return hx + hx * jnp.tanh(hx)


def _log_sigmoid(x):
    return jnp.minimum(x, 0.0) - jnp.log(1.0 + jnp.exp(-jnp.abs(x)))


def _params(n_grid):
    return pltpu.CompilerParams(dimension_semantics=("arbitrary",) * n_grid,
                                vmem_limit_bytes=VMEM_LIMIT_BYTES)


def _resident(shape):
    nd = len(shape)
    return pl.BlockSpec(shape, lambda *_: (0,) * nd, pipeline_mode=pl.Buffered(1))


def _gain_folded(w, g):
    return (g[:, None] * w).astype(BF16)


def _rms_split(x_ref):
    x = x_ref[...]
    return x.astype(BF16), lax.rsqrt(jnp.mean(x * x, axis=-1, keepdims=True) + EPS)


def _norm_matmul_kernel(x_ref, w_ref, out_ref):
    x_bf, rs = _rms_split(x_ref)
    out_ref[...] = _dot(x_bf, w_ref[...]) * rs


def _norm_matmul(x, wg_bf, tm):
    t, d = x.shape
    n = wg_bf.shape[1]
    return pl.pallas_call(
        _norm_matmul_kernel,
        grid=(t // tm,),
        in_specs=[pl.BlockSpec((tm, d), lambda i: (i, 0)),
                  _resident((d, n))],
        out_specs=pl.BlockSpec((tm, n), lambda i: (i, 0)),
        out_shape=jax.ShapeDtypeStruct((t, n), F32),
        compiler_params=_params(1),
        name="norm_matmul",
    )(x, wg_bf)


def _proj_mlstm_kernel(x_ref, w_ref, wx_ref, wg_ref, cw_ref, cb_ref, q_out, k_out, v_out, o_out, xq_out, gt_out,
                       cbuf, *, tm, per_seq):
    i = pl.program_id(0)
    qk_w = 2 * A_HEADS * A_QK_DIM

    @pl.when(i % per_seq == 0)
    def _():
        cbuf[0:SUBLANES] = jnp.zeros((SUBLANES, qk_w), F32)

    x_bf, rs = _rms_split(x_ref)

    def proj(off, width):
        return _dot(x_bf, w_ref[:, off:off + width]) * rs

    cbuf[SUBLANES:SUBLANES + tm] = proj(0, qk_w)
    acc = cb_ref[...]
    for j in range(A_CONV):
        lo = SUBLANES - (A_CONV - 1) + j
        acc = acc + cw_ref[j:j + 1, :] * cbuf[lo:lo + tm, :]
    y = _silu(acc)
    cbuf[0:SUBLANES] = cbuf[tm:tm + SUBLANES]
    pad = jnp.zeros((tm, A_QK_PAD - A_QK_DIM), F32)
    for hh in range(A_HEADS):
        qh = y[:, hh * A_QK_DIM:(hh + 1) * A_QK_DIM]
        kh = y[:, (A_HEADS + hh) * A_QK_DIM:(A_HEADS + hh + 1) * A_QK_DIM] * (A_QK_DIM ** -0.5)
        q_out[:, hh * A_QK_PAD:(hh + 1) * A_QK_PAD] = jnp.concatenate([qh, pad], axis=1).astype(BF16)
        k_out[:, hh * A_QK_PAD:(hh + 1) * A_QK_PAD] = jnp.concatenate([kh, pad], axis=1).astype(BF16)
    off = qk_w
    v_out[...] = proj(off, MIX_WIDTH).astype(BF16)
    off += MIX_WIDTH
    o_out[...] = _sigmoid(proj(off, MIX_WIDTH))
    xq_out[...] = (_dot(x_bf, wx_ref[...]) * rs).astype(BF16)
    gt_out[...] = _dot(x_bf, wg_ref[...]) * rs


def _proj_mlstm(x, w_bf, conv_w, conv_b, seq, tm):
    t, d = x.shape
    qk_w = 2 * A_HEADS * A_QK_DIM
    g0 = qk_w + 2 * MIX_WIDTH
    n_gates = 2 * A_HEADS
    assert w_bf.shape[1] == g0 + n_gates + XA_WIDTH
    w_g = jnp.pad(w_bf[:, g0:g0 + n_gates], ((0, 0), (0, LANES - n_gates)))
    w_x = w_bf[:, g0 + n_gates:]
    widths = (A_HEADS * A_QK_PAD, A_HEADS * A_QK_PAD, MIX_WIDTH, MIX_WIDTH, XA_WIDTH, LANES)
    dtypes = (BF16, BF16, BF16, F32, BF16, F32)
    return pl.pallas_call(
        functools.partial(_proj_mlstm_kernel, tm=tm, per_seq=seq // tm),
        grid=(t // tm,),
        in_specs=[pl.BlockSpec((tm, d), lambda i: (i, 0)),
                  _resident((d, g0)),
                  _resident((d, XA_WIDTH)),
                  _resident((d, LANES)),
                  _resident((A_CONV, qk_w)),
                  _resident((1, qk_w))],
        out_specs=[pl.BlockSpec((tm, s), lambda i: (i, 0)) for s in widths],
        out_shape=[jax.ShapeDtypeStruct((t, s), dt) for s, dt in zip(widths, dtypes)],
        scratch_shapes=[pltpu.VMEM((tm + SUBLANES, qk_w), F32)],
        compiler_params=_params(1),
        name="proj_mlstm",
    )(x, w_bf, w_x, w_g, conv_w, conv_b)


def _proj_hgrn2_kernel(x_ref, w_ref, lbl_ref, q_out, lf_out, kk_out, v_out, sg_out, xq_out, *, layer):
    lg = lbl_ref[...]
    lg = lg - jnp.max(lg, axis=0, keepdims=True)
    pe = jnp.exp(lg)
    pr = pe / jnp.sum(pe, axis=0, keepdims=True)
    c0 = pr[0:1, :]
    cl = c0
    for r in range(1, layer + 1):
        cl = cl + pr[r:r + 1, :]
    lb = cl - c0
    om = 1.0 - lb

    x_bf, rs = _rms_split(x_ref)
    kw = B_HEADS * B_K_DIM

    def proj(off, width):
        return _dot(x_bf, w_ref[:, off:off + width]) * rs

    q_out[...] = _silu(proj(0, kw))
    fz = proj(kw, kw)
    sig = 1.0 / (1.0 + jnp.exp(-fz))
    gate = lb + om * sig
    lf_out[...] = jnp.where(gate > F32_TINY, jnp.log2(gate), jnp.log2(om) + fz * LOG2_E)
    kk_out[...] = om * (1.0 - sig)
    off = 2 * kw
    v_out[...] = proj(off, MIX_WIDTH).astype(BF16)
    off += MIX_WIDTH
    sg_out[...] = _silu(proj(off, MIX_WIDTH))
    off += MIX_WIDTH
    xq_out[...] = proj(off, XA_WIDTH).astype(BF16)


def _proj_hgrn2(x, w_bf, lb_logits, tm, layer):
    t, d = x.shape
    n = w_bf.shape[1]
    kw = B_HEADS * B_K_DIM
    widths = (kw, kw, kw, MIX_WIDTH, MIX_WIDTH, XA_WIDTH)
    dtypes = (F32, F32, F32, BF16, F32, BF16)
    assert 2 * kw + 2 * MIX_WIDTH + XA_WIDTH == n
    return pl.pallas_call(
        functools.partial(_proj_hgrn2_kernel, layer=layer),
        grid=(t // tm,),
        in_specs=[pl.BlockSpec((tm, d), lambda i: (i, 0)),
                  _resident((d, n)),
                  _resident(lb_logits.shape)],
        out_specs=[pl.BlockSpec((tm, s), lambda i: (i, 0)) for s in widths],
        out_shape=[jax.ShapeDtypeStruct((t, s), dt) for s, dt in zip(widths, dtypes)],
        compiler_params=_params(1),
        name="proj_hgrn2",
    )(x, w_bf, lb_logits)


def _mlstm_kernel(q_ref, k_ref, v_ref, o_ref, gt_ref, gb_ref, hg_ref, tril_ref, out_ref, c_st, m_st, *, tb, hb):
    hgrp = pl.program_id(1)
    t = pl.program_id(2)
    aug_w = A_V_DIM + A_AUG
    rep = aug_w // LANES

    @pl.when(t == 0)
    def _():
        c_st[...] = jnp.zeros_like(c_st)
        m_st[...] = jnp.zeros_like(m_st)

    tril = tril_ref[...]
    row = lax.broadcasted_iota(jnp.int32, (A_CHUNK, A_CHUNK), 0)
    col = lax.broadcasted_iota(jnp.int32, (A_CHUNK, A_CHUNK), 1)
    causal = row >= col
    wide = hb * LANES
    srow = lax.broadcasted_iota(jnp.int32, (LANES, 2 * wide), 0)
    scol = lax.broadcasted_iota(jnp.int32, (LANES, 2 * wide), 1)
    blk = scol // LANES
    want = jnp.where(blk < hb, hgrp * hb + blk, A_HEADS + hgrp * hb + (blk - hb))
    sel = jnp.where(srow == want, 1.0, 0.0).astype(BF16)
    frow = lax.broadcasted_iota(jnp.int32, (A_CHUNK, LANES), 1)
    first = jnp.where(frow == 0, 1.0, 0.0).astype(BF16)
    ones_aug = jnp.ones((A_CHUNK, A_AUG), BF16)
    heads = range(hb)

    def lanes_of(hh):
        return slice(hh * LANES, (hh + 1) * LANES)

    qk_lanes = [slice(hh * A_QK_PAD, hh * A_QK_PAD + A_QK_DIM) for hh in heads]
    v_lanes = [slice(hh * A_V_DIM, (hh + 1) * A_V_DIM) for hh in heads]

    def chunk_group(c, carry):
        group = range(A_CHUNKS_PER_ITER)
        rows = [pl.ds(pl.multiple_of((c * A_CHUNKS_PER_ITER + i) * A_CHUNK, A_CHUNK), A_CHUNK) for i in group]
        gates = [gt_ref[r, :] + gb_ref[...] for r in rows]
        g2 = [_dot_exact_rhs(gates[i], sel) for i in group]
        ic = [g2[i][:, :wide] for i in group]
        lf = [_log_sigmoid(g2[i][:, wide:]) for i in group]
        gc = [_dot_exact_lhs(tril, lf[i]) for i in group]
        r_mat = [[_dot_exact_lhs(first, (ic[i] - gc[i])[:, lanes_of(hh)], _NT) for hh in heads]
                 for i in group]
        dmat = [[jnp.where(causal, gc[i][:, hh * LANES:hh * LANES + A_CHUNK] + r_mat[i][hh], -jnp.inf)
                 for hh in heads] for i in group]
        mx = [jnp.concatenate([jnp.broadcast_to(jnp.max(dmat[i][hh], axis=1, keepdims=True), (A_CHUNK, LANES))
                               for hh in heads], axis=1) for i in group]

        m_prev = m_st[0:1, :]
        w_inter, e_neg, p, decay, ws = [], [], [], [], []
        for i in group:
            a = gc[i] + m_prev
            m_row = jnp.maximum(a, mx[i])
            w_inter.append(jnp.exp(a - m_row))
            e_neg.append(jnp.exp(-m_row))
            p.append([jnp.exp(dmat[i][hh] - m_row[:, hh * LANES:hh * LANES + A_CHUNK]) for hh in heads])
            g_end = gc[i][A_CHUNK - 1:A_CHUNK, :]
            a_end = g_end + m_prev
            w_end = g_end - gc[i] + ic[i]
            m_prev = jnp.maximum(a_end, jnp.max(w_end, axis=0, keepdims=True))
            decay.append(jnp.exp(a_end - m_prev))
            ws.append(jnp.exp(w_end - m_prev).astype(BF16))
        m_st[...] = jnp.broadcast_to(m_prev, (SUBLANES, wide))

        q = [[q_ref[r, qk_lanes[hh]] for hh in heads] for r in rows]
        k = [[k_ref[r, qk_lanes[hh]] for hh in heads] for r in rows]
        v_aug = [[jnp.concatenate([v_ref[r, v_lanes[hh]], ones_aug], axis=1) for hh in heads] for r in rows]
        qk = [[_dot(q[i][hh], k[i][hh], _NT) for hh in heads] for i in group]
        intra = [[_dot((qk[i][hh] * p[i][hh]).astype(BF16), v_aug[i][hh]) for hh in heads] for i in group]
        upd = [[_dot(k[i][hh] * jnp.concatenate([ws[i][:, lanes_of(hh)]] * (A_QK_PAD // LANES), axis=1)[:, :A_QK_DIM],
                     v_aug[i][hh], _TN) for hh in heads] for i in group]
        state = [c_st[hh] for hh in heads]
        inter = []
        for i in group:
            inter.append([_dot(q[i][hh], state[hh].astype(BF16)) for hh in heads])
            state = [jnp.concatenate([decay[i][:, lanes_of(hh)]] * rep, axis=1) * state[hh] + upd[i][hh]
                     for hh in heads]
        for hh in heads:
            c_st[hh] = state[hh]
        for i in group:
            for hh in heads:
                tot = jnp.concatenate([w_inter[i][:, lanes_of(hh)]] * rep, axis=1) * inter[i][hh] + intra[i][hh]
                den = tot[:, A_V_DIM:]
                inv = 1.0 / jnp.maximum(jnp.abs(den), e_neg[i][:, lanes_of(hh)])
                h_out = tot[:, :A_V_DIM] * jnp.concatenate([inv] * (A_V_DIM // LANES), axis=1)
                ms = jnp.mean(h_out * h_out, axis=-1, keepdims=True)
                hn = h_out * lax.rsqrt(ms + EPS) * hg_ref[:, v_lanes[hh]]
                out_ref[rows[i], v_lanes[hh]] = (o_ref[rows[i], v_lanes[hh]] * hn).astype(out_ref.dtype)
        return carry

    lax.fori_loop(0, tb // (A_CHUNK * A_CHUNKS_PER_ITER), chunk_group, 0)


def _mlstm(zq, zk, zv, zo, zg, gate_b, head_g, batch, seq, tb, hb):
    t_total = batch * seq
    nt = seq // tb
    assert tb % (A_CHUNK * A_CHUNKS_PER_ITER) == 0
    tril = jnp.asarray(np.tril(np.ones((A_CHUNK, A_CHUNK), np.float32)), BF16)
    row_map = lambda b, h, t: (b * nt + t, h)
    head_map = lambda b, h, t: (0, h)
    fixed = lambda b, h, t: (0, 0)
    return pl.pallas_call(
        functools.partial(_mlstm_kernel, tb=tb, hb=hb),
        grid=(batch, A_HEADS // hb, nt),
        in_specs=[pl.BlockSpec((tb, hb * A_QK_PAD), row_map),
                  pl.BlockSpec((tb, hb * A_QK_PAD), row_map),
                  pl.BlockSpec((tb, hb * A_V_DIM), row_map),
                  pl.BlockSpec((tb, hb * A_V_DIM), row_map),
                  pl.BlockSpec((tb, LANES), lambda b, h, t: (b * nt + t, 0)),
                  pl.BlockSpec((1, LANES), fixed),
                  pl.BlockSpec((1, hb * A_V_DIM), head_map),
                  pl.BlockSpec((A_CHUNK, A_CHUNK), fixed)],
        out_specs=pl.BlockSpec((tb, hb * A_V_DIM), row_map),
        out_shape=jax.ShapeDtypeStruct((t_total, MIX_WIDTH), BF16),
        scratch_shapes=[pltpu.VMEM((hb, A_QK_DIM, A_V_DIM + A_AUG), F32),
                        pltpu.VMEM((SUBLANES, hb * LANES), F32)],
        compiler_params=_params(3),
        name="mlstm",
    )(zq, zk, zv, zo, zg, gate_b, head_g, tril)


def _hgrn2_pair_masks():
    n = CHUNK
    masks = [np.eye(n, dtype=np.float32)]
    for j in range(1, B_LEVELS + 1):
        c = n >> j
        pm = np.zeros((n, n), np.float32)
        for r in range(n):
            if r % (2 * c) >= c:
                mid = (r // (2 * c)) * 2 * c + c
                pm[r, mid - c:mid] = 1.0
        masks.append(pm)
    return np.stack(masks, axis=0)


def _hgrn2_kernel(q_ref, lf_ref, kk_ref, v_ref, sg_ref, hg_ref, tril_ref, mask_ref, out_ref, s_st, *, tb, hb):
    t = pl.program_id(2)

    @pl.when(t == 0)
    def _():
        s_st[...] = jnp.zeros_like(s_st)

    tril = tril_ref[...]
    wide = hb * B_K_DIM
    heads = range(hb)
    rowi = lax.broadcasted_iota(jnp.int32, (CHUNK, wide), 0)
    sub = lax.broadcasted_iota(jnp.int32, (CHUNK // SUBLANES, SUBLANES, wide), 1)

    def level_operand(q, kk, gc, lf, half):
        if half >= SUBLANES:
            pieces = []
            for b0 in range(0, CHUNK, 2 * half):
                mid = b0 + half
                ref = gc[mid - 1:mid, :]
                pieces.append(kk[b0:mid, :] * jnp.exp2(ref - gc[b0:mid, :]))
                pieces.append(q[mid:mid + half, :] * jnp.exp2(gc[mid:mid + half, :] - ref))
            return jnp.concatenate(pieces, axis=0)
        if half == 1:
            odd = (rowi & 1) != 0
            return jnp.where(odd, q * jnp.exp2(lf), kk)
        g3 = gc.reshape(CHUNK // SUBLANES, SUBLANES, wide)
        ref = g3[:, half - 1:half, :]
        for b0 in range(2 * half, SUBLANES, 2 * half):
            ref = jnp.where(sub >= b0, g3[:, b0 + half - 1:b0 + half, :], ref)
        d = g3 - ref
        upper = (sub & half) != 0
        ex = jnp.exp2(jnp.where(upper, d, -d)).reshape(CHUNK, wide)
        return jnp.where((rowi & half) != 0, q, kk) * ex

    def hl(x, hh):
        return x[:, hh * B_K_DIM:(hh + 1) * B_K_DIM]

    def coarse_operands(q, kk, gc):
        q_ops, k_ops = [], []
        for half in B_MERGED_HALVES:
            pad = jnp.zeros((half, wide), F32)
            for b0 in range(0, CHUNK, 2 * half):
                mid = b0 + half
                ref = gc[mid - 1:mid, :]
                k_rows = kk[b0:mid, :] * jnp.exp2(ref - gc[b0:mid, :])
                q_rows = q[mid:mid + half, :] * jnp.exp2(gc[mid:mid + half, :] - ref)
                for ops, block in ((k_ops, [k_rows, pad]), (q_ops, [pad, q_rows])):
                    parts = [jnp.concatenate(block, axis=0).astype(BF16)]
                    if b0:
                        parts.insert(0, jnp.zeros((b0, wide), BF16))
                    if mid + half < CHUNK:
                        parts.append(jnp.zeros((CHUNK - mid - half, wide), BF16))
                    ops.append(jnp.concatenate(parts, axis=0) if len(parts) > 1 else parts[0])
        return q_ops, k_ops

    def chunk_group(c, carry):
        group = range(B_CHUNKS_PER_ITER)
        rows = [pl.ds(pl.multiple_of((c * B_CHUNKS_PER_ITER + i) * CHUNK, CHUNK), CHUNK) for i in group]
        q = [q_ref[r, :] for r in rows]
        lf = [lf_ref[r, :] for r in rows]
        kk = [kk_ref[r, :] for r in rows]
        v = [v_ref[r, :] for r in rows]
        gc = [_dot_exact_lhs(tril, lf[i]) for i in group]
        g_end = [gc[i][CHUNK - 1:CHUNK, :] for i in group]

        masked_levels = range(len(B_MERGED_HALVES) + 1, B_LEVELS + 1)
        xs = [[level_operand(q[i], kk[i], gc[i], lf[i], CHUNK >> j).astype(BF16) for j in masked_levels]
              for i in group]
        coarse = [coarse_operands(q[i], kk[i], gc[i]) for i in group]
        q_bf = [q[i].astype(BF16) for i in group]
        k_bf = [kk[i].astype(BF16) for i in group]
        qg = [(q[i] * jnp.exp2(gc[i])).astype(BF16) for i in group]
        kd = [(kk[i] * jnp.exp2(g_end[i] - gc[i])).astype(BF16) for i in group]
        dec = [jnp.exp2(g_end[i]) for i in group]

        prod = [[[_dot(hl(q_bf[i], hh), hl(k_bf[i], hh), _NT)] + [_dot(hl(x, hh), hl(x, hh), _NT) for x in xs[i]]
                 for hh in heads] for i in group]
        merged = [[_dot(jnp.concatenate([hl(a, hh) for a in coarse[i][0]], axis=1),
                        jnp.concatenate([hl(a, hh) for a in coarse[i][1]], axis=1), _NT)
                   for hh in heads] for i in group]
        upd = [[_dot(hl(v[i], hh), hl(kd[i], hh), _TN) for hh in heads] for i in group]
        state = [s_st[hh] for hh in heads]
        o_inter = []
        for i in group:
            o_inter.append([_dot(hl(qg[i], hh), state[hh].astype(BF16), _NT) for hh in heads])
            state = [state[hh] * hl(dec[i], hh) + upd[i][hh] for hh in heads]
        for hh in heads:
            s_st[hh] = state[hh]
        for i in group:
            outs = []
            for hh in heads:
                acc = mask_ref[0] * prod[i][hh][0] + merged[i][hh]
                for n, j in enumerate(masked_levels):
                    acc = acc + mask_ref[j] * prod[i][hh][1 + n]
                o = _dot(acc.astype(BF16), hl(v[i], hh)) + o_inter[i][hh]
                ms = jnp.mean(o * o, axis=-1, keepdims=True)
                outs.append(o * lax.rsqrt(ms + EPS))
            on = jnp.concatenate(outs, axis=1) * hg_ref[...]
            out_ref[rows[i], :] = (on * sg_ref[rows[i], :]).astype(out_ref.dtype)
        return carry

    lax.fori_loop(0, tb // (CHUNK * B_CHUNKS_PER_ITER), chunk_group, 0)


def _hgrn2(zq, zlf, zkk, zv, zsg, head_g, batch, seq, tb, hb):
    t_total = batch * seq
    nt = seq // tb
    assert tb % (CHUNK * B_CHUNKS_PER_ITER) == 0
    mask_np = _hgrn2_pair_masks()
    tril = jnp.asarray(np.tril(np.ones((CHUNK, CHUNK), np.float32)), BF16)
    row_map = lambda b, h, t: (b * nt + t, h)
    head_map = lambda b, h, t: (0, h)
    return pl.pallas_call(
        functools.partial(_hgrn2_kernel, tb=tb, hb=hb),
        grid=(batch, B_HEADS // hb, nt),
        in_specs=[pl.BlockSpec((tb, hb * B_K_DIM), row_map),
                  pl.BlockSpec((tb, hb * B_K_DIM), row_map),
                  pl.BlockSpec((tb, hb * B_K_DIM), row_map),
                  pl.BlockSpec((tb, hb * B_V_DIM), row_map),
                  pl.BlockSpec((tb, hb * B_V_DIM), row_map),
                  pl.BlockSpec((1, hb * B_V_DIM), head_map),
                  pl.BlockSpec((CHUNK, CHUNK), lambda b, h, t: (0, 0)),
                  pl.BlockSpec(mask_np.shape, lambda b, h, t: (0, 0, 0))],
        out_specs=pl.BlockSpec((tb, hb * B_V_DIM), row_map),
        out_shape=jax.ShapeDtypeStruct((t_total, MIX_WIDTH), BF16),
        scratch_shapes=[pltpu.VMEM((hb, B_V_DIM, B_K_DIM), F32)],
        compiler_params=_params(3),
        name="hgrn2",
    )(zq, zlf, zkk, zv, zsg, head_g, tril, jnp.asarray(mask_np, F32))


def _outproj_kernel(y_ref, xq_ref, kv_ref, wo_ref, h_ref, out_ref, hn_ref, rs_ref):
    kv = kv_ref[...]
    parts = []
    for hh in range(XA_HEADS):
        lo = hh * XA_HEAD_DIM
        qh = xq_ref[:, lo:lo + XA_HEAD_DIM]
        kh = kv[:, lo:lo + XA_HEAD_DIM].astype(BF16)
        vh = kv[:, XA_WIDTH + lo:XA_WIDTH + lo + XA_HEAD_DIM].astype(BF16)
        s = _dot(qh, kh, _NT) * (XA_HEAD_DIM ** -0.5)
        e = jnp.exp(s - jnp.max(s, axis=-1, keepdims=True))
        den = jnp.sum(e, axis=-1, keepdims=True)
        parts.append((_dot(e.astype(BF16), vh) / den).astype(BF16))
    y_mem = jnp.concatenate(parts, axis=1)
    h_new = (h_ref[...] + _dot(y_ref[...], wo_ref[0:MIX_WIDTH, :])
             + _dot(y_mem, wo_ref[MIX_WIDTH:D_MODEL, :]))
    out_ref[...] = h_new
    hn_ref[...] = h_new.astype(BF16)
    rs = lax.rsqrt(jnp.mean(h_new * h_new, axis=-1, keepdims=True) + EPS)
    rs_ref[...] = jnp.broadcast_to(rs, rs_ref.shape)


def _outproj(y_mix, zxq, kv_all, layer, w_out_bf, h, seq, mem_len, tm):
    t = h.shape[0]
    per_seq = seq // tm
    return pl.pallas_call(
        _outproj_kernel,
        grid=(t // tm,),
        in_specs=[pl.BlockSpec((tm, MIX_WIDTH), lambda i: (i, 0)),
                  pl.BlockSpec((tm, XA_WIDTH), lambda i: (i, 0)),
                  pl.BlockSpec((mem_len, 2 * XA_WIDTH), lambda i: (i // per_seq, layer)),
                  _resident((D_MODEL, D_MODEL)),
                  pl.BlockSpec((tm, D_MODEL), lambda i: (i, 0))],
        out_specs=[pl.BlockSpec((tm, D_MODEL), lambda i: (i, 0)),
                   pl.BlockSpec((tm, D_MODEL), lambda i: (i, 0)),
                   pl.BlockSpec((tm, LANES), lambda i: (i, 0))],
        out_shape=[jax.ShapeDtypeStruct((t, D_MODEL), F32),
                   jax.ShapeDtypeStruct((t, D_MODEL), BF16),
                   jax.ShapeDtypeStruct((t, LANES), F32)],
        compiler_params=_params(1),
        name="outproj",
    )(y_mix, zxq, kv_all, w_out_bf, h)


def _ffn_kernel(hn_ref, rs_ref, h_ref, wu_ref, wg_ref, cw_ref, cb_ref, wd_ref, out_ref,
                gbuf, gcarry, act_s, *, tm, nf, per_seq):
    i = pl.program_id(0)
    j = pl.program_id(1)
    tf = act_s.shape[2]

    @pl.when(j < nf)
    def _():
        hn = hn_ref[...]
        rs = jnp.concatenate([rs_ref[...]] * (tf // LANES), axis=1)
        g = _dot(hn, wg_ref[...]) * rs
        prev = jnp.where(i % per_seq == 0, 0.0, gcarry[j])
        gbuf[0:SUBLANES] = prev
        gbuf[SUBLANES:SUBLANES + tm] = g
        gcarry[j] = g[tm - SUBLANES:tm, :]
        gc = cb_ref[...]
        for k in range(FFN_CONV):
            lo = SUBLANES - (FFN_CONV - 1) + k
            gc = gc + cw_ref[k:k + 1, :] * gbuf[lo:lo + tm, :]
        sg = _silu(gc) * rs
        u = _dot(hn, wu_ref[...])
        act_s[j] = (sg * u).astype(BF16)

    @pl.when(j >= nf)
    def _():
        act = jnp.concatenate([act_s[f] for f in range(nf)], axis=1)
        out_ref[...] = h_ref[...] + _dot(act, wd_ref[...])


def _ffn(hn, rs, h, w_up_bf, conv_w, conv_b, w_down_all_bf, layer, seq, tm, tf, tn):
    t, d = h.shape
    nf = D_FF // tf
    nd = d // tn
    per_seq = seq // tm
    up = lambda j: jnp.minimum(j, nf - 1)
    down = lambda j: jnp.maximum(j - nf, 0)
    return pl.pallas_call(
        functools.partial(_ffn_kernel, tm=tm, nf=nf, per_seq=per_seq),
        grid=(t // tm, nf + nd),
        in_specs=[pl.BlockSpec((tm, d), lambda i, j: (i, 0)),
                  pl.BlockSpec((tm, LANES), lambda i, j: (i, 0)),
                  pl.BlockSpec((tm, tn), lambda i, j: (i, down(j))),
                  pl.BlockSpec((d, tf), lambda i, j: (0, up(j))),
                  pl.BlockSpec((d, tf), lambda i, j: (0, nf + up(j))),
                  pl.BlockSpec((FFN_CONV, tf), lambda i, j: (0, up(j))),
                  pl.BlockSpec((1, tf), lambda i, j: (0, up(j))),
                  pl.BlockSpec((None, D_FF, tn), lambda i, j: (layer, 0, down(j)))],
        out_specs=pl.BlockSpec((tm, tn), lambda i, j: (i, down(j))),
        out_shape=jax.ShapeDtypeStruct((t, d), F32),
        scratch_shapes=[pltpu.VMEM((tm + SUBLANES, tf), F32),
                        pltpu.VMEM((nf, SUBLANES, tf), F32),
                        pltpu.VMEM((nf, tm, tf), BF16)],
        compiler_params=_params(2),
        name="ffn",
    )(hn, rs, h, w_up_bf, w_up_bf, conv_w, conv_b.reshape(1, D_FF), w_down_all_bf)


def _final_norm_kernel(x_ref, g_ref, out_ref):
    x = x_ref[...]
    out_ref[...] = x * lax.rsqrt(jnp.mean(x * x, axis=-1, keepdims=True) + EPS) * g_ref[...]


def _final_norm(x, g, tm):
    t, d = x.shape
    return pl.pallas_call(
        _final_norm_kernel,
        grid=(t // tm,),
        in_specs=[pl.BlockSpec((tm, d), lambda i: (i, 0)),
                  pl.BlockSpec((1, d), lambda i: (0, 0))],
        out_specs=pl.BlockSpec((tm, d), lambda i: (i, 0)),
        out_shape=jax.ShapeDtypeStruct((t, d), F32),
        compiler_params=_params(1),
        name="final_norm",
    )(x, g.reshape(1, d))


def _pick(n, cap):
    b = min(n, cap)
    while n % b:
        b //= 2
    return b


def kernel(x, mem, norm_mix_g, norm_mem_g, norm_ffn_g, norm_out_g, w_mem_kv, a_w_in, a_gate_b, a_conv_w, a_conv_b, a_head_g, a_w_out, b_w_in, b_lb_logits, b_head_g, b_w_out, ffn_w_up, ffn_conv_w, ffn_conv_b, ffn_w_down):
    batch, seq, d = x.shape
    mem_len = mem.shape[1]
    depth = norm_mix_g.shape[0]
    t = batch * seq
    tm_proj = _pick(seq, TM_PROJ)
    tm_out = _pick(seq, TM_OUT)
    tm_ffn = _pick(seq, TM_FFN)
    tb = _pick(seq, TB_REC)

    h = x.reshape(t, d)
    w_kv_all = jnp.concatenate([_gain_folded(w_mem_kv[layer], norm_mem_g[layer]) for layer in range(depth)], axis=1)
    kv_all = _norm_matmul(mem.reshape(batch * mem_len, d), w_kv_all, _pick(mem_len, TM_PROJ))
    w_down_all = ffn_w_down.astype(BF16)
    for layer in range(depth):
        j = layer // N_MIXERS
        if layer % N_MIXERS == 0:
            zq, zk, zv, zo, zxq, zg = _proj_mlstm(h, _gain_folded(a_w_in[j], norm_mix_g[layer]), a_conv_w[j],
                                                  a_conv_b[j][None, :], seq, tm_proj)
            gb = jnp.pad(a_gate_b[j][None, :], ((0, 0), (0, LANES - 2 * A_HEADS)))
            y_mix = _mlstm(zq, zk, zv, zo, zg, gb, a_head_g[j].reshape(1, MIX_WIDTH), batch, seq, tb,
                           A_HEADS_PER_STEP)
            w_out = a_w_out[j]
        else:
            zq, zlf, zkk, zv, zsg, zxq = _proj_hgrn2(h, _gain_folded(b_w_in[j], norm_mix_g[layer]), b_lb_logits,
                                                     tm_proj, layer)
            y_mix = _hgrn2(zq, zlf, zkk, zv, zsg, b_head_g[j].reshape(1, MIX_WIDTH), batch, seq, tb,
                           B_HEADS_PER_STEP)
            w_out = b_w_out[j]
        h, hn, rs = _outproj(y_mix, zxq, kv_all, layer, w_out.astype(BF16), h, seq, mem_len, tm_out)
        h = _ffn(hn, rs, h, _gain_folded(ffn_w_up[layer], norm_ffn_g[layer]), ffn_conv_w[layer], ffn_conv_b[layer],
                 w_down_all, layer, seq, tm_ffn, FFN_TF, FFN_TN)
    return _final_norm(h, norm_out_g, tm_out).reshape(batch, seq, d)
```

```python
import functools

import numpy as np
import jax
import jax.numpy as jnp
from jax import lax
from jax.experimental import pallas as pl
from jax.experimental.pallas import tpu as pltpu

F32 = jnp.float32
BF16 = jnp.bfloat16

D_MODEL = 2048
N_MIXERS = 2
CHUNK = 64
EPS = 1e-6
LOG2_E = 1.4426950408889634
F32_TINY = 1e-37

XA_HEADS = 4
XA_WIDTH = D_MODEL // 4
XA_HEAD_DIM = XA_WIDTH // XA_HEADS
MIX_WIDTH = D_MODEL - XA_WIDTH

A_HEADS = 4
A_V_DIM = MIX_WIDTH // A_HEADS
A_QK_DIM = A_V_DIM // 2
A_QK_PAD = 256
A_CONV = 4
A_AUG = 128
A_HEADS_PER_STEP = 4
A_CHUNK = 128
A_CHUNKS_PER_ITER = 2
B_HEADS = 12
B_K_DIM = 128
B_V_DIM = 128
B_LEVELS = 6
B_HEADS_PER_STEP = 12
B_CHUNKS_PER_ITER = 2
B_MERGED_HALVES = (CHUNK // 2, CHUNK // 4)

D_FF = 5632
FFN_CONV = 3

LANES = 128
SUBLANES = 8
VMEM_LIMIT_BYTES = 56 * 1024 * 1024

TM_PROJ = 256
TM_OUT = 512
TM_FFN = 1024
TB_REC = 512
FFN_TF = 512
FFN_TN = 512

_NN = (((1,), (0,)), ((), ()))
_NT = (((1,), (1,)), ((), ()))
_TN = (((0,), (0,)), ((), ()))


def _dot(a, b, dims=_NN):
    return lax.dot_general(a, b, dims, preferred_element_type=F32)


def _split2(x):
    hi = x.astype(BF16)
    return hi, (x - hi.astype(F32)).astype(BF16)


def _dot_exact_lhs(a_bf, x, dims=_NN):
    return sum(_dot(a_bf, p, dims) for p in _split2(x))


def _dot_exact_rhs(x, b_bf, dims=_NN):
    return sum(_dot(p, b_bf, dims) for p in _split2(x))


def _sigmoid(x):
    return 0.5 + 0.5 * jnp.tanh(0.5 * x)


def _silu(x):
    hx = 0.5 * x
    return hx + hx * jnp.tanh(hx)


def _log_sigmoid(x):
    return jnp.minimum(x, 0.0) - jnp.log(1.0 + jnp.exp(-jnp.abs(x)))


def _params(n_grid):
    return pltpu.CompilerParams(dimension_semantics=("arbitrary",) * n_grid,
                                vmem_limit_bytes=VMEM_LIMIT_BYTES)


def _resident(shape):
    nd = len(shape)
    return pl.BlockSpec(shape, lambda *_: (0,) * nd, pipeline_mode=pl.Buffered(1))


def _gain_folded(w, g):
    return (g[:, None] * w).astype(BF16)


def _rms_split(x_ref):
    x = x_ref[...]
    return x.astype(BF16), lax.rsqrt(jnp.mean(x * x, axis=-1, keepdims=True) + EPS)


def _norm_matmul_kernel(x_ref, w_ref, out_ref):
    x_bf, rs = _rms_split(x_ref)
    out_ref[...] = _dot(x_bf, w_ref[...]) * rs


def _norm_matmul(x, wg_bf, tm):
    t, d = x.shape
    n = wg_bf.shape[1]
    return pl.pallas_call(
        _norm_matmul_kernel,
        grid=(t // tm,),
        in_specs=[pl.BlockSpec((tm, d), lambda i: (i, 0)),
                  _resident((d, n))],
        out_specs=pl.BlockSpec((tm, n), lambda i: (i, 0)),
        out_shape=jax.ShapeDtypeStruct((t, n), F32),
        compiler_params=_params(1),
        name="norm_matmul",
    )(x, wg_bf)


def _proj_mlstm_kernel(x_ref, w_ref, wx_ref, wg_ref, cw_ref, cb_ref, q_out, k_out, v_out, o_out, xq_out, gt_out,
                       cbuf, *, tm, per_seq):
    i = pl.program_id(0)
    qk_w = 2 * A_HEADS * A_QK_DIM

    @pl.when(i % per_seq == 0)
    def _():
        cbuf[0:SUBLANES] = jnp.zeros((SUBLANES, qk_w), F32)

    x_bf, rs = _rms_split(x_ref)

    def proj(off, width):
        return _dot(x_bf, w_ref[:, off:off + width]) * rs

    cbuf[SUBLANES:SUBLANES + tm] = proj(0, qk_w)
    acc = cb_ref[...]
    for j in range(A_CONV):
        lo = SUBLANES - (A_CONV - 1) + j
        acc = acc + cw_ref[j:j + 1, :] * cbuf[lo:lo + tm, :]
    y = _silu(acc)
    cbuf[0:SUBLANES] = cbuf[tm:tm + SUBLANES]
    pad = jnp.zeros((tm, A_QK_PAD - A_QK_DIM), F32)
    for hh in range(A_HEADS):
        qh = y[:, hh * A_QK_DIM:(hh + 1) * A_QK_DIM]
        kh = y[:, (A_HEADS + hh) * A_QK_DIM:(A_HEADS + hh + 1) * A_QK_DIM] * (A_QK_DIM ** -0.5)
        q_out[:, hh * A_QK_PAD:(hh + 1) * A_QK_PAD] = jnp.concatenate([qh, pad], axis=1).astype(BF16)
        k_out[:, hh * A_QK_PAD:(hh + 1) * A_QK_PAD] = jnp.concatenate([kh, pad], axis=1).astype(BF16)
    off = qk_w
    v_out[...] = proj(off, MIX_WIDTH).astype(BF16)
    off += MIX_WIDTH
    o_out[...] = _sigmoid(proj(off, MIX_WIDTH))
    xq_out[...] = (_dot(x_bf, wx_ref[...]) * rs).astype(BF16)
    gt_out[...] = _dot(x_bf, wg_ref[...]) * rs


def _proj_mlstm(x, w_bf, conv_w, conv_b, seq, tm):
    t, d = x.shape
    qk_w = 2 * A_HEADS * A_QK_DIM
    g0 = qk_w + 2 * MIX_WIDTH
    n_gates = 2 * A_HEADS
    assert w_bf.shape[1] == g0 + n_gates + XA_WIDTH
    w_g = jnp.pad(w_bf[:, g0:g0 + n_gates], ((0, 0), (0, LANES - n_gates)))
    w_x = w_bf[:, g0 + n_gates:]
    widths = (A_HEADS * A_QK_PAD, A_HEADS * A_QK_PAD, MIX_WIDTH, MIX_WIDTH, XA_WIDTH, LANES)
    dtypes = (BF16, BF16, BF16, F32, BF16, F32)
    return pl.pallas_call(
        functools.partial(_proj_mlstm_kernel, tm=tm, per_seq=seq // tm),
        grid=(t // tm,),
        in_specs=[pl.BlockSpec((tm, d), lambda i: (i, 0)),
                  _resident((d, g0)),
                  _resident((d, XA_WIDTH)),
                  _resident((d, LANES)),
                  _resident((A_CONV, qk_w)),
                  _resident((1, qk_w))],
        out_specs=[pl.BlockSpec((tm, s), lambda i: (i, 0)) for s in widths],
        out_shape=[jax.ShapeDtypeStruct((t, s), dt) for s, dt in zip(widths, dtypes)],
        scratch_shapes=[pltpu.VMEM((tm + SUBLANES, qk_w), F32)],
        compiler_params=_params(1),
        name="proj_mlstm",
    )(x, w_bf, w_x, w_g, conv_w, conv_b)


def _proj_hgrn2_kernel(x_ref, w_ref, lbl_ref, q_out, lf_out, kk_out, v_out, sg_out, xq_out, *, layer):
    lg = lbl_ref[...]
    lg = lg - jnp.max(lg, axis=0, keepdims=True)
    pe = jnp.exp(lg)
    pr = pe / jnp.sum(pe, axis=0, keepdims=True)
    c0 = pr[0:1, :]
    cl = c0
    for r in range(1, layer + 1):
        cl = cl + pr[r:r + 1, :]
    lb = cl - c0
    om = 1.0 - lb

    x_bf, rs = _rms_split(x_ref)
    kw = B_HEADS * B_K_DIM

    def proj(off, width):
        return _dot(x_bf, w_ref[:, off:off + width]) * rs

    q_out[...] = _silu(proj(0, kw))
    fz = proj(kw, kw)
    sig = 1.0 / (1.0 + jnp.exp(-fz))
    gate = lb + om * sig
    lf_out[...] = jnp.where(gate > F32_TINY, jnp.log2(gate), jnp.log2(om) + fz * LOG2_E)
    kk_out[...] = om * (1.0 - sig)
    off = 2 * kw
    v_out[...] = proj(off, MIX_WIDTH).astype(BF16)
    off += MIX_WIDTH
    sg_out[...] = _silu(proj(off, MIX_WIDTH))
    off += MIX_WIDTH
    xq_out[...] = proj(off, XA_WIDTH).astype(BF16)


def _proj_hgrn2(x, w_bf, lb_logits, tm, layer):
    t, d = x.shape
    n = w_bf.shape[1]
    kw = B_HEADS * B_K_DIM
    widths = (kw, kw, kw, MIX_WIDTH, MIX_WIDTH, XA_WIDTH)
    dtypes = (F32, F32, F32, BF16, F32, BF16)
    assert 2 * kw + 2 * MIX_WIDTH + XA_WIDTH == n
    return pl.pallas_call(
        functools.partial(_proj_hgrn2_kernel, layer=layer),
        grid=(t // tm,),
        in_specs=[pl.BlockSpec((tm, d), lambda i: (i, 0)),
                  _resident((d, n)),
                  _resident(lb_logits.shape)],
        out_specs=[pl.BlockSpec((tm, s), lambda i: (i, 0)) for s in widths],
        out_shape=[jax.ShapeDtypeStruct((t, s), dt) for s, dt in zip(widths, dtypes)],
        compiler_params=_params(1),
        name="proj_hgrn2",
    )(x, w_bf, lb_logits)


def _mlstm_kernel(q_ref, k_ref, v_ref, o_ref, gt_ref, gb_ref, hg_ref, tril_ref, out_ref, c_st, m_st, *, tb, hb):
    hgrp = pl.program_id(1)
    t = pl.program_id(2)
    aug_w = A_V_DIM + A_AUG
    rep = aug_w // LANES

    @pl.when(t == 0)
    def _():
        c_st[...] = jnp.zeros_like(c_st)
        m_st[...] = jnp.zeros_like(m_st)

    tril = tril_ref[...]
    row = lax.broadcasted_iota(jnp.int32, (A_CHUNK, A_CHUNK), 0)
    col = lax.broadcasted_iota(jnp.int32, (A_CHUNK, A_CHUNK), 1)
    causal = row >= col
    wide = hb * LANES
    srow = lax.broadcasted_iota(jnp.int32, (LANES, 2 * wide), 0)
    scol = lax.broadcasted_iota(jnp.int32, (LANES, 2 * wide), 1)
    blk = scol // LANES
    want = jnp.where(blk < hb, hgrp * hb + blk, A_HEADS + hgrp * hb + (blk - hb))
    sel = jnp.where(srow == want, 1.0, 0.0).astype(BF16)
    ones_aug = jnp.ones((A_CHUNK, A_AUG), BF16)
    heads = range(hb)

    def lanes_of(hh):
        return slice(hh * LANES, (hh + 1) * LANES)

    qk_lanes = [slice(hh * A_QK_PAD, hh * A_QK_PAD + A_QK_DIM) for hh in heads]
    v_lanes = [slice(hh * A_V_DIM, (hh + 1) * A_V_DIM) for hh in heads]

    def chunk_group(c, carry):
        group = range(A_CHUNKS_PER_ITER)
        rows = [pl.ds(pl.multiple_of((c * A_CHUNKS_PER_ITER + i) * A_CHUNK, A_CHUNK), A_CHUNK) for i in group]
        gates = [gt_ref[r, :] + gb_ref[...] for r in rows]
        g2 = [_dot_exact_rhs(gates[i], sel) for i in group]
        ic = [g2[i][:, :wide] for i in group]
        lf = [_log_sigmoid(g2[i][:, wide:]) for i in group]
        gc = [_dot_exact_lhs(tril, lf[i]) for i in group]
        r_mat = [[(ic[i] - gc[i])[:, lanes_of(hh)].T for hh in heads] for i in group]
        dmat = [[jnp.where(causal, gc[i][:, hh * LANES:hh * LANES + A_CHUNK] + r_mat[i][hh], -jnp.inf)
                 for hh in heads] for i in group]
        mx = [jnp.concatenate([jnp.broadcast_to(jnp.max(dmat[i][hh], axis=1, keepdims=True), (A_CHUNK, LANES))
                               for hh in heads], axis=1) for i in group]

        m_prev = m_st[0:1, :]
        w_inter, e_neg, p, decay, ws = [], [], [], [], []
        for i in group:
            a = gc[i] + m_prev
            m_row = jnp.maximum(a, mx[i])
            w_inter.append(jnp.exp(a - m_row))
            e_neg.append(jnp.exp(-m_row))
            p.append([jnp.exp(dmat[i][hh] - m_row[:, hh * LANES:hh * LANES + A_CHUNK]) for hh in heads])
            g_end = gc[i][A_CHUNK - 1:A_CHUNK, :]
            a_end = g_end + m_prev
            w_end = g_end - gc[i] + ic[i]
            m_prev = jnp.maximum(a_end, jnp.max(w_end, axis=0, keepdims=True))
            decay.append(jnp.exp(a_end - m_prev))
            ws.append(jnp.exp(w_end - m_prev).astype(BF16))
        m_st[...] = jnp.broadcast_to(m_prev, (SUBLANES, wide))

        q = [[q_ref[r, qk_lanes[hh]] for hh in heads] for r in rows]
        k = [[k_ref[r, qk_lanes[hh]] for hh in heads] for r in rows]
        v_aug = [[jnp.concatenate([v_ref[r, v_lanes[hh]], ones_aug], axis=1) for hh in heads] for r in rows]
        qk = [[_dot(q[i][hh], k[i][hh], _NT) for hh in heads] for i in group]
        intra = [[_dot((qk[i][hh] * p[i][hh]).astype(BF16), v_aug[i][hh]) for hh in heads] for i in group]
        upd = [[_dot(k[i][hh] * jnp.concatenate([ws[i][:, lanes_of(hh)]] * (A_QK_PAD // LANES), axis=1)[:, :A_QK_DIM],
                     v_aug[i][hh], _TN) for hh in heads] for i in group]
        state = [c_st[hh] for hh in heads]
        inter = []
        for i in group:
            inter.append([_dot(q[i][hh], state[hh].astype(BF16)) for hh in heads])
            state = [jnp.concatenate([decay[i][:, lanes_of(hh)]] * rep, axis=1) * state[hh] + upd[i][hh]
                     for hh in heads]
        for hh in heads:
            c_st[hh] = state[hh]
        for i in group:
            for hh in heads:
                tot = jnp.concatenate([w_inter[i][:, lanes_of(hh)]] * rep, axis=1) * inter[i][hh] + intra[i][hh]
                den = tot[:, A_V_DIM:]
                inv = 1.0 / jnp.maximum(jnp.abs(den), e_neg[i][:, lanes_of(hh)])
                h_out = tot[:, :A_V_DIM] * jnp.concatenate([inv] * (A_V_DIM // LANES), axis=1)
                ms = jnp.mean(h_out * h_out, axis=-1, keepdims=True)
                hn = h_out * lax.rsqrt(ms + EPS) * hg_ref[:, v_lanes[hh]]
                out_ref[rows[i], v_lanes[hh]] = (o_ref[rows[i], v_lanes[hh]] * hn).astype(out_ref.dtype)
        return carry

    lax.fori_loop(0, tb // (A_CHUNK * A_CHUNKS_PER_ITER), chunk_group, 0)


def _mlstm(zq, zk, zv, zo, zg, gate_b, head_g, batch, seq, tb, hb):
    t_total = batch * seq
    nt = seq // tb
    assert tb % (A_CHUNK * A_CHUNKS_PER_ITER) == 0
    tril = jnp.asarray(np.tril(np.ones((A_CHUNK, A_CHUNK), np.float32)), BF16)
    row_map = lambda b, h, t: (b * nt + t, h)
    head_map = lambda b, h, t: (0, h)
    fixed = lambda b, h, t: (0, 0)
    return pl.pallas_call(
        functools.partial(_mlstm_kernel, tb=tb, hb=hb),
        grid=(batch, A_HEADS // hb, nt),
        in_specs=[pl.BlockSpec((tb, hb * A_QK_PAD), row_map),
                  pl.BlockSpec((tb, hb * A_QK_PAD), row_map),
                  pl.BlockSpec((tb, hb * A_V_DIM), row_map),
                  pl.BlockSpec((tb, hb * A_V_DIM), row_map),
                  pl.BlockSpec((tb, LANES), lambda b, h, t: (b * nt + t, 0)),
                  pl.BlockSpec((1, LANES), fixed),
                  pl.BlockSpec((1, hb * A_V_DIM), head_map),
                  pl.BlockSpec((A_CHUNK, A_CHUNK), fixed)],
        out_specs=pl.BlockSpec((tb, hb * A_V_DIM), row_map),
        out_shape=jax.ShapeDtypeStruct((t_total, MIX_WIDTH), BF16),
        scratch_shapes=[pltpu.VMEM((hb, A_QK_DIM, A_V_DIM + A_AUG), F32),
                        pltpu.VMEM((SUBLANES, hb * LANES), F32)],
        compiler_params=_params(3),
        name="mlstm",
    )(zq, zk, zv, zo, zg, gate_b, head_g, tril)


def _hgrn2_pair_masks():
    n = CHUNK
    masks = [np.eye(n, dtype=np.float32)]
    for j in range(1, B_LEVELS + 1):
        c = n >> j
        pm = np.zeros((n, n), np.float32)
        for r in range(n):
            if r % (2 * c) >= c:
                mid = (r // (2 * c)) * 2 * c + c
                pm[r, mid - c:mid] = 1.0
        masks.append(pm)
    return np.stack(masks, axis=0)


def _hgrn2_kernel(q_ref, lf_ref, kk_ref, v_ref, sg_ref, hg_ref, tril_ref, mask_ref, out_ref, s_st, *, tb, hb):
    t = pl.program_id(2)

    @pl.when(t == 0)
    def _():
        s_st[...] = jnp.zeros_like(s_st)

    tril = tril_ref[...]
    wide = hb * B_K_DIM
    heads = range(hb)
    rowi = lax.broadcasted_iota(jnp.int32, (CHUNK, wide), 0)
    sub = lax.broadcasted_iota(jnp.int32, (CHUNK // SUBLANES, SUBLANES, wide), 1)

    def level_operand(q, kk, gc, lf, half):
        if half >= SUBLANES:
            pieces = []
            for b0 in range(0, CHUNK, 2 * half):
                mid = b0 + half
                ref = gc[mid - 1:mid, :]
                pieces.append(kk[b0:mid, :] * jnp.exp2(ref - gc[b0:mid, :]))
                pieces.append(q[mid:mid + half, :] * jnp.exp2(gc[mid:mid + half, :] - ref))
            return jnp.concatenate(pieces, axis=0)
        if half == 1:
            odd = (rowi & 1) != 0
            return jnp.where(odd, q * jnp.exp2(lf), kk)
        g3 = gc.reshape(CHUNK // SUBLANES, SUBLANES, wide)
        ref = g3[:, half - 1:half, :]
        for b0 in range(2 * half, SUBLANES, 2 * half):
            ref = jnp.where(sub >= b0, g3[:, b0 + half - 1:b0 + half, :], ref)
        d = g3 - ref
        upper = (sub & half) != 0
        ex = jnp.exp2(jnp.where(upper, d, -d)).reshape(CHUNK, wide)
        return jnp.where((rowi & half) != 0, q, kk) * ex

    def hl(x, hh):
        return x[:, hh * B_K_DIM:(hh + 1) * B_K_DIM]

    def coarse_operands(q, kk, gc):
        q_ops, k_ops = [], []
        for half in B_MERGED_HALVES:
            pad = jnp.zeros((half, wide), F32)
            for b0 in range(0, CHUNK, 2 * half):
                mid = b0 + half
                ref = gc[mid - 1:mid, :]
                k_rows = kk[b0:mid, :] * jnp.exp2(ref - gc[b0:mid, :])
                q_rows = q[mid:mid + half, :] * jnp.exp2(gc[mid:mid + half, :] - ref)
                for ops, block in ((k_ops, [k_rows, pad]), (q_ops, [pad, q_rows])):
                    parts = [jnp.concatenate(block, axis=0).astype(BF16)]
                    if b0:
                        parts.insert(0, jnp.zeros((b0, wide), BF16))
                    if mid + half < CHUNK:
                        parts.append(jnp.zeros((CHUNK - mid - half, wide), BF16))
                    ops.append(jnp.concatenate(parts, axis=0) if len(parts) > 1 else parts[0])
        return q_ops, k_ops

    def chunk_group(c, carry):
        group = range(B_CHUNKS_PER_ITER)
        rows = [pl.ds(pl.multiple_of((c * B_CHUNKS_PER_ITER + i) * CHUNK, CHUNK), CHUNK) for i in group]
        q = [q_ref[r, :] for r in rows]
        lf = [lf_ref[r, :] for r in rows]
        kk = [kk_ref[r, :] for r in rows]
        v = [v_ref[r, :] for r in rows]
        gc = [_dot_exact_lhs(tril, lf[i]) for i in group]
        g_end = [gc[i][CHUNK - 1:CHUNK, :] for i in group]

        masked_levels = range(len(B_MERGED_HALVES) + 1, B_LEVELS + 1)
        xs = [[level_operand(q[i], kk[i], gc[i], lf[i], CHUNK >> j).astype(BF16) for j in masked_levels]
              for i in group]
        coarse = [coarse_operands(q[i], kk[i], gc[i]) for i in group]
        q_bf = [q[i].astype(BF16) for i in group]
        k_bf = [kk[i].astype(BF16) for i in group]
        qg = [(q[i] * jnp.exp2(gc[i])).astype(BF16) for i in group]
        kd = [(kk[i] * jnp.exp2(g_end[i] - gc[i])).astype(BF16) for i in group]
        dec = [jnp.exp2(g_end[i]) for i in group]

        prod = [[[_dot(hl(q_bf[i], hh), hl(k_bf[i], hh), _NT)] + [_dot(hl(x, hh), hl(x, hh), _NT) for x in xs[i]]
                 for hh in heads] for i in group]
        merged = [[_dot(jnp.concatenate([hl(a, hh) for a in coarse[i][0]], axis=1),
                        jnp.concatenate([hl(a, hh) for a in coarse[i][1]], axis=1), _NT)
                   for hh in heads] for i in group]
        upd = [[_dot(hl(v[i], hh), hl(kd[i], hh), _TN) for hh in heads] for i in group]
        state = [s_st[hh] for hh in heads]
        o_inter = []
        for i in group:
            o_inter.append([_dot(hl(qg[i], hh), state[hh].astype(BF16), _NT) for hh in heads])
            state = [state[hh] * hl(dec[i], hh) + upd[i][hh] for hh in heads]
        for hh in heads:
            s_st[hh] = state[hh]
        for i in group:
            outs = []
            for hh in heads:
                acc = mask_ref[0] * prod[i][hh][0] + merged[i][hh]
                for n, j in enumerate(masked_levels):
                    acc = acc + mask_ref[j] * prod[i][hh][1 + n]
                o = _dot(acc.astype(BF16), hl(v[i], hh)) + o_inter[i][hh]
                ms = jnp.mean(o * o, axis=-1, keepdims=True)
                outs.append(o * lax.rsqrt(ms + EPS))
            on = jnp.concatenate(outs, axis=1) * hg_ref[...]
            out_ref[rows[i], :] = (on * sg_ref[rows[i], :]).astype(out_ref.dtype)
        return carry

    lax.fori_loop(0, tb // (CHUNK * B_CHUNKS_PER_ITER), chunk_group, 0)


def _hgrn2(zq, zlf, zkk, zv, zsg, head_g, batch, seq, tb, hb):
    t_total = batch * seq
    nt = seq // tb
    assert tb % (CHUNK * B_CHUNKS_PER_ITER) == 0
    mask_np = _hgrn2_pair_masks()
    tril = jnp.asarray(np.tril(np.ones((CHUNK, CHUNK), np.float32)), BF16)
    row_map = lambda b, h, t: (b * nt + t, h)
    head_map = lambda b, h, t: (0, h)
    return pl.pallas_call(
        functools.partial(_hgrn2_kernel, tb=tb, hb=hb),
        grid=(batch, B_HEADS // hb, nt),
        in_specs=[pl.BlockSpec((tb, hb * B_K_DIM), row_map),
                  pl.BlockSpec((tb, hb * B_K_DIM), row_map),
                  pl.BlockSpec((tb, hb * B_K_DIM), row_map),
                  pl.BlockSpec((tb, hb * B_V_DIM), row_map),
                  pl.BlockSpec((tb, hb * B_V_DIM), row_map),
                  pl.BlockSpec((1, hb * B_V_DIM), head_map),
                  pl.BlockSpec((CHUNK, CHUNK), lambda b, h, t: (0, 0)),
                  pl.BlockSpec(mask_np.shape, lambda b, h, t: (0, 0, 0))],
        out_specs=pl.BlockSpec((tb, hb * B_V_DIM), row_map),
        out_shape=jax.ShapeDtypeStruct((t_total, MIX_WIDTH), BF16),
        scratch_shapes=[pltpu.VMEM((hb, B_V_DIM, B_K_DIM), F32)],
        compiler_params=_params(3),
        name="hgrn2",
    )(zq, zlf, zkk, zv, zsg, head_g, tril, jnp.asarray(mask_np, F32))


def _outproj_kernel(y_ref, xq_ref, kv_ref, wo_ref, h_ref, out_ref, hn_ref, rs_ref):
    kv = kv_ref[...]
    parts = []
    for hh in range(XA_HEADS):
        lo = hh * XA_HEAD_DIM
        qh = xq_ref[:, lo:lo + XA_HEAD_DIM]
        kh = kv[:, lo:lo + XA_HEAD_DIM].astype(BF16)
        vh = kv[:, XA_WIDTH + lo:XA_WIDTH + lo + XA_HEAD_DIM].astype(BF16)
        s = _dot(qh, kh, _NT) * (XA_HEAD_DIM ** -0.5)
        e = jnp.exp(s - jnp.max(s, axis=-1, keepdims=True))
        den = jnp.sum(e, axis=-1, keepdims=True)
        parts.append((_dot(e.astype(BF16), vh) / den).astype(BF16))
    y_mem = jnp.concatenate(parts, axis=1)
    h_new = (h_ref[...] + _dot(y_ref[...], wo_ref[0:MIX_WIDTH, :])
             + _dot(y_mem, wo_ref[MIX_WIDTH:D_MODEL, :]))
    out_ref[...] = h_new
    hn_ref[...] = h_new.astype(BF16)
    rs = lax.rsqrt(jnp.mean(h_new * h_new, axis=-1, keepdims=True) + EPS)
    rs_ref[...] = jnp.broadcast_to(rs, rs_ref.shape)


def _outproj(y_mix, zxq, kv_all, layer, w_out_bf, h, seq, mem_len, tm):
    t = h.shape[0]
    per_seq = seq // tm
    return pl.pallas_call(
        _outproj_kernel,
        grid=(t // tm,),
        in_specs=[pl.BlockSpec((tm, MIX_WIDTH), lambda i: (i, 0)),
                  pl.BlockSpec((tm, XA_WIDTH), lambda i: (i, 0)),
                  pl.BlockSpec((mem_len, 2 * XA_WIDTH), lambda i: (i // per_seq, layer)),
                  _resident((D_MODEL, D_MODEL)),
                  pl.BlockSpec((tm, D_MODEL), lambda i: (i, 0))],
        out_specs=[pl.BlockSpec((tm, D_MODEL), lambda i: (i, 0)),
                   pl.BlockSpec((tm, D_MODEL), lambda i: (i, 0)),
                   pl.BlockSpec((tm, LANES), lambda i: (i, 0))],
        out_shape=[jax.ShapeDtypeStruct((t, D_MODEL), F32),
                   jax.ShapeDtypeStruct((t, D_MODEL), BF16),
                   jax.ShapeDtypeStruct((t, LANES), F32)],
        compiler_params=_params(1),
        name="outproj",
    )(y_mix, zxq, kv_all, w_out_bf, h)


def _ffn_kernel(hn_ref, rs_ref, h_ref, wu_ref, wg_ref, cw_ref, cb_ref, wd_ref, out_ref,
                gbuf, gcarry, act_s, *, tm, nf, per_seq):
    i = pl.program_id(0)
    j = pl.program_id(1)
    tf = act_s.shape[2]

    @pl.when(j < nf)
    def _():
        hn = hn_ref[...]
        rs = jnp.concatenate([rs_ref[...]] * (tf // LANES), axis=1)
        g = _dot(hn, wg_ref[...]) * rs
        prev = jnp.where(i % per_seq == 0, 0.0, gcarry[j])
        gbuf[0:SUBLANES] = prev
        gbuf[SUBLANES:SUBLANES + tm] = g
        gcarry[j] = g[tm - SUBLANES:tm, :]
        gc = cb_ref[...]
        for k in range(FFN_CONV):
            lo = SUBLANES - (FFN_CONV - 1) + k
            gc = gc + cw_ref[k:k + 1, :] * gbuf[lo:lo + tm, :]
        sg = _silu(gc) * rs
        u = _dot(hn, wu_ref[...])
        act_s[j] = (sg * u).astype(BF16)

    @pl.when(j >= nf)
    def _():
        act = jnp.concatenate([act_s[f] for f in range(nf)], axis=1)
        out_ref[...] = h_ref[...] + _dot(act, wd_ref[...])


def _ffn(hn, rs, h, w_up_bf, conv_w, conv_b, w_down_all_bf, layer, seq, tm, tf, tn):
    t, d = h.shape
    nf = D_FF // tf
    nd = d // tn
    per_seq = seq // tm
    up = lambda j: jnp.minimum(j, nf - 1)
    down = lambda j: jnp.maximum(j - nf, 0)
    return pl.pallas_call(
        functools.partial(_ffn_kernel, tm=tm, nf=nf, per_seq=per_seq),
        grid=(t // tm, nf + nd),
        in_specs=[pl.BlockSpec((tm, d), lambda i, j: (i, 0)),
                  pl.BlockSpec((tm, LANES), lambda i, j: (i, 0)),
                  pl.BlockSpec((tm, tn), lambda i, j: (i, down(j))),
                  pl.BlockSpec((d, tf), lambda i, j: (0, up(j))),
                  pl.BlockSpec((d, tf), lambda i, j: (0, nf + up(j))),
                  pl.BlockSpec((FFN_CONV, tf), lambda i, j: (0, up(j))),
                  pl.BlockSpec((1, tf), lambda i, j: (0, up(j))),
                  pl.BlockSpec((None, D_FF, tn), lambda i, j: (layer, 0, down(j)))],
        out_specs=pl.BlockSpec((tm, tn), lambda i, j: (i, down(j))),
        out_shape=jax.ShapeDtypeStruct((t, d), F32),
        scratch_shapes=[pltpu.VMEM((tm + SUBLANES, tf), F32),
                        pltpu.VMEM((nf, SUBLANES, tf), F32),
                        pltpu.VMEM((nf, tm, tf), BF16)],
        compiler_params=_params(2),
        name="ffn",
    )(hn, rs, h, w_up_bf, w_up_bf, conv_w, conv_b.reshape(1, D_FF), w_down_all_bf)


def _final_norm_kernel(x_ref, g_ref, out_ref):
    x = x_ref[...]
    out_ref[...] = x * lax.rsqrt(jnp.mean(x * x, axis=-1, keepdims=True) + EPS) * g_ref[...]


def _final_norm(x, g, tm):
    t, d = x.shape
    return pl.pallas_call(
        _final_norm_kernel,
        grid=(t // tm,),
        in_specs=[pl.BlockSpec((tm, d), lambda i: (i, 0)),
                  pl.BlockSpec((1, d), lambda i: (0, 0))],
        out_specs=pl.BlockSpec((tm, d), lambda i: (i, 0)),
        out_shape=jax.ShapeDtypeStruct((t, d), F32),
        compiler_params=_params(1),
        name="final_norm",
    )(x, g.reshape(1, d))


def _pick(n, cap):
    b = min(n, cap)
    while n % b:
        b //= 2
    return b


def kernel(x, mem, norm_mix_g, norm_mem_g, norm_ffn_g, norm_out_g, w_mem_kv, a_w_in, a_gate_b, a_conv_w, a_conv_b, a_head_g, a_w_out, b_w_in, b_lb_logits, b_head_g, b_w_out, ffn_w_up, ffn_conv_w, ffn_conv_b, ffn_w_down):
    batch, seq, d = x.shape
    mem_len = mem.shape[1]
    depth = norm_mix_g.shape[0]
    t = batch * seq
    tm_proj = _pick(seq, TM_PROJ)
    tm_out = _pick(seq, TM_OUT)
    tm_ffn = _pick(seq, TM_FFN)
    tb = _pick(seq, TB_REC)

    h = x.reshape(t, d)
    w_kv_all = jnp.concatenate([_gain_folded(w_mem_kv[layer], norm_mem_g[layer]) for layer in range(depth)], axis=1)
    kv_all = _norm_matmul(mem.reshape(batch * mem_len, d), w_kv_all, _pick(mem_len, TM_PROJ))
    w_down_all = ffn_w_down.astype(BF16)
    for layer in range(depth):
        j = layer // N_MIXERS
        if layer % N_MIXERS == 0:
            zq, zk, zv, zo, zxq, zg = _proj_mlstm(h, _gain_folded(a_w_in[j], norm_mix_g[layer]), a_conv_w[j],
                                                  a_conv_b[j][None, :], seq, tm_proj)
            gb = jnp.pad(a_gate_b[j][None, :], ((0, 0), (0, LANES - 2 * A_HEADS)))
            y_mix = _mlstm(zq, zk, zv, zo, zg, gb, a_head_g[j].reshape(1, MIX_WIDTH), batch, seq, tb,
                           A_HEADS_PER_STEP)
            w_out = a_w_out[j]
        else:
            zq, zlf, zkk, zv, zsg, zxq = _proj_hgrn2(h, _gain_folded(b_w_in[j], norm_mix_g[layer]), b_lb_logits,
                                                     tm_proj, layer)
            y_mix = _hgrn2(zq, zlf, zkk, zv, zsg, b_head_g[j].reshape(1, MIX_WIDTH), batch, seq, tb,
                           B_HEADS_PER_STEP)
            w_out = b_w_out[j]
        h, hn, rs = _outproj(y_mix, zxq, kv_all, layer, w_out.astype(BF16), h, seq, mem_len, tm_out)
        h = _ffn(hn, rs, h, _gain_folded(ffn_w_up[layer], norm_ffn_g[layer]), ffn_conv_w[layer], ffn_conv_b[layer],
                 w_down_all, layer, seq, tm_ffn, FFN_TF, FFN_TN)
    return _final_norm(h, norm_out_g, tm_out).reshape(batch, seq, d)
```
